```python
import math
import jax, jax.numpy as jnp
from jax import lax
import numpy as np

D_MODEL = 1024
BATCH = 4
SEQ = 4096
DEPTH = 2

HEAD_DIM = 64
N_SB_HEADS = 8
N_MOBA_HEADS = 8
N_ATT_HEADS = N_SB_HEADS + N_MOBA_HEADS
ATT_W = N_ATT_HEADS * HEAD_DIM
SB_QBLOCK = 128
MOBA_BLOCK = 256
MOBA_TOPK = 3
MOBA_QBLOCK = 32
N_RET_HEADS = 4
RET_DK = D_MODEL // N_RET_HEADS
RET_DV = 2 * RET_DK
RET_QK_W = N_RET_HEADS * RET_DK
RET_V_W = N_RET_HEADS * RET_DV
RET_CHUNK = 128
D_FF = -(-8 * D_MODEL // 768) * 256
ROPE_THETA = 10000.0
NORM_EPS = 1e-6
GN_EPS = 1e-5
NEG = -1e30
N_EVEN = (DEPTH + 1) // 2
N_ODD = DEPTH // 2

kernel_name = "hybrid_stickbreak_moba_retention_adaln"


def rms_norm(x, g):
    xf = x.astype(jnp.float32)
    y = xf * lax.rsqrt(jnp.mean(xf * xf, axis=-1, keepdims=True) + NORM_EPS)
    return (y * g.astype(jnp.float32)).astype(x.dtype)


def rotary(x, pos):
    d = x.shape[-1]
    inv = ROPE_THETA ** (-jnp.arange(0, d, 2, dtype=jnp.float32) / d)
    ang = pos.astype(jnp.float32)[:, None] * inv[None, :]
    cos, sin = jnp.cos(ang), jnp.sin(ang)
    xf = x.astype(jnp.float32)
    x1, x2 = xf[..., : d // 2], xf[..., d // 2:]
    out = jnp.concatenate([x1 * cos - x2 * sin, x2 * cos + x1 * sin], axis=-1)
    return out.astype(x.dtype)


def ada_mod(c, w, b):
    m = jax.nn.silu(c) @ w + b
    return jnp.split(m[:, None, :], 6, axis=-1)


def modulate(h, shift, scale):
    return h * (1 + scale) + shift


def stick_breaking_attention(q, k, v):
    B, H, S, D = q.shape
    nblk = S // SB_QBLOCK
    kpos = jnp.arange(S)
    scale = D ** -0.5

    def block(i):
        start = i * SB_QBLOCK
        qb = lax.dynamic_slice_in_dim(q, start, SB_QBLOCK, axis=2)
        z = jnp.einsum('bhqd,bhkd->bhqk', qb, k).astype(jnp.float32) * scale
        qpos = start + jnp.arange(SB_QBLOCK)
        strict = kpos[None, :] < qpos[:, None]
        log_1mb = jnp.where(strict, jax.nn.log_sigmoid(-z), 0.0)
        after = lax.cumsum(log_1mb, axis=3, reverse=True) - log_1mb
        a = jnp.where(strict, jnp.exp(jax.nn.log_sigmoid(z) + after), 0.0)
        return jnp.einsum('bhqk,bhkd->bhqd', a.astype(v.dtype), v)

    out = lax.map(block, jnp.arange(nblk))
    return jnp.moveaxis(out, 0, 2).reshape(B, H, S, D)


def moba_attention(q, k, v):
    B, H, S, D = q.shape
    nb = -(-S // MOBA_BLOCK)
    spad = nb * MOBA_BLOCK
    pad = ((0, 0), (0, 0), (0, spad - S), (0, 0))
    kp = jnp.pad(k, pad)
    vp = jnp.pad(v, pad)
    kb = kp.reshape(B, H, nb, MOBA_BLOCK, D)
    vb = vp.reshape(B, H, nb, MOBA_BLOCK, D)
    kmean = jnp.mean(kb.astype(jnp.float32), axis=3).astype(k.dtype)
    kk = min(MOBA_TOPK, nb)
    scale = D ** -0.5
    nqb = S // MOBA_QBLOCK
    bi = jnp.arange(B)[:, None, None, None]
    hi = jnp.arange(H)[None, :, None, None]
    blk_ids = jnp.arange(nb)
    kpos_in = jnp.arange(MOBA_BLOCK)

    def block(i):
        start = i * MOBA_QBLOCK
        qb = lax.dynamic_slice_in_dim(q, start, MOBA_QBLOCK, axis=2)
        qpos = start + jnp.arange(MOBA_QBLOCK)
        own = start // MOBA_BLOCK
        gate = jnp.einsum('bhqd,bhnd->bhqn', qb, kmean).astype(jnp.float32)
        gate = jnp.where(blk_ids < own, gate, NEG)
        _, idx = lax.top_k(gate, kk)
        sel_valid = idx < own
        ksel = kb[bi, hi, idx]
        vsel = vb[bi, hi, idx]
        s_past = jnp.einsum('bhqd,bhqnkd->bhqnk', qb, ksel).astype(jnp.float32) * scale
        s_past = jnp.where(sel_valid[..., None], s_past, NEG)
        s_past = s_past.reshape(B, H, MOBA_QBLOCK, kk * MOBA_BLOCK)
        k_own = lax.dynamic_slice_in_dim(kp, own * MOBA_BLOCK, MOBA_BLOCK, axis=2)
        v_own = lax.dynamic_slice_in_dim(vp, own * MOBA_BLOCK, MOBA_BLOCK, axis=2)
        s_own = jnp.einsum('bhqd,bhkd->bhqk', qb, k_own).astype(jnp.float32) * scale
        kpos_own = own * MOBA_BLOCK + kpos_in
        s_own = jnp.where(kpos_own[None, :] <= qpos[:, None], s_own, NEG)
        p = jax.nn.softmax(jnp.concatenate([s_past, s_own], axis=-1), axis=-1).astype(v.dtype)
        p_past = p[..., : kk * MOBA_BLOCK].reshape(B, H, MOBA_QBLOCK, kk, MOBA_BLOCK)
        p_own = p[..., kk * MOBA_BLOCK:]
        return (jnp.einsum('bhqnk,bhqnkd->bhqd', p_past, vsel)
                + jnp.einsum('bhqk,bhkd->bhqd', p_own, v_own))

    out = lax.map(block, jnp.arange(nqb))
    return jnp.moveaxis(out, 0, 2).reshape(B, H, S, D)


def retention_chunkwise(q, k, v):
    B, H, S, dk = q.shape
    dv = v.shape[-1]
    C = RET_CHUNK
    nc = S // C
    log_g = jnp.log(1.0 - 2.0 ** (-5.0 - jnp.arange(H, dtype=jnp.float32)))
    n = jnp.arange(C, dtype=jnp.float32)
    diff = n[:, None] - n[None, :]
    decay = jnp.where(diff >= 0, jnp.exp(log_g[:, None, None] * jnp.maximum(diff, 0.0)), 0.0)
    xi = jnp.exp(log_g[:, None] * (n + 1.0))
    zeta = jnp.exp(log_g[:, None] * (C - 1.0 - n))
    chunk_decay = jnp.exp(log_g * C)

    def chunks(t):
        return jnp.moveaxis(t.reshape(B, H, nc, C, t.shape[-1]), 2, 0)

    def step(state, inp):
        qi, ki, vi = inp
        inner = jnp.einsum('bhqd,bhkd->bhqk', qi, ki).astype(jnp.float32) * decay
        o = (jnp.einsum('bhqk,bhkv->bhqv', inner, vi.astype(jnp.float32))
             + jnp.einsum('bhqd,bhdv->bhqv', qi.astype(jnp.float32), state) * xi[..., None])
        state = state * chunk_decay[:, None, None] + jnp.einsum(
            'bhkd,bhkv->bhdv', ki.astype(jnp.float32) * zeta[..., None], vi.astype(jnp.float32))
        return state, o

    init = jnp.zeros((B, H, dk, dv), jnp.float32)
    _, out = lax.scan(step, init, (chunks(q), chunks(k), chunks(v)))
    return jnp.moveaxis(out, 0, 2).reshape(B, H, S, dv)


def attn_hybrid_mixer(h, w_qkv, w_o, pos):
    B, S, _ = h.shape
    qkv = (h @ w_qkv).reshape(B, S, 3, N_ATT_HEADS, HEAD_DIM).transpose(2, 0, 3, 1, 4)
    q, k, v = qkv[0], qkv[1], qkv[2]
    sb = stick_breaking_attention(q[:, :N_SB_HEADS], k[:, :N_SB_HEADS], v[:, :N_SB_HEADS])
    qm = rotary(q[:, N_SB_HEADS:], pos)
    km = rotary(k[:, N_SB_HEADS:], pos)
    mb = moba_attention(qm, km, v[:, N_SB_HEADS:])
    o = jnp.concatenate([sb, mb], axis=1).transpose(0, 2, 1, 3).reshape(B, S, ATT_W)
    return o @ w_o


def retention_mixer(h, w_in, gn_gain, w_o, pos):
    B, S, _ = h.shape
    proj = h @ w_in
    q, k, v, g = jnp.split(proj, [RET_QK_W, 2 * RET_QK_W, 2 * RET_QK_W + RET_V_W], axis=-1)

    def heads(t, d):
        return t.reshape(B, S, N_RET_HEADS, d).transpose(0, 2, 1, 3)

    q = rotary(heads(q, RET_DK), pos)
    k = rotary(heads(k, RET_DK), pos) * (RET_DK ** -0.5)
    v = heads(v, RET_DV)
    o = retention_chunkwise(q, k, v)
    mu = jnp.mean(o, axis=-1, keepdims=True)
    var = jnp.mean(jnp.square(o - mu), axis=-1, keepdims=True)
    o = (o - mu) * lax.rsqrt(var + GN_EPS)
    o = o.transpose(0, 2, 1, 3).reshape(B, S, RET_V_W) * gn_gain.astype(jnp.float32)
    return (jax.nn.silu(g) * o.astype(h.dtype)) @ w_o


def swiglu(h, w_gate_up, w_down):
    gt, up = jnp.split(h @ w_gate_up, 2, axis=-1)
    return (jax.nn.silu(gt) * up) @ w_down


def setup_inputs(seed: int = 0) -> dict:
    key = jax.random.key(seed)
    ks = jax.random.split(key, 14)
    f32 = jnp.float32

    def nrm(k, shape, fan_in):
        return jax.random.normal(k, shape, f32) * (fan_in ** -0.5)

    return {
        "x": jax.random.normal(ks[0], (BATCH, SEQ, D_MODEL), f32),
        "c": jax.random.normal(ks[1], (BATCH, D_MODEL), f32),
        "ada_w": nrm(ks[2], (DEPTH, D_MODEL, 6 * D_MODEL), D_MODEL),
        "ada_b": 0.02 * jax.random.normal(ks[3], (DEPTH, 6 * D_MODEL), f32),
        "norm_gains": 1.0 + 0.05 * jax.random.normal(ks[4], (DEPTH, 2, D_MODEL), f32),
        "att_w_qkv": nrm(ks[5], (N_EVEN, D_MODEL, 3 * ATT_W), D_MODEL),
        "att_w_o": nrm(ks[6], (N_EVEN, ATT_W, D_MODEL), ATT_W),
        "ret_w_in": nrm(ks[7], (N_ODD, D_MODEL, 2 * RET_QK_W + 2 * RET_V_W), D_MODEL),
        "ret_gn": 1.0 + 0.05 * jax.random.normal(ks[8], (N_ODD, RET_V_W), f32),
        "ret_w_o": nrm(ks[9], (N_ODD, RET_V_W, D_MODEL), RET_V_W),
        "ffn_w_gate_up": nrm(ks[10], (DEPTH, D_MODEL, 2 * D_FF), D_MODEL),
        "ffn_w_down": nrm(ks[11], (DEPTH, D_FF, D_MODEL), D_FF),
        "final_norm": 1.0 + 0.05 * jax.random.normal(ks[12], (D_MODEL,), f32),
    }


def reference(x, c, ada_w, ada_b, norm_gains, att_w_qkv, att_w_o, ret_w_in, ret_gn,
              ret_w_o, ffn_w_gate_up, ffn_w_down, final_norm):
    S = x.shape[1]
    pos = jnp.arange(S, dtype=jnp.int32)
    for layer in range(DEPTH):
        shift_m, scale_m, gate_m, shift_f, scale_f, gate_f = ada_mod(c, ada_w[layer], ada_b[layer])
        h = modulate(rms_norm(x, norm_gains[layer, 0]), shift_m, scale_m)
        j = layer // 2
        if layer % 2 == 0:
            y = attn_hybrid_mixer(h, att_w_qkv[j], att_w_o[j], pos)
        else:
            y = retention_mixer(h, ret_w_in[j], ret_gn[j], ret_w_o[j], pos)
        x = x + gate_m * y
        h = modulate(rms_norm(x, norm_gains[layer, 1]), shift_f, scale_f)
        x = x + gate_f * swiglu(h, ffn_w_gate_up[layer], ffn_w_down[layer])
    return rms_norm(x, final_norm)
```

```python
import functools
import math

import jax
import jax.numpy as jnp
from jax import lax
from jax.experimental import pallas as pl
from jax.experimental.pallas import tpu as pltpu

F32 = jnp.float32
BF16 = jnp.bfloat16

D_MODEL = 1024
HEAD_DIM = 64
N_SB_HEADS = 8
N_MOBA_HEADS = 8
ATT_W = (N_SB_HEADS + N_MOBA_HEADS) * HEAD_DIM
MOBA_BLOCK = 256
MOBA_TOPK = 3
N_RET_HEADS = 4
RET_DK = D_MODEL // N_RET_HEADS
RET_DV = 2 * RET_DK
RET_QK_W = N_RET_HEADS * RET_DK
RET_V_W = N_RET_HEADS * RET_DV
D_FF = -(-8 * D_MODEL // 768) * 256
ROPE_THETA = 10000.0
NORM_EPS = 1e-6
GN_EPS = 1e-5
NEG = -1e30

LANES = 128
ROW_TILE = 512
COL_CHUNK = 512
ATT_TILE = 256
RET_CHUNK = 256
VMEM_LIMIT = 56 * 1024 * 1024


def _params(n_axes):
    return pltpu.CompilerParams(
        dimension_semantics=("arbitrary",) * n_axes, vmem_limit_bytes=VMEM_LIMIT)


def _resident(shape, index_map):
    return pl.BlockSpec(shape, index_map, pipeline_mode=pl.Buffered(1))


def _silu(v):
    return v * (1.0 / (1.0 + jnp.exp(-v)))


def _ada_kernel(c_ref, w_ref, b_ref, o_ref):
    o_ref[0] = jnp.dot(_silu(c_ref[...]), w_ref[0], preferred_element_type=F32) + b_ref[0]


def _ada_mod(c, ada_w, ada_b):
    depth, d, n = ada_w.shape
    cp = jnp.pad(c, ((0, 8 - c.shape[0]), (0, 0)))
    tn = 1536
    return pl.pallas_call(
        _ada_kernel,
        grid=(depth, n // tn),
        in_specs=[
            pl.BlockSpec((8, d), lambda l, j: (0, 0)),
            pl.BlockSpec((1, d, tn), lambda l, j: (l, 0, j)),
            pl.BlockSpec((1, 1, tn), lambda l, j: (l, 0, j)),
        ],
        out_specs=pl.BlockSpec((1, 8, tn), lambda l, j: (l, 0, j)),
        out_shape=jax.ShapeDtypeStruct((depth, 8, n), F32),
        compiler_params=_params(2),
        name="ada_mod",
    )(cp, ada_w, ada_b.reshape(depth, 1, n))


def _norm_mod(x, g, shift, scale):
    y = x * lax.rsqrt(jnp.mean(x * x, axis=-1, keepdims=True) + NORM_EPS)
    return (y * g) * (1.0 + scale) + shift


def _att_proj_kernel(x_ref, g_ref, sh_ref, sc_ref, cos_ref, sina_ref, sinb_ref, w_ref, o_ref):
    h = _norm_mod(x_ref[...], g_ref[...], sh_ref[0], sc_ref[0]).astype(BF16)
    cos, sina, sinb = cos_ref[...], sina_ref[...], sinb_ref[...]
    sb_w = N_SB_HEADS * HEAD_DIM
    for c in range(3 * ATT_W // COL_CHUNK):
        lo = c * COL_CHUNK
        y = jnp.dot(h, w_ref[:, lo:lo + COL_CHUNK], preferred_element_type=F32)
        is_q = lo < ATT_W
        rotated = (lo % ATT_W) >= sb_w and lo < 2 * ATT_W
        for s in range(COL_CHUNK // LANES):
            ys = y[:, s * LANES:(s + 1) * LANES]
            if rotated:
                ys = (ys * cos + pltpu.roll(ys, LANES - HEAD_DIM // 2, 1) * sina
                      + pltpu.roll(ys, HEAD_DIM // 2, 1) * sinb)
            if is_q:
                ys = ys * (HEAD_DIM ** -0.5)
            o_ref[:, lo + s * LANES:lo + (s + 1) * LANES] = ys.astype(BF16)


def _ret_proj_kernel(x_ref, g_ref, sh_ref, sc_ref, cos_ref, sin_ref, w_ref, o_ref):
    h = _norm_mod(x_ref[...], g_ref[...], sh_ref[0], sc_ref[0]).astype(BF16)
    cos, sin = cos_ref[...], sin_ref[...]
    n = 2 * RET_QK_W + 2 * RET_V_W
    for c in range(n // COL_CHUNK):
        lo = c * COL_CHUNK
        y = jnp.dot(h, w_ref[:, lo:lo + COL_CHUNK], preferred_element_type=F32)
        if lo < 2 * RET_QK_W:
            mul = 1.0 if lo < RET_QK_W else RET_DK ** -0.5
            half = RET_DK // 2
            for hd in range(COL_CHUNK // RET_DK):
                x1 = y[:, hd * RET_DK:hd * RET_DK + half]
                x2 = y[:, hd * RET_DK + half:(hd + 1) * RET_DK]
                o1 = (x1 * cos - x2 * sin) * mul
                o2 = (x2 * cos + x1 * sin) * mul
                o_ref[:, lo + hd * RET_DK:lo + hd * RET_DK + half] = o1.astype(BF16)
                o_ref[:, lo + hd * RET_DK + half:lo + (hd + 1) * RET_DK] = o2.astype(BF16)
        else:
            o_ref[:, lo:lo + COL_CHUNK] = y.astype(BF16)


def _proj(kernel, x, gain, shift, scale, tables, w, seq, name):
    t, d = x.shape
    n = w.shape[1]
    tm = ROW_TILE
    per_seq = seq // tm
    row = lambda i: (i, 0)
    batch = lambda i: (i // per_seq, 0, 0)
    pos = lambda i: (i % per_seq, 0)
    return pl.pallas_call(
        kernel,
        grid=(t // tm,),
        in_specs=[
            pl.BlockSpec((tm, d), row),
            pl.BlockSpec((1, d), lambda i: (0, 0)),
            pl.BlockSpec((1, 1, d), batch),
            pl.BlockSpec((1, 1, d), batch),
            *[pl.BlockSpec((tm, LANES), pos) for _ in tables],
            _resident((d, n), lambda i: (0, 0)),
        ],
        out_specs=pl.BlockSpec((tm, n), row),
        out_shape=jax.ShapeDtypeStruct((t, n), BF16),
        compiler_params=_params(1),
        name=name,
    )(x, gain, shift, scale, *tables, w)


def _pair_rows(q, head0):
    zero = jnp.zeros_like(q)
    return jnp.concatenate([jnp.where(head0, q, zero), jnp.where(head0, zero, q)], axis=0)


def _sb_kernel(q_ref, k_ref, v_ref, o_ref, acc_ref, run_ref):
    tq = tk = ATT_TILE
    seq = q_ref.shape[0]
    head0 = lax.broadcasted_iota(jnp.int32, (1, LANES), 1) < HEAD_DIM
    tri = (lax.broadcasted_iota(jnp.int32, (tk, tk), 0)
           > lax.broadcasted_iota(jnp.int32, (tk, tk), 1)).astype(BF16)
    strict = (lax.broadcasted_iota(jnp.int32, (2 * tq, tk), 1)
              < lax.broadcasted_iota(jnp.int32, (2 * tq, tk), 0) % tq)

    def q_tile(i, _):
        q2 = _pair_rows(q_ref[pl.ds(pl.multiple_of(i * tq, tq), tq), :], head0)
        acc_ref[...] = jnp.zeros_like(acc_ref)
        run_ref[...] = jnp.zeros_like(run_ref)

        def block(j, diagonal):
            rows = pl.ds(pl.multiple_of(j * tk, tk), tk)
            z = lax.dot_general(q2, k_ref[rows, :], (((1,), (1,)), ((), ())),
                                preferred_element_type=F32)
            log_b = jnp.minimum(z, 0.0) - jnp.log(1.0 + jnp.exp(-jnp.abs(z)))
            log_1mb = log_b - z
            if diagonal:
                log_1mb = jnp.where(strict, log_1mb, 0.0)
            hi = log_1mb.astype(BF16)
            lo = (log_1mb - hi.astype(F32)).astype(BF16)
            after = (jnp.dot(hi, tri, preferred_element_type=F32)
                     + jnp.dot(lo, tri, preferred_element_type=F32) + run_ref[...])
            a = jnp.exp(log_b + after)
            if diagonal:
                a = jnp.where(strict, a, 0.0)
            acc_ref[...] += jnp.dot(a.astype(BF16), v_ref[rows, :], preferred_element_type=F32)
            run_ref[...] += jnp.sum(log_1mb, axis=1, keepdims=True)

        block(i, True)

        def past(t, _):
            block(i - 1 - t, False)
            return 0

        lax.fori_loop(0, i, past, 0)
        o = jnp.where(head0, acc_ref[:tq, :], acc_ref[tq:, :])
        o_ref[pl.ds(pl.multiple_of(i * tq, tq), tq), :] = o.astype(BF16)
        return 0

    lax.fori_loop(0, seq // tq, q_tile, 0)


def _pair_specs(seq, q_col, k_col, v_col):
    return [
        pl.BlockSpec((seq, LANES), lambda b, p: (b, q_col + p)),
        pl.BlockSpec((seq, LANES), lambda b, p: (b, k_col + p)),
        pl.BlockSpec((seq, LANES), lambda b, p: (b, v_col + p)),
    ]


def _sb_attention(qkv, batch, seq):
    pairs = N_SB_HEADS // 2
    blocks = ATT_W // LANES
    return pl.pallas_call(
        _sb_kernel,
        grid=(batch, pairs),
        in_specs=_pair_specs(seq, 0, blocks, 2 * blocks),
        out_specs=pl.BlockSpec((seq, LANES), lambda b, p: (b, p)),
        out_shape=jax.ShapeDtypeStruct((batch * seq, pairs * LANES), BF16),
        scratch_shapes=[pltpu.VMEM((2 * ATT_TILE, LANES), F32), pltpu.VMEM((2 * ATT_TILE, 1), F32)],
        compiler_params=_params(2),
        name="sb_attention",
    )(qkv, qkv, qkv)


def _moba_kernel(q_ref, k_ref, v_ref, o_ref, acc_ref, m_ref, l_ref, sel_ref):
    tq = tk = ATT_TILE
    seq = q_ref.shape[0]
    nb = seq // MOBA_BLOCK
    head0 = lax.broadcasted_iota(jnp.int32, (1, LANES), 1) < HEAD_DIM
    causal = (lax.broadcasted_iota(jnp.int32, (2 * tq, tk), 1)
              <= lax.broadcasted_iota(jnp.int32, (2 * tq, tk), 0) % tq)
    blk = lax.broadcasted_iota(jnp.int32, (2 * tq, nb), 1).astype(F32)
    kmean = (jnp.sum(k_ref[...].astype(F32).reshape(nb, MOBA_BLOCK, LANES), axis=1)
             * (1.0 / MOBA_BLOCK)).astype(BF16)

    def q_tile(i, _):
        q2 = _pair_rows(q_ref[pl.ds(pl.multiple_of(i * tq, tq), tq), :], head0)
        gate = lax.dot_general(q2, kmean, (((1,), (1,)), ((), ())), preferred_element_type=F32)
        past_blk = blk < i.astype(F32)
        gate = jnp.where(past_blk, gate, NEG)
        sel = jnp.zeros((2 * tq, nb), F32)
        for _r in range(MOBA_TOPK):
            top = jnp.max(gate, axis=1, keepdims=True)
            idx = jnp.min(jnp.where(gate == top, blk, float(nb)), axis=1, keepdims=True)
            hit = blk == idx
            sel = jnp.where(hit & past_blk, 1.0, sel)
            gate = jnp.where(hit, -jnp.inf, gate)
        sel_ref[...] = sel
        m_ref[...] = jnp.full_like(m_ref, -jnp.inf)
        l_ref[...] = jnp.zeros_like(l_ref)
        acc_ref[...] = jnp.zeros_like(acc_ref)

        def block(j, keep):
            rows = pl.ds(pl.multiple_of(j * tk, tk), tk)
            s = lax.dot_general(q2, k_ref[rows, :], (((1,), (1,)), ((), ())),
                                preferred_element_type=F32)
            s = jnp.where(keep, s, NEG)
            m_old = m_ref[...]
            m_new = jnp.maximum(m_old, jnp.max(s, axis=1, keepdims=True))
            alpha = jnp.exp(m_old - m_new)
            p = jnp.exp(s - m_new)
            l_ref[...] = alpha * l_ref[...] + jnp.sum(p, axis=1, keepdims=True)
            acc_ref[...] = alpha * acc_ref[...] + jnp.dot(
                p.astype(BF16), v_ref[rows, :], preferred_element_type=F32)
            m_ref[...] = m_new

        block(i, causal)

        def past(j, _):
            picked = jnp.max(jnp.where(blk == j.astype(F32), sel_ref[...], 0.0), axis=1,
                             keepdims=True)
            block(j, picked > 0.0)
            return 0

        lax.fori_loop(0, i, past, 0)
        out = acc_ref[...] / l_ref[...]
        o = jnp.where(head0, out[:tq, :], out[tq:, :])
        o_ref[pl.ds(pl.multiple_of(i * tq, tq), tq), :] = o.astype(BF16)
        return 0

    lax.fori_loop(0, seq // tq, q_tile, 0)


def _moba_attention(qkv, batch, seq):
    pairs = N_MOBA_HEADS // 2
    blocks = ATT_W // LANES
    first = N_SB_HEADS // 2
    rows = 2 * ATT_TILE
    return pl.pallas_call(
        _moba_kernel,
        grid=(batch, pairs),
        in_specs=_pair_specs(seq, first, blocks + first, 2 * blocks + first),
        out_specs=pl.BlockSpec((seq, LANES), lambda b, p: (b, p)),
        out_shape=jax.ShapeDtypeStruct((batch * seq, pairs * LANES), BF16),
        scratch_shapes=[pltpu.VMEM((rows, LANES), F32), pltpu.VMEM((rows, 1), F32),
                        pltpu.VMEM((rows, 1), F32), pltpu.VMEM((rows, seq // MOBA_BLOCK), F32)],
        compiler_params=_params(2),
        name="moba_attention",
    )(qkv, qkv, qkv)


def _ret_kernel(q_ref, k_ref, v_ref, g_ref, gn_ref, o_ref, state_ref, decay_ref):
    c = RET_CHUNK
    seq = q_ref.shape[0]
    head = (pl.program_id(1) + 5).astype(F32)
    log_g = jnp.log(1.0 - jnp.exp2(-jnp.full((1, 1), head, F32)))
    n_row = lax.broadcasted_iota(jnp.int32, (c, 1), 0).astype(F32)
    diff = (lax.broadcasted_iota(jnp.int32, (c, c), 0)
            - lax.broadcasted_iota(jnp.int32, (c, c), 1)).astype(F32)
    decay_ref[...] = jnp.where(diff >= 0, jnp.exp(log_g * jnp.maximum(diff, 0.0)), 0.0)
    xi = jnp.exp(log_g * (n_row + 1.0))
    zeta = jnp.exp(log_g * (c - 1.0 - n_row))
    chunk_decay = jnp.exp(log_g * c)
    state_ref[...] = jnp.zeros_like(state_ref)
    gain = gn_ref[...]

    def chunk(i, _):
        rows = pl.ds(pl.multiple_of(i * c, c), c)
        q, k, v = q_ref[rows, :], k_ref[rows, :], v_ref[rows, :]
        inner = lax.dot_general(q, k, (((1,), (1,)), ((), ())),
                                preferred_element_type=F32) * decay_ref[...]
        state = state_ref[...]
        o = (jnp.dot(inner.astype(BF16), v, preferred_element_type=F32)
             + jnp.dot(q, state.astype(BF16), preferred_element_type=F32) * xi)
        kz = (k.astype(F32) * zeta).astype(BF16)
        state_ref[...] = state * chunk_decay + lax.dot_general(
            kz, v, (((0,), (0,)), ((), ())), preferred_element_type=F32)
        mu = jnp.mean(o, axis=-1, keepdims=True)
        var = jnp.mean(jnp.square(o - mu), axis=-1, keepdims=True)
        on = (o - mu) * lax.rsqrt(var + GN_EPS) * gain
        o_ref[rows, :] = (_silu(g_ref[rows, :].astype(F32)) * on).astype(BF16)
        return 0

    lax.fori_loop(0, seq // c, chunk, 0)


def _retention(proj, gn_gain, batch, seq):
    qb = RET_QK_W // RET_DK
    vb = 2 * RET_QK_W // RET_DV
    return pl.pallas_call(
        _ret_kernel,
        grid=(batch, N_RET_HEADS),
        in_specs=[
            pl.BlockSpec((seq, RET_DK), lambda b, h: (b, h)),
            pl.BlockSpec((seq, RET_DK), lambda b, h: (b, qb + h)),
            pl.BlockSpec((seq, RET_DV), lambda b, h: (b, vb + h)),
            pl.BlockSpec((seq, RET_DV), lambda b, h: (b, vb + N_RET_HEADS + h)),
            pl.BlockSpec((1, RET_DV), lambda b, h: (0, h)),
        ],
        out_specs=pl.BlockSpec((seq, RET_DV), lambda b, h: (b, h)),
        out_shape=jax.ShapeDtypeStruct((batch * seq, RET_V_W), BF16),
        scratch_shapes=[pltpu.VMEM((RET_DK, RET_DV), F32), pltpu.VMEM((RET_CHUNK, RET_CHUNK), F32)],
        compiler_params=_params(2),
        name="retention",
    )(proj, proj, proj, proj, gn_gain)


def _out_proj_kernel(*refs, n_in):
    x_ref, gate_ref = refs[0], refs[1]
    a_refs, w_refs, o_ref = refs[2:2 + n_in], refs[2 + n_in:2 + 2 * n_in], refs[2 + 2 * n_in]
    y = jnp.dot(a_refs[0][...], w_refs[0][...], preferred_element_type=F32)
    for a_ref, w_ref in zip(a_refs[1:], w_refs[1:]):
        y += jnp.dot(a_ref[...], w_ref[...], preferred_element_type=F32)
    o_ref[...] = x_ref[...] + gate_ref[0] * y


def _out_proj(x, gate, acts, weights, seq, name):
    t, d = x.shape
    tm = ROW_TILE
    per_seq = seq // tm
    row = lambda i: (i, 0)
    return pl.pallas_call(
        functools.partial(_out_proj_kernel, n_in=len(acts)),
        grid=(t // tm,),
        in_specs=[
            pl.BlockSpec((tm, d), row),
            pl.BlockSpec((1, 1, d), lambda i: (i // per_seq, 0, 0)),
            *[pl.BlockSpec((tm, a.shape[1]), row) for a in acts],
            *[_resident(w.shape, lambda i: (0, 0)) for w in weights],
        ],
        out_specs=pl.BlockSpec((tm, d), row),
        out_shape=jax.ShapeDtypeStruct((t, d), F32),
        compiler_params=_params(1),
        name=name,
    )(x, gate, *acts, *weights)


def _ffn_kernel(x_ref, g_ref, sh_ref, sc_ref, gate_ref, wgu_ref, wd_ref, fin_ref, o_ref,
                *, final_norm):
    x = x_ref[...]
    h = _norm_mod(x, g_ref[...], sh_ref[0], sc_ref[0]).astype(BF16)
    half = D_FF // 2
    y = None
    for c in range(2):
        cols = slice(c * half, (c + 1) * half)
        gt = jnp.dot(h, wgu_ref[:, cols], preferred_element_type=F32)
        up = jnp.dot(h, wgu_ref[:, D_FF + c * half:D_FF + (c + 1) * half],
                     preferred_element_type=F32)
        part = jnp.dot((_silu(gt) * up).astype(BF16), wd_ref[cols, :], preferred_element_type=F32)
        y = part if y is None else y + part
    out = x + gate_ref[0] * y
    if final_norm:
        out = out * lax.rsqrt(jnp.mean(out * out, axis=-1, keepdims=True) + NORM_EPS) * fin_ref[...]
    o_ref[...] = out


def _ffn(x, gain, shift, scale, gate, w_gate_up, w_down, fin_gain, seq, final_norm, name):
    t, d = x.shape
    tm = ROW_TILE
    per_seq = seq // tm
    row = lambda i: (i, 0)
    batch = lambda i: (i // per_seq, 0, 0)
    const = lambda i: (0, 0)
    return pl.pallas_call(
        functools.partial(_ffn_kernel, final_norm=final_norm),
        grid=(t // tm,),
        in_specs=[
            pl.BlockSpec((tm, d), row),
            pl.BlockSpec((1, d), const),
            pl.BlockSpec((1, 1, d), batch),
            pl.BlockSpec((1, 1, d), batch),
            pl.BlockSpec((1, 1, d), batch),
            _resident(w_gate_up.shape, const),
            _resident(w_down.shape, const),
            pl.BlockSpec((1, d), const),
        ],
        out_specs=pl.BlockSpec((tm, d), row),
        out_shape=jax.ShapeDtypeStruct((t, d), F32),
        compiler_params=_params(1),
        name=name,
    )(x, gain, shift, scale, gate, w_gate_up, w_down, fin_gain)


def _rope_tables(seq, dim, tile):
    inv = ROPE_THETA ** (-jnp.arange(0, dim, 2, dtype=F32) / dim)
    ang = jnp.arange(seq, dtype=jnp.int32).astype(F32)[:, None] * inv[None, :]
    return jnp.tile(jnp.cos(ang), (1, tile)), jnp.tile(jnp.sin(ang), (1, tile))


def kernel(x, c, ada_w, ada_b, norm_gains, att_w_qkv, att_w_o, ret_w_in, ret_gn, ret_w_o,
           ffn_w_gate_up, ffn_w_down, final_norm):
    batch, seq, d = x.shape
    depth = ada_w.shape[0]
    assert d == D_MODEL and batch <= 8 and seq % ROW_TILE == 0 and seq % ATT_TILE == 0

    mod = _ada_mod(c, ada_w, ada_b)[:, :batch].reshape(depth, batch, 1, 6, d)
    xt = x.reshape(batch * seq, d)

    half = HEAD_DIM // 2
    cos_a, sin_a = _rope_tables(seq, HEAD_DIM, LANES // half)
    first_half = (jnp.arange(LANES) % HEAD_DIM) < half
    att_tables = (cos_a, jnp.where(first_half, -sin_a, 0.0), jnp.where(first_half, 0.0, sin_a))
    ret_tables = _rope_tables(seq, RET_DK, 1)

    for layer in range(depth):
        shift_m, scale_m, gate_m, shift_f, scale_f, gate_f = (mod[layer, :, :, i] for i in range(6))
        gains = norm_gains[layer]
        j = layer // 2
        if layer % 2 == 0:
            qkv = _proj(_att_proj_kernel, xt, gains[0:1], shift_m, scale_m, att_tables,
                        att_w_qkv[j].astype(BF16), seq, "att_proj")
            sb = _sb_attention(qkv, batch, seq)
            mb = _moba_attention(qkv, batch, seq)
            w_o = att_w_o[j].astype(BF16)
            split = N_SB_HEADS * HEAD_DIM
            xt = _out_proj(xt, gate_m, (sb, mb), (w_o[:split], w_o[split:]), seq, "att_out")
        else:
            proj = _proj(_ret_proj_kernel, xt, gains[0:1], shift_m, scale_m, ret_tables,
                         ret_w_in[j].astype(BF16), seq, "ret_proj")
            ro = _retention(proj, ret_gn[j].reshape(1, RET_V_W), batch, seq)
            xt = _out_proj(xt, gate_m, (ro,), (ret_w_o[j].astype(BF16),), seq, "ret_out")
        xt = _ffn(xt, gains[1:2], shift_f, scale_f, gate_f, ffn_w_gate_up[layer].astype(BF16),
                  ffn_w_down[layer].astype(BF16), final_norm.reshape(1, d), seq,
                  layer == depth - 1, "ffn%d" % layer)
    return xt.reshape(batch, seq, d)
```

```python
import functools
import math

import jax
import jax.numpy as jnp
from jax import lax
from jax.experimental import pallas as pl
from jax.experimental.pallas import tpu as pltpu

F32 = jnp.float32
BF16 = jnp.bfloat16

D_MODEL = 1024
HEAD_DIM = 64
N_SB_HEADS = 8
N_MOBA_HEADS = 8
ATT_W = (N_SB_HEADS + N_MOBA_HEADS) * HEAD_DIM
MOBA_BLOCK = 256
MOBA_TOPK = 3
N_RET_HEADS = 4
RET_DK = D_MODEL // N_RET_HEADS
RET_DV = 2 * RET_DK
RET_QK_W = N_RET_HEADS * RET_DK
RET_V_W = N_RET_HEADS * RET_DV
D_FF = -(-8 * D_MODEL // 768) * 256
ROPE_THETA = 10000.0
NORM_EPS = 1e-6
GN_EPS = 1e-5
NEG = -1e30

LANES = 128
ROW_TILE = 512
COL_CHUNK = 512
ATT_TILE = 256
RET_CHUNK = 256
VMEM_LIMIT = 56 * 1024 * 1024


def _params(n_axes):
    return pltpu.CompilerParams(
        dimension_semantics=("arbitrary",) * n_axes, vmem_limit_bytes=VMEM_LIMIT)


def _resident(shape, index_map):
    return pl.BlockSpec(shape, index_map, pipeline_mode=pl.Buffered(1))


def _silu(v):
    return v * (1.0 / (1.0 + jnp.exp(-v)))


def _ada_kernel(c_ref, w_ref, b_ref, o_ref):
    o_ref[0] = jnp.dot(_silu(c_ref[...]), w_ref[0], preferred_element_type=F32) + b_ref[0]


def _ada_mod(c, ada_w, ada_b):
    depth, d, n = ada_w.shape
    cp = jnp.pad(c, ((0, 8 - c.shape[0]), (0, 0)))
    tn = 1536
    return pl.pallas_call(
        _ada_kernel,
        grid=(depth, n // tn),
        in_specs=[
            pl.BlockSpec((8, d), lambda l, j: (0, 0)),
            pl.BlockSpec((1, d, tn), lambda l, j: (l, 0, j)),
            pl.BlockSpec((1, 1, tn), lambda l, j: (l, 0, j)),
        ],
        out_specs=pl.BlockSpec((1, 8, tn), lambda l, j: (l, 0, j)),
        out_shape=jax.ShapeDtypeStruct((depth, 8, n), F32),
        compiler_params=_params(2),
        name="ada_mod",
    )(cp, ada_w, ada_b.reshape(depth, 1, n))


def _norm_mod(x, g, shift, scale):
    y = x * lax.rsqrt(jnp.mean(x * x, axis=-1, keepdims=True) + NORM_EPS)
    return (y * g) * (1.0 + scale) + shift


def _att_proj_kernel(x_ref, g_ref, sh_ref, sc_ref, cos_ref, sina_ref, sinb_ref, w_ref, o_ref):
    h = _norm_mod(x_ref[...], g_ref[...], sh_ref[0], sc_ref[0]).astype(BF16)
    cos, sina, sinb = cos_ref[...], sina_ref[...], sinb_ref[...]
    sb_w = N_SB_HEADS * HEAD_DIM
    for c in range(3 * ATT_W // COL_CHUNK):
        lo = c * COL_CHUNK
        y = jnp.dot(h, w_ref[:, lo:lo + COL_CHUNK], preferred_element_type=F32)
        is_q = lo < ATT_W
        rotated = (lo % ATT_W) >= sb_w and lo < 2 * ATT_W
        for s in range(COL_CHUNK // LANES):
            ys = y[:, s * LANES:(s + 1) * LANES]
            if rotated:
                ys = (ys * cos + pltpu.roll(ys, LANES - HEAD_DIM // 2, 1) * sina
                      + pltpu.roll(ys, HEAD_DIM // 2, 1) * sinb)
            if is_q:
                ys = ys * (HEAD_DIM ** -0.5)
            o_ref[:, lo + s * LANES:lo + (s + 1) * LANES] = ys.astype(BF16)


def _ret_proj_kernel(x_ref, g_ref, sh_ref, sc_ref, cos_ref, sin_ref, w_ref, o_ref):
    h = _norm_mod(x_ref[...], g_ref[...], sh_ref[0], sc_ref[0]).astype(BF16)
    cos, sin = cos_ref[...], sin_ref[...]
    n = 2 * RET_QK_W + 2 * RET_V_W
    for c in range(n // COL_CHUNK):
        lo = c * COL_CHUNK
        y = jnp.dot(h, w_ref[:, lo:lo + COL_CHUNK], preferred_element_type=F32)
        if lo < 2 * RET_QK_W:
            mul = 1.0 if lo < RET_QK_W else RET_DK ** -0.5
            half = RET_DK // 2
            for hd in range(COL_CHUNK // RET_DK):
                x1 = y[:, hd * RET_DK:hd * RET_DK + half]
                x2 = y[:, hd * RET_DK + half:(hd + 1) * RET_DK]
                o1 = (x1 * cos - x2 * sin) * mul
                o2 = (x2 * cos + x1 * sin) * mul
                o_ref[:, lo + hd * RET_DK:lo + hd * RET_DK + half] = o1.astype(BF16)
                o_ref[:, lo + hd * RET_DK + half:lo + (hd + 1) * RET_DK] = o2.astype(BF16)
        else:
            o_ref[:, lo:lo + COL_CHUNK] = y.astype(BF16)


def _proj(kernel, x, gain, shift, scale, tables, w, seq, name):
    t, d = x.shape
    n = w.shape[1]
    tm = ROW_TILE
    per_seq = seq // tm
    row = lambda i: (i, 0)
    batch = lambda i: (i // per_seq, 0, 0)
    pos = lambda i: (i % per_seq, 0)
    return pl.pallas_call(
        kernel,
        grid=(t // tm,),
        in_specs=[
            pl.BlockSpec((tm, d), row),
            pl.BlockSpec((1, d), lambda i: (0, 0)),
            pl.BlockSpec((1, 1, d), batch),
            pl.BlockSpec((1, 1, d), batch),
            *[pl.BlockSpec((tm, LANES), pos) for _ in tables],
            _resident((d, n), lambda i: (0, 0)),
        ],
        out_specs=pl.BlockSpec((tm, n), row),
        out_shape=jax.ShapeDtypeStruct((t, n), BF16),
        compiler_params=_params(1),
        name=name,
    )(x, gain, shift, scale, *tables, w)


def _pair_rows(q, head0):
    zero = jnp.zeros_like(q)
    return jnp.concatenate([jnp.where(head0, q, zero), jnp.where(head0, zero, q)], axis=0)


def _fill_vt(v_ref, vt_ref):
    t = ATT_TILE
    for n in range(v_ref.shape[0] // t):
        vt_ref[n] = v_ref[n * t:(n + 1) * t, :].T


def _tile_masks(strict):
    t = ATT_TILE
    key = lax.broadcasted_iota(jnp.int32, (t, 2 * t), 0)
    qry = lax.broadcasted_iota(jnp.int32, (t, 2 * t), 1) % t
    chan0 = lax.broadcasted_iota(jnp.int32, (LANES, 1), 0) < HEAD_DIM
    return (key < qry) if strict else (key <= qry), chan0


def _store_pair_out(o_ref, i, acc_t, chan0):
    t = ATT_TILE
    o_t = jnp.where(chan0, acc_t[:, :t], acc_t[:, t:])
    o_ref[pl.ds(pl.multiple_of(i * t, t), t), :] = o_t.T.astype(BF16)


def _sb_kernel(q_ref, k_ref, v_ref, o_ref, vt_ref, acc_ref):
    t = ATT_TILE
    seq = q_ref.shape[0]
    head0 = lax.broadcasted_iota(jnp.int32, (1, LANES), 1) < HEAD_DIM
    strict, chan0 = _tile_masks(True)
    later = (lax.broadcasted_iota(jnp.int32, (t, t), 1)
             > lax.broadcasted_iota(jnp.int32, (t, t), 0)).astype(BF16)
    tri2 = jnp.concatenate([later, later], axis=1)
    _fill_vt(v_ref, vt_ref)

    def q_tile(i, _):
        q2 = _pair_rows(q_ref[pl.ds(pl.multiple_of(i * t, t), t), :], head0)

        def block(j, run, diagonal):
            rows = pl.ds(pl.multiple_of(j * t, t), t)
            z = lax.dot_general(k_ref[rows, :], q2, (((1,), (1,)), ((), ())),
                                preferred_element_type=F32)
            log_b = jnp.minimum(z, 0.0) - jnp.log(1.0 + jnp.exp(-jnp.abs(z)))
            log_1mb = log_b - z
            if diagonal:
                log_1mb = jnp.where(strict, log_1mb, 0.0)
            hi = log_1mb.astype(BF16)
            lo = (log_1mb - hi.astype(F32)).astype(BF16)
            after = jnp.dot(tri2, jnp.concatenate([hi, lo], axis=0),
                            preferred_element_type=F32) + run
            a = jnp.exp(log_b + after)
            if diagonal:
                a = jnp.where(strict, a, 0.0)
            part = jnp.dot(vt_ref[j], a.astype(BF16), preferred_element_type=F32)
            return run + jnp.sum(log_1mb, axis=0, keepdims=True), part

        run, part = block(i, jnp.zeros((1, 2 * t), F32), True)
        acc_ref[...] = part

        def past(s, run):
            run, part = block(i - 1 - s, run, False)
            acc_ref[...] += part
            return run

        lax.fori_loop(0, i, past, run)
        _store_pair_out(o_ref, i, acc_ref[...], chan0)
        return 0

    lax.fori_loop(0, seq // t, q_tile, 0)


def _pair_specs(seq, q_col, k_col, v_col):
    return [
        pl.BlockSpec((seq, LANES), lambda b, p: (b, q_col + p)),
        pl.BlockSpec((seq, LANES), lambda b, p: (b, k_col + p)),
        pl.BlockSpec((seq, LANES), lambda b, p: (b, v_col + p)),
    ]


def _sb_attention(qkv, batch, seq):
    pairs = N_SB_HEADS // 2
    blocks = ATT_W // LANES
    return pl.pallas_call(
        _sb_kernel,
        grid=(batch, pairs),
        in_specs=_pair_specs(seq, 0, blocks, 2 * blocks),
        out_specs=pl.BlockSpec((seq, LANES), lambda b, p: (b, p)),
        out_shape=jax.ShapeDtypeStruct((batch * seq, pairs * LANES), BF16),
        scratch_shapes=[pltpu.VMEM((seq // ATT_TILE, LANES, ATT_TILE), BF16),
                        pltpu.VMEM((LANES, 2 * ATT_TILE), F32)],
        compiler_params=_params(2),
        name="sb_attention",
    )(qkv, qkv, qkv)


def _moba_kernel(q_ref, k_ref, v_ref, o_ref, vt_ref, acc_ref, sel_ref):
    t = ATT_TILE
    seq = q_ref.shape[0]
    nb = seq // MOBA_BLOCK
    head0 = lax.broadcasted_iota(jnp.int32, (1, LANES), 1) < HEAD_DIM
    causal, chan0 = _tile_masks(False)
    blk = lax.broadcasted_iota(jnp.int32, (nb, 2 * t), 0).astype(F32)
    kmean = (jnp.sum(k_ref[...].astype(F32).reshape(nb, MOBA_BLOCK, LANES), axis=1)
             * (1.0 / MOBA_BLOCK)).astype(BF16)
    _fill_vt(v_ref, vt_ref)

    def q_tile(i, _):
        q2 = _pair_rows(q_ref[pl.ds(pl.multiple_of(i * t, t), t), :], head0)
        gate = lax.dot_general(kmean, q2, (((1,), (1,)), ((), ())),
                               preferred_element_type=F32)
        past_blk = blk < i.astype(F32)
        gate = jnp.where(past_blk, gate, NEG)
        sel = jnp.zeros((nb, 2 * t), F32)
        for _r in range(MOBA_TOPK):
            top = jnp.max(gate, axis=0, keepdims=True)
            idx = jnp.min(jnp.where(gate == top, blk, float(nb)), axis=0, keepdims=True)
            hit = blk == idx
            sel = jnp.where(hit & past_blk, 1.0, sel)
            gate = jnp.where(hit, -jnp.inf, gate)
        sel_ref[...] = sel

        def block(j, keep, m_old):
            rows = pl.ds(pl.multiple_of(j * t, t), t)
            s = lax.dot_general(k_ref[rows, :], q2, (((1,), (1,)), ((), ())),
                                preferred_element_type=F32)
            s = jnp.where(keep, s, NEG)
            m_new = jnp.max(s, axis=0, keepdims=True)
            if m_old is not None:
                m_new = jnp.maximum(m_old, m_new)
            p = jnp.exp(s - m_new)
            part = jnp.dot(vt_ref[j], p.astype(BF16), preferred_element_type=F32)
            return m_new, jnp.sum(p, axis=0, keepdims=True), part

        m, l, part = block(i, causal, None)
        acc_ref[...] = part

        def past(j, carry):
            m_old, l_old = carry
            m_new, l_blk, part = block(j, sel_ref[pl.ds(j, 1), :] > 0.0, m_old)
            alpha = jnp.exp(m_old - m_new)
            acc_ref[...] = alpha * acc_ref[...] + part
            return m_new, alpha * l_old + l_blk

        m, l = lax.fori_loop(0, i, past, (m, l))
        _store_pair_out(o_ref, i, acc_ref[...] / l, chan0)
        return 0

    lax.fori_loop(0, seq // t, q_tile, 0)


def _moba_attention(qkv, batch, seq):
    pairs = N_MOBA_HEADS // 2
    blocks = ATT_W // LANES
    first = N_SB_HEADS // 2
    rows = 2 * ATT_TILE
    return pl.pallas_call(
        _moba_kernel,
        grid=(batch, pairs),
        in_specs=_pair_specs(seq, first, blocks + first, 2 * blocks + first),
        out_specs=pl.BlockSpec((seq, LANES), lambda b, p: (b, p)),
        out_shape=jax.ShapeDtypeStruct((batch * seq, pairs * LANES), BF16),
        scratch_shapes=[pltpu.VMEM((seq // ATT_TILE, LANES, ATT_TILE), BF16),
                        pltpu.VMEM((LANES, rows), F32),
                        pltpu.VMEM((seq // MOBA_BLOCK, rows), F32)],
        compiler_params=_params(2),
        name="moba_attention",
    )(qkv, qkv, qkv)


def _ret_kernel(q_ref, k_ref, v_ref, g_ref, gn_ref, o_ref, state_ref, decay_ref):
    c = RET_CHUNK
    seq = q_ref.shape[0]
    head = (pl.program_id(1) + 5).astype(F32)
    log_g = jnp.log(1.0 - jnp.exp2(-jnp.full((1, 1), head, F32)))
    n_row = lax.broadcasted_iota(jnp.int32, (c, 1), 0).astype(F32)
    diff = (lax.broadcasted_iota(jnp.int32, (c, c), 0)
            - lax.broadcasted_iota(jnp.int32, (c, c), 1)).astype(F32)
    decay_ref[...] = jnp.where(diff >= 0, jnp.exp(log_g * jnp.maximum(diff, 0.0)), 0.0)
    xi = jnp.exp(log_g * (n_row + 1.0))
    zeta = jnp.exp(log_g * (c - 1.0 - n_row))
    chunk_decay = jnp.exp(log_g * c)
    state_ref[...] = jnp.zeros_like(state_ref)
    gain = gn_ref[...]

    def chunk(i, _):
        rows = pl.ds(pl.multiple_of(i * c, c), c)
        q, k, v = q_ref[rows, :], k_ref[rows, :], v_ref[rows, :]
        inner = lax.dot_general(q, k, (((1,), (1,)), ((), ())),
                                preferred_element_type=F32) * decay_ref[...]
        state = state_ref[...]
        o = (jnp.dot(inner.astype(BF16), v, preferred_element_type=F32)
             + jnp.dot(q, state.astype(BF16), preferred_element_type=F32) * xi)
        kz = (k.astype(F32) * zeta).astype(BF16)
        state_ref[...] = state * chunk_decay + lax.dot_general(
            kz, v, (((0,), (0,)), ((), ())), preferred_element_type=F32)
        mu = jnp.mean(o, axis=-1, keepdims=True)
        var = jnp.mean(jnp.square(o - mu), axis=-1, keepdims=True)
        on = (o - mu) * lax.rsqrt(var + GN_EPS) * gain
        o_ref[rows, :] = (_silu(g_ref[rows, :].astype(F32)) * on).astype(BF16)
        return 0

    lax.fori_loop(0, seq // c, chunk, 0)


def _retention(proj, gn_gain, batch, seq):
    qb = RET_QK_W // RET_DK
    vb = 2 * RET_QK_W // RET_DV
    return pl.pallas_call(
        _ret_kernel,
        grid=(batch, N_RET_HEADS),
        in_specs=[
            pl.BlockSpec((seq, RET_DK), lambda b, h: (b, h)),
            pl.BlockSpec((seq, RET_DK), lambda b, h: (b, qb + h)),
            pl.BlockSpec((seq, RET_DV), lambda b, h: (b, vb + h)),
            pl.BlockSpec((seq, RET_DV), lambda b, h: (b, vb + N_RET_HEADS + h)),
            pl.BlockSpec((1, RET_DV), lambda b, h: (0, h)),
        ],
        out_specs=pl.BlockSpec((seq, RET_DV), lambda b, h: (b, h)),
        out_shape=jax.ShapeDtypeStruct((batch * seq, RET_V_W), BF16),
        scratch_shapes=[pltpu.VMEM((RET_DK, RET_DV), F32), pltpu.VMEM((RET_CHUNK, RET_CHUNK), F32)],
        compiler_params=_params(2),
        name="retention",
    )(proj, proj, proj, proj, gn_gain)


def _out_proj_kernel(*refs, n_in):
    x_ref, gate_ref = refs[0], refs[1]
    a_refs, w_refs, o_ref = refs[2:2 + n_in], refs[2 + n_in:2 + 2 * n_in], refs[2 + 2 * n_in]
    y = jnp.dot(a_refs[0][...], w_refs[0][...], preferred_element_type=F32)
    for a_ref, w_ref in zip(a_refs[1:], w_refs[1:]):
        y += jnp.dot(a_ref[...], w_ref[...], preferred_element_type=F32)
    o_ref[...] = x_ref[...] + gate_ref[0] * y


def _out_proj(x, gate, acts, weights, seq, name):
    t, d = x.shape
    tm = ROW_TILE
    per_seq = seq // tm
    row = lambda i: (i, 0)
    return pl.pallas_call(
        functools.partial(_out_proj_kernel, n_in=len(acts)),
        grid=(t // tm,),
        in_specs=[
            pl.BlockSpec((tm, d), row),
            pl.BlockSpec((1, 1, d), lambda i: (i // per_seq, 0, 0)),
            *[pl.BlockSpec((tm, a.shape[1]), row) for a in acts],
            *[_resident(w.shape, lambda i: (0, 0)) for w in weights],
        ],
        out_specs=pl.BlockSpec((tm, d), row),
        out_shape=jax.ShapeDtypeStruct((t, d), F32),
        compiler_params=_params(1),
        name=name,
    )(x, gate, *acts, *weights)


def _ffn_kernel(x_ref, g_ref, sh_ref, sc_ref, gate_ref, wgu_ref, wd_ref, fin_ref, o_ref,
                *, final_norm):
    x = x_ref[...]
    h = _norm_mod(x, g_ref[...], sh_ref[0], sc_ref[0]).astype(BF16)
    half = D_FF // 2
    y = None
    for c in range(2):
        cols = slice(c * half, (c + 1) * half)
        gt = jnp.dot(h, wgu_ref[:, cols], preferred_element_type=F32)
        up = jnp.dot(h, wgu_ref[:, D_FF + c * half:D_FF + (c + 1) * half],
                     preferred_element_type=F32)
        part = jnp.dot((_silu(gt) * up).astype(BF16), wd_ref[cols, :], preferred_element_type=F32)
        y = part if y is None else y + part
    out = x + gate_ref[0] * y
    if final_norm:
        out = out * lax.rsqrt(jnp.mean(out * out, axis=-1, keepdims=True) + NORM_EPS) * fin_ref[...]
    o_ref[...] = out


def _ffn(x, gain, shift, scale, gate, w_gate_up, w_down, fin_gain, seq, final_norm, name):
    t, d = x.shape
    tm = ROW_TILE
    per_seq = seq // tm
    row = lambda i: (i, 0)
    batch = lambda i: (i // per_seq, 0, 0)
    const = lambda i: (0, 0)
    return pl.pallas_call(
        functools.partial(_ffn_kernel, final_norm=final_norm),
        grid=(t // tm,),
        in_specs=[
            pl.BlockSpec((tm, d), row),
            pl.BlockSpec((1, d), const),
            pl.BlockSpec((1, 1, d), batch),
            pl.BlockSpec((1, 1, d), batch),
            pl.BlockSpec((1, 1, d), batch),
            _resident(w_gate_up.shape, const),
            _resident(w_down.shape, const),
            pl.BlockSpec((1, d), const),
        ],
        out_specs=pl.BlockSpec((tm, d), row),
        out_shape=jax.ShapeDtypeStruct((t, d), F32),
        compiler_params=_params(1),
        name=name,
    )(x, gain, shift, scale, gate, w_gate_up, w_down, fin_gain)


def _rope_tables(seq, dim, tile):
    inv = ROPE_THETA ** (-jnp.arange(0, dim, 2, dtype=F32) / dim)
    ang = jnp.arange(seq, dtype=jnp.int32).astype(F32)[:, None] * inv[None, :]
    return jnp.tile(jnp.cos(ang), (1, tile)), jnp.tile(jnp.sin(ang), (1, tile))


def kernel(x, c, ada_w, ada_b, norm_gains, att_w_qkv, att_w_o, ret_w_in, ret_gn, ret_w_o,
           ffn_w_gate_up, ffn_w_down, final_norm):
    batch, seq, d = x.shape
    depth = ada_w.shape[0]
    assert d == D_MODEL and batch <= 8 and seq % ROW_TILE == 0 and seq % ATT_TILE == 0

    mod = _ada_mod(c, ada_w, ada_b)[:, :batch].reshape(depth, batch, 1, 6, d)
    xt = x.reshape(batch * seq, d)

    half = HEAD_DIM // 2
    cos_a, sin_a = _rope_tables(seq, HEAD_DIM, LANES // half)
    first_half = (jnp.arange(LANES) % HEAD_DIM) < half
    att_tables = (cos_a, jnp.where(first_half, -sin_a, 0.0), jnp.where(first_half, 0.0, sin_a))
    ret_tables = _rope_tables(seq, RET_DK, 1)

    for layer in range(depth):
        shift_m, scale_m, gate_m, shift_f, scale_f, gate_f = (mod[layer, :, :, i] for i in range(6))
        gains = norm_gains[layer]
        j = layer // 2
        if layer % 2 == 0:
            qkv = _proj(_att_proj_kernel, xt, gains[0:1], shift_m, scale_m, att_tables,
                        att_w_qkv[j].astype(BF16), seq, "att_proj")
            sb = _sb_attention(qkv, batch, seq)
            mb = _moba_attention(qkv, batch, seq)
            w_o = att_w_o[j].astype(BF16)
            split = N_SB_HEADS * HEAD_DIM
            xt = _out_proj(xt, gate_m, (sb, mb), (w_o[:split], w_o[split:]), seq, "att_out")
        else:
            proj = _proj(_ret_proj_kernel, xt, gains[0:1], shift_m, scale_m, ret_tables,
                         ret_w_in[j].astype(BF16), seq, "ret_proj")
            ro = _retention(proj, ret_gn[j].reshape(1, RET_V_W), batch, seq)
            xt = _out_proj(xt, gate_m, (ro,), (ret_w_o[j].astype(BF16),), seq, "ret_out")
        xt = _ffn(xt, gains[1:2], shift_f, scale_f, gate_f, ffn_w_gate_up[layer].astype(BF16),
                  ffn_w_down[layer].astype(BF16), final_norm.reshape(1, d), seq,
                  layer == depth - 1, "ffn%d" % layer)
    return xt.reshape(batch, seq, d)
```

```python
import functools
import math

import jax
import jax.numpy as jnp
from jax import lax
from jax.experimental import pallas as pl
from jax.experimental.pallas import tpu as pltpu

F32 = jnp.float32
BF16 = jnp.bfloat16

D_MODEL = 1024
HEAD_DIM = 64
N_SB_HEADS = 8
N_MOBA_HEADS = 8
ATT_W = (N_SB_HEADS + N_MOBA_HEADS) * HEAD_DIM
MOBA_BLOCK = 256
MOBA_TOPK = 3
N_RET_HEADS = 4
RET_DK = D_MODEL // N_RET_HEADS
RET_DV = 2 * RET_DK
RET_QK_W = N_RET_HEADS * RET_DK
RET_V_W = N_RET_HEADS * RET_DV
D_FF = -(-8 * D_MODEL // 768) * 256
ROPE_THETA = 10000.0
NORM_EPS = 1e-6
GN_EPS = 1e-5
NEG = -1e30
Q_SCALE = HEAD_DIM ** -0.5 * math.log2(math.e)

LANES = 128
ROW_TILE = 512
COL_CHUNK = 512
ATT_TILE = 256
RET_CHUNK = 256
VMEM_LIMIT = 56 * 1024 * 1024


def _params(n_axes):
    return pltpu.CompilerParams(
        dimension_semantics=("arbitrary",) * n_axes, vmem_limit_bytes=VMEM_LIMIT)


def _resident(shape, index_map):
    return pl.BlockSpec(shape, index_map, pipeline_mode=pl.Buffered(1))


def _silu(v):
    return v * (1.0 / (1.0 + jnp.exp(-v)))


def _ada_kernel(c_ref, w_ref, b_ref, o_ref):
    o_ref[0] = jnp.dot(_silu(c_ref[...]), w_ref[0], preferred_element_type=F32) + b_ref[0]


def _ada_mod(c, ada_w, ada_b):
    depth, d, n = ada_w.shape
    cp = jnp.pad(c, ((0, 8 - c.shape[0]), (0, 0)))
    tn = 1536
    return pl.pallas_call(
        _ada_kernel,
        grid=(depth, n // tn),
        in_specs=[
            pl.BlockSpec((8, d), lambda l, j: (0, 0)),
            pl.BlockSpec((1, d, tn), lambda l, j: (l, 0, j)),
            pl.BlockSpec((1, 1, tn), lambda l, j: (l, 0, j)),
        ],
        out_specs=pl.BlockSpec((1, 8, tn), lambda l, j: (l, 0, j)),
        out_shape=jax.ShapeDtypeStruct((depth, 8, n), F32),
        compiler_params=_params(2),
        name="ada_mod",
    )(cp, ada_w, ada_b.reshape(depth, 1, n))


def _norm_mod(x, g, shift, scale):
    y = x * lax.rsqrt(jnp.mean(x * x, axis=-1, keepdims=True) + NORM_EPS)
    return (y * g) * (1.0 + scale) + shift


def _att_proj_kernel(x_ref, g_ref, sh_ref, sc_ref, cos_ref, sina_ref, sinb_ref, w_ref, o_ref):
    h = _norm_mod(x_ref[...], g_ref[...], sh_ref[0], sc_ref[0]).astype(BF16)
    cos, sina, sinb = cos_ref[...], sina_ref[...], sinb_ref[...]
    sb_w = N_SB_HEADS * HEAD_DIM
    for c in range(3 * ATT_W // COL_CHUNK):
        lo = c * COL_CHUNK
        y = jnp.dot(h, w_ref[:, lo:lo + COL_CHUNK], preferred_element_type=F32)
        is_q = lo < ATT_W
        rotated = (lo % ATT_W) >= sb_w and lo < 2 * ATT_W
        for s in range(COL_CHUNK // LANES):
            ys = y[:, s * LANES:(s + 1) * LANES]
            if rotated:
                ys = (ys * cos + pltpu.roll(ys, LANES - HEAD_DIM // 2, 1) * sina
                      + pltpu.roll(ys, HEAD_DIM // 2, 1) * sinb)
            if is_q:
                ys = ys * Q_SCALE
            o_ref[:, lo + s * LANES:lo + (s + 1) * LANES] = ys.astype(BF16)


def _ret_proj_kernel(x_ref, g_ref, sh_ref, sc_ref, cos_ref, sin_ref, w_ref, o_ref):
    h = _norm_mod(x_ref[...], g_ref[...], sh_ref[0], sc_ref[0]).astype(BF16)
    cos, sin = cos_ref[...], sin_ref[...]
    n = 2 * RET_QK_W + 2 * RET_V_W
    for c in range(n // COL_CHUNK):
        lo = c * COL_CHUNK
        y = jnp.dot(h, w_ref[:, lo:lo + COL_CHUNK], preferred_element_type=F32)
        if lo < 2 * RET_QK_W:
            mul = 1.0 if lo < RET_QK_W else RET_DK ** -0.5
            half = RET_DK // 2
            for hd in range(COL_CHUNK // RET_DK):
                x1 = y[:, hd * RET_DK:hd * RET_DK + half]
                x2 = y[:, hd * RET_DK + half:(hd + 1) * RET_DK]
                o1 = (x1 * cos - x2 * sin) * mul
                o2 = (x2 * cos + x1 * sin) * mul
                o_ref[:, lo + hd * RET_DK:lo + hd * RET_DK + half] = o1.astype(BF16)
                o_ref[:, lo + hd * RET_DK + half:lo + (hd + 1) * RET_DK] = o2.astype(BF16)
        else:
            o_ref[:, lo:lo + COL_CHUNK] = y.astype(BF16)


def _proj(kernel, x, gain, shift, scale, tables, w, seq, name):
    t, d = x.shape
    n = w.shape[1]
    tm = ROW_TILE
    per_seq = seq // tm
    row = lambda i: (i, 0)
    batch = lambda i: (i // per_seq, 0, 0)
    pos = lambda i: (i % per_seq, 0)
    return pl.pallas_call(
        kernel,
        grid=(t // tm,),
        in_specs=[
            pl.BlockSpec((tm, d), row),
            pl.BlockSpec((1, d), lambda i: (0, 0)),
            pl.BlockSpec((1, 1, d), batch),
            pl.BlockSpec((1, 1, d), batch),
            *[pl.BlockSpec((tm, LANES), pos) for _ in tables],
            _resident((d, n), lambda i: (0, 0)),
        ],
        out_specs=pl.BlockSpec((tm, n), row),
        out_shape=jax.ShapeDtypeStruct((t, n), BF16),
        compiler_params=_params(1),
        name=name,
    )(x, gain, shift, scale, *tables, w)


def _pair_rows(q, head0):
    zero = jnp.zeros_like(q)
    return jnp.concatenate([jnp.where(head0, q, zero), jnp.where(head0, zero, q)], axis=0)


def _fill_vt(v_ref, vt_ref):
    t = ATT_TILE
    for n in range(v_ref.shape[0] // t):
        vt_ref[n] = v_ref[n * t:(n + 1) * t, :].T


def _tile_masks(strict):
    t = ATT_TILE
    key = lax.broadcasted_iota(jnp.int32, (t, 2 * t), 0)
    qry = lax.broadcasted_iota(jnp.int32, (t, 2 * t), 1) % t
    chan0 = lax.broadcasted_iota(jnp.int32, (LANES, 1), 0) < HEAD_DIM
    return (key < qry) if strict else (key <= qry), chan0


def _load_super_tile(q_ref, i, head0):
    t = ATT_TILE
    base = pl.multiple_of(i * 2 * t, 2 * t)
    qa = _pair_rows(q_ref[pl.ds(base, t), :], head0)
    qb = _pair_rows(q_ref[pl.ds(base + t, t), :], head0)
    return jnp.concatenate([qa, qb], axis=0), qb


def _store_super_tile(o_ref, i, acc_t, chan0):
    t = ATT_TILE
    base = pl.multiple_of(i * 2 * t, 2 * t)
    for cell in range(2):
        lo = cell * 2 * t
        o_t = jnp.where(chan0, acc_t[:, lo:lo + t], acc_t[:, lo + t:lo + 2 * t])
        o_ref[pl.ds(base + cell * t, t), :] = o_t.T.astype(BF16)


def _scores(k_ref, j, q_rows):
    t = ATT_TILE
    return lax.dot_general(k_ref[pl.ds(pl.multiple_of(j * t, t), t), :], q_rows,
                           (((1,), (1,)), ((), ())), preferred_element_type=F32)


def _sb_kernel(q_ref, k_ref, v_ref, o_ref, vt_ref, acc_ref):
    t = ATT_TILE
    w = 2 * t
    seq = q_ref.shape[0]
    head0 = lax.broadcasted_iota(jnp.int32, (1, LANES), 1) < HEAD_DIM
    strict, chan0 = _tile_masks(True)
    later = (lax.broadcasted_iota(jnp.int32, (t, t), 1)
             > lax.broadcasted_iota(jnp.int32, (t, t), 0)).astype(BF16)
    tri2 = jnp.concatenate([later, later], axis=1)
    _fill_vt(v_ref, vt_ref)

    def block(j, q_rows, run, mask):
        z = _scores(k_ref, j, q_rows)
        log_b = jnp.minimum(z, 0.0) - jnp.log2(1.0 + jnp.exp2(-jnp.abs(z)))
        log_1mb = log_b - z
        if mask is not None:
            log_1mb = jnp.where(mask, log_1mb, 0.0)
        hi = log_1mb.astype(BF16)
        lo = (log_1mb - hi.astype(F32)).astype(BF16)
        after = jnp.dot(tri2, jnp.concatenate([hi, lo], axis=0),
                        preferred_element_type=F32) + run
        a = jnp.exp2(log_b + after)
        if mask is not None:
            a = jnp.where(mask, a, 0.0)
        part = jnp.dot(vt_ref[j], a.astype(BF16), preferred_element_type=F32)
        return run + jnp.sum(log_1mb, axis=0, keepdims=True), part

    def q_super(i, _):
        q2, qb = _load_super_tile(q_ref, i, head0)
        run_b, part = block(2 * i + 1, qb, jnp.zeros((1, w), F32), strict)
        acc_ref[:, w:] = part
        run = jnp.concatenate([jnp.zeros((1, w), F32), run_b], axis=1)
        mask = jnp.concatenate([strict, jnp.ones((t, w), jnp.bool_)], axis=1)
        run, part = block(2 * i, q2, run, mask)
        acc_ref[:, :w] = part[:, :w]
        acc_ref[:, w:] += part[:, w:]

        def past(s, run):
            j = 2 * (i - 1 - s)
            run, part_b = block(j + 1, q2, run, None)
            run, part_a = block(j, q2, run, None)
            acc_ref[...] += part_b + part_a
            return run

        lax.fori_loop(0, i, past, run)
        _store_super_tile(o_ref, i, acc_ref[...], chan0)
        return 0

    lax.fori_loop(0, seq // w, q_super, 0)


def _pair_specs(seq, q_col, k_col, v_col):
    return [
        pl.BlockSpec((seq, LANES), lambda b, p: (b, q_col + p)),
        pl.BlockSpec((seq, LANES), lambda b, p: (b, k_col + p)),
        pl.BlockSpec((seq, LANES), lambda b, p: (b, v_col + p)),
    ]


def _sb_attention(qkv, batch, seq):
    pairs = N_SB_HEADS // 2
    blocks = ATT_W // LANES
    return pl.pallas_call(
        _sb_kernel,
        grid=(batch, pairs),
        in_specs=_pair_specs(seq, 0, blocks, 2 * blocks),
        out_specs=pl.BlockSpec((seq, LANES), lambda b, p: (b, p)),
        out_shape=jax.ShapeDtypeStruct((batch * seq, pairs * LANES), BF16),
        scratch_shapes=[pltpu.VMEM((seq // ATT_TILE, LANES, ATT_TILE), BF16),
                        pltpu.VMEM((LANES, 4 * ATT_TILE), F32)],
        compiler_params=_params(2),
        name="sb_attention",
    )(qkv, qkv, qkv)


def _moba_kernel(q_ref, k_ref, v_ref, o_ref, vt_ref, acc_ref, sel_ref):
    t = ATT_TILE
    w = 2 * t
    seq = q_ref.shape[0]
    nb = seq // MOBA_BLOCK
    head0 = lax.broadcasted_iota(jnp.int32, (1, LANES), 1) < HEAD_DIM
    causal, chan0 = _tile_masks(False)
    blk = lax.broadcasted_iota(jnp.int32, (nb, 2 * w), 0)
    cell = lax.broadcasted_iota(jnp.int32, (nb, 2 * w), 1) // w
    blk_f = blk.astype(F32)
    kmean = (jnp.sum(k_ref[...].astype(F32).reshape(nb, MOBA_BLOCK, LANES), axis=1)
             * (1.0 / MOBA_BLOCK)).astype(BF16)
    _fill_vt(v_ref, vt_ref)

    def masked(j, q_rows, keep):
        return jnp.where(keep, _scores(k_ref, j, q_rows), NEG)

    def q_super(i, _):
        q2, qb = _load_super_tile(q_ref, i, head0)
        gate = lax.dot_general(kmean, q2, (((1,), (1,)), ((), ())),
                               preferred_element_type=F32)
        past_blk = blk < 2 * i + cell
        gate = jnp.where(past_blk, gate, NEG)
        sel = jnp.zeros((nb, 2 * w), F32)
        for _r in range(MOBA_TOPK):
            top = jnp.max(gate, axis=0, keepdims=True)
            idx = jnp.min(jnp.where(gate == top, blk_f, float(nb)), axis=0, keepdims=True)
            hit = blk_f == idx
            sel = jnp.where(hit & past_blk, 1.0, sel)
            gate = jnp.where(hit, -jnp.inf, gate)
        sel_ref[...] = sel

        s = masked(2 * i + 1, qb, causal)
        m_b = jnp.max(s, axis=0, keepdims=True)
        p = jnp.exp2(s - m_b)
        acc_ref[:, :w] = jnp.zeros((LANES, w), F32)
        acc_ref[:, w:] = jnp.dot(vt_ref[2 * i + 1], p.astype(BF16), preferred_element_type=F32)
        m = jnp.concatenate([jnp.full((1, w), -jnp.inf, F32), m_b], axis=1)
        l = jnp.concatenate([jnp.zeros((1, w), F32), jnp.sum(p, axis=0, keepdims=True)], axis=1)
        keep = jnp.concatenate(
            [causal, jnp.broadcast_to(sel_ref[pl.ds(2 * i, 1), w:] > 0.0, (t, w))], axis=1)
        s = masked(2 * i, q2, keep)
        m_new = jnp.maximum(m, jnp.max(s, axis=0, keepdims=True))
        alpha = jnp.exp2(m - m_new)
        p = jnp.exp2(s - m_new)
        acc_ref[...] = alpha * acc_ref[...] + jnp.dot(
            vt_ref[2 * i], p.astype(BF16), preferred_element_type=F32)
        m, l = m_new, alpha * l + jnp.sum(p, axis=0, keepdims=True)

        def past(jj, carry):
            m_old, l_old = carry
            j = 2 * jj
            s_a = masked(j, q2, sel_ref[pl.ds(j, 1), :] > 0.0)
            s_b = masked(j + 1, q2, sel_ref[pl.ds(j + 1, 1), :] > 0.0)
            m_new = jnp.maximum(m_old, jnp.maximum(jnp.max(s_a, axis=0, keepdims=True),
                                                   jnp.max(s_b, axis=0, keepdims=True)))
            alpha = jnp.exp2(m_old - m_new)
            p_a = jnp.exp2(s_a - m_new)
            p_b = jnp.exp2(s_b - m_new)
            part = (jnp.dot(vt_ref[j], p_a.astype(BF16), preferred_element_type=F32)
                    + jnp.dot(vt_ref[j + 1], p_b.astype(BF16), preferred_element_type=F32))
            acc_ref[...] = alpha * acc_ref[...] + part
            l_new = (alpha * l_old + jnp.sum(p_a, axis=0, keepdims=True)
                     + jnp.sum(p_b, axis=0, keepdims=True))
            return m_new, l_new

        m, l = lax.fori_loop(0, i, past, (m, l))
        _store_super_tile(o_ref, i, acc_ref[...] / l, chan0)
        return 0

    lax.fori_loop(0, seq // w, q_super, 0)


def _moba_attention(qkv, batch, seq):
    pairs = N_MOBA_HEADS // 2
    blocks = ATT_W // LANES
    first = N_SB_HEADS // 2
    rows = 4 * ATT_TILE
    return pl.pallas_call(
        _moba_kernel,
        grid=(batch, pairs),
        in_specs=_pair_specs(seq, first, blocks + first, 2 * blocks + first),
        out_specs=pl.BlockSpec((seq, LANES), lambda b, p: (b, p)),
        out_shape=jax.ShapeDtypeStruct((batch * seq, pairs * LANES), BF16),
        scratch_shapes=[pltpu.VMEM((seq // ATT_TILE, LANES, ATT_TILE), BF16),
                        pltpu.VMEM((LANES, rows), F32),
                        pltpu.VMEM((seq // MOBA_BLOCK, rows), F32)],
        compiler_params=_params(2),
        name="moba_attention",
    )(qkv, qkv, qkv)


def _ret_kernel(q_ref, k_ref, v_ref, g_ref, gn_ref, o_ref, state_ref, decay_ref):
    c = RET_CHUNK
    seq = q_ref.shape[0]
    head = (pl.program_id(1) + 5).astype(F32)
    log_g = jnp.log(1.0 - jnp.exp2(-jnp.full((1, 1), head, F32)))
    n_row = lax.broadcasted_iota(jnp.int32, (c, 1), 0).astype(F32)
    diff = (lax.broadcasted_iota(jnp.int32, (c, c), 0)
            - lax.broadcasted_iota(jnp.int32, (c, c), 1)).astype(F32)
    decay_ref[...] = jnp.where(diff >= 0, jnp.exp(log_g * jnp.maximum(diff, 0.0)), 0.0)
    xi = jnp.exp(log_g * (n_row + 1.0))
    zeta = jnp.exp(log_g * (c - 1.0 - n_row))
    chunk_decay = jnp.exp(log_g * c)
    state_ref[...] = jnp.zeros_like(state_ref)
    gain = gn_ref[...]

    def chunk(i, _):
        rows = pl.ds(pl.multiple_of(i * c, c), c)
        q, k, v = q_ref[rows, :], k_ref[rows, :], v_ref[rows, :]
        inner = lax.dot_general(q, k, (((1,), (1,)), ((), ())),
                                preferred_element_type=F32) * decay_ref[...]
        state = state_ref[...]
        o = (jnp.dot(inner.astype(BF16), v, preferred_element_type=F32)
             + jnp.dot(q, state.astype(BF16), preferred_element_type=F32) * xi)
        kz = (k.astype(F32) * zeta).astype(BF16)
        state_ref[...] = state * chunk_decay + lax.dot_general(
            kz, v, (((0,), (0,)), ((), ())), preferred_element_type=F32)
        mu = jnp.mean(o, axis=-1, keepdims=True)
        var = jnp.mean(jnp.square(o - mu), axis=-1, keepdims=True)
        on = (o - mu) * lax.rsqrt(var + GN_EPS) * gain
        o_ref[rows, :] = (_silu(g_ref[rows, :].astype(F32)) * on).astype(BF16)
        return 0

    lax.fori_loop(0, seq // c, chunk, 0)


def _retention(proj, gn_gain, batch, seq):
    qb = RET_QK_W // RET_DK
    vb = 2 * RET_QK_W // RET_DV
    return pl.pallas_call(
        _ret_kernel,
        grid=(batch, N_RET_HEADS),
        in_specs=[
            pl.BlockSpec((seq, RET_DK), lambda b, h: (b, h)),
            pl.BlockSpec((seq, RET_DK), lambda b, h: (b, qb + h)),
            pl.BlockSpec((seq, RET_DV), lambda b, h: (b, vb + h)),
            pl.BlockSpec((seq, RET_DV), lambda b, h: (b, vb + N_RET_HEADS + h)),
            pl.BlockSpec((1, RET_DV), lambda b, h: (0, h)),
        ],
        out_specs=pl.BlockSpec((seq, RET_DV), lambda b, h: (b, h)),
        out_shape=jax.ShapeDtypeStruct((batch * seq, RET_V_W), BF16),
        scratch_shapes=[pltpu.VMEM((RET_DK, RET_DV), F32), pltpu.VMEM((RET_CHUNK, RET_CHUNK), F32)],
        compiler_params=_params(2),
        name="retention",
    )(proj, proj, proj, proj, gn_gain)


def _out_proj_kernel(*refs, n_in):
    x_ref, gate_ref = refs[0], refs[1]
    a_refs, w_refs, o_ref = refs[2:2 + n_in], refs[2 + n_in:2 + 2 * n_in], refs[2 + 2 * n_in]
    y = jnp.dot(a_refs[0][...], w_refs[0][...], preferred_element_type=F32)
    for a_ref, w_ref in zip(a_refs[1:], w_refs[1:]):
        y += jnp.dot(a_ref[...], w_ref[...], preferred_element_type=F32)
    o_ref[...] = x_ref[...] + gate_ref[0] * y


def _out_proj(x, gate, acts, weights, seq, name):
    t, d = x.shape
    tm = ROW_TILE
    per_seq = seq // tm
    row = lambda i: (i, 0)
    return pl.pallas_call(
        functools.partial(_out_proj_kernel, n_in=len(acts)),
        grid=(t // tm,),
        in_specs=[
            pl.BlockSpec((tm, d), row),
            pl.BlockSpec((1, 1, d), lambda i: (i // per_seq, 0, 0)),
            *[pl.BlockSpec((tm, a.shape[1]), row) for a in acts],
            *[_resident(w.shape, lambda i: (0, 0)) for w in weights],
        ],
        out_specs=pl.BlockSpec((tm, d), row),
        out_shape=jax.ShapeDtypeStruct((t, d), F32),
        compiler_params=_params(1),
        name=name,
    )(x, gate, *acts, *weights)


def _ffn_kernel(x_ref, g_ref, sh_ref, sc_ref, gate_ref, wgu_ref, wd_ref, fin_ref, o_ref,
                *, final_norm):
    x = x_ref[...]
    h = _norm_mod(x, g_ref[...], sh_ref[0], sc_ref[0]).astype(BF16)
    half = D_FF // 2
    y = None
    for c in range(2):
        cols = slice(c * half, (c + 1) * half)
        gt = jnp.dot(h, wgu_ref[:, cols], preferred_element_type=F32)
        up = jnp.dot(h, wgu_ref[:, D_FF + c * half:D_FF + (c + 1) * half],
                     preferred_element_type=F32)
        part = jnp.dot((_silu(gt) * up).astype(BF16), wd_ref[cols, :], preferred_element_type=F32)
        y = part if y is None else y + part
    out = x + gate_ref[0] * y
    if final_norm:
        out = out * lax.rsqrt(jnp.mean(out * out, axis=-1, keepdims=True) + NORM_EPS) * fin_ref[...]
    o_ref[...] = out


def _ffn(x, gain, shift, scale, gate, w_gate_up, w_down, fin_gain, seq, final_norm, name):
    t, d = x.shape
    tm = ROW_TILE
    per_seq = seq // tm
    row = lambda i: (i, 0)
    batch = lambda i: (i // per_seq, 0, 0)
    const = lambda i: (0, 0)
    return pl.pallas_call(
        functools.partial(_ffn_kernel, final_norm=final_norm),
        grid=(t // tm,),
        in_specs=[
            pl.BlockSpec((tm, d), row),
            pl.BlockSpec((1, d), const),
            pl.BlockSpec((1, 1, d), batch),
            pl.BlockSpec((1, 1, d), batch),
            pl.BlockSpec((1, 1, d), batch),
            _resident(w_gate_up.shape, const),
            _resident(w_down.shape, const),
            pl.BlockSpec((1, d), const),
        ],
        out_specs=pl.BlockSpec((tm, d), row),
        out_shape=jax.ShapeDtypeStruct((t, d), F32),
        compiler_params=_params(1),
        name=name,
    )(x, gain, shift, scale, gate, w_gate_up, w_down, fin_gain)


def _rope_tables(seq, dim, tile):
    inv = ROPE_THETA ** (-jnp.arange(0, dim, 2, dtype=F32) / dim)
    ang = jnp.arange(seq, dtype=jnp.int32).astype(F32)[:, None] * inv[None, :]
    return jnp.tile(jnp.cos(ang), (1, tile)), jnp.tile(jnp.sin(ang), (1, tile))


def kernel(x, c, ada_w, ada_b, norm_gains, att_w_qkv, att_w_o, ret_w_in, ret_gn, ret_w_o,
           ffn_w_gate_up, ffn_w_down, final_norm):
    batch, seq, d = x.shape
    depth = ada_w.shape[0]
    assert d == D_MODEL and batch <= 8 and seq % ROW_TILE == 0 and seq % ATT_TILE == 0

    mod = _ada_mod(c, ada_w, ada_b)[:, :batch].reshape(depth, batch, 1, 6, d)
    xt = x.reshape(batch * seq, d)

    half = HEAD_DIM // 2
    cos_a, sin_a = _rope_tables(seq, HEAD_DIM, LANES // half)
    first_half = (jnp.arange(LANES) % HEAD_DIM) < half
    att_tables = (cos_a, jnp.where(first_half, -sin_a, 0.0), jnp.where(first_half, 0.0, sin_a))
    ret_tables = _rope_tables(seq, RET_DK, 1)

    for layer in range(depth):
        shift_m, scale_m, gate_m, shift_f, scale_f, gate_f = (mod[layer, :, :, i] for i in range(6))
        gains = norm_gains[layer]
        j = layer // 2
        if layer % 2 == 0:
            qkv = _proj(_att_proj_kernel, xt, gains[0:1], shift_m, scale_m, att_tables,
                        att_w_qkv[j].astype(BF16), seq, "att_proj")
            sb = _sb_attention(qkv, batch, seq)
            mb = _moba_attention(qkv, batch, seq)
            w_o = att_w_o[j].astype(BF16)
            split = N_SB_HEADS * HEAD_DIM
            xt = _out_proj(xt, gate_m, (sb, mb), (w_o[:split], w_o[split:]), seq, "att_out")
        else:
            proj = _proj(_ret_proj_kernel, xt, gains[0:1], shift_m, scale_m, ret_tables,
                         ret_w_in[j].astype(BF16), seq, "ret_proj")
            ro = _retention(proj, ret_gn[j].reshape(1, RET_V_W), batch, seq)
            xt = _out_proj(xt, gate_m, (ro,), (ret_w_o[j].astype(BF16),), seq, "ret_out")
        xt = _ffn(xt, gains[1:2], shift_f, scale_f, gate_f, ffn_w_gate_up[layer].astype(BF16),
                  ffn_w_down[layer].astype(BF16), final_norm.reshape(1, d), seq,
                  layer == depth - 1, "ffn%d" % layer)
    return xt.reshape(batch, seq, d)
```

```python
import functools
import math

import jax
import jax.numpy as jnp
from jax import lax
from jax.experimental import pallas as pl
from jax.experimental.pallas import tpu as pltpu

F32 = jnp.float32
BF16 = jnp.bfloat16

D_MODEL = 1024
HEAD_DIM = 64
N_SB_HEADS = 8
N_MOBA_HEADS = 8
ATT_W = (N_SB_HEADS + N_MOBA_HEADS) * HEAD_DIM
MOBA_BLOCK = 256
MOBA_TOPK = 3
N_RET_HEADS = 4
RET_DK = D_MODEL // N_RET_HEADS
RET_DV = 2 * RET_DK
RET_QK_W = N_RET_HEADS * RET_DK
RET_V_W = N_RET_HEADS * RET_DV
D_FF = -(-8 * D_MODEL // 768) * 256
ROPE_THETA = 10000.0
NORM_EPS = 1e-6
GN_EPS = 1e-5
NEG = -1e30
Q_SCALE = HEAD_DIM ** -0.5 * math.log2(math.e)

LANES = 128
ROW_TILE = 512
COL_CHUNK = 512
ATT_TILE = 256
PAIRS_PER_STEP = 2
ATT_FLAGS = None
RET_CHUNK = 256
VMEM_LIMIT = 56 * 1024 * 1024


def _params(n_axes, flags=None):
    return pltpu.CompilerParams(
        dimension_semantics=("arbitrary",) * n_axes, vmem_limit_bytes=VMEM_LIMIT, flags=flags)


def _resident(shape, index_map):
    return pl.BlockSpec(shape, index_map, pipeline_mode=pl.Buffered(1))


def _silu(v):
    return v * (1.0 / (1.0 + jnp.exp(-v)))


def _ada_kernel(c_ref, w_ref, b_ref, o_ref):
    o_ref[0] = jnp.dot(_silu(c_ref[...]), w_ref[0], preferred_element_type=F32) + b_ref[0]


def _ada_mod(c, ada_w, ada_b):
    depth, d, n = ada_w.shape
    cp = jnp.pad(c, ((0, 8 - c.shape[0]), (0, 0)))
    tn = 1536
    return pl.pallas_call(
        _ada_kernel,
        grid=(depth, n // tn),
        in_specs=[
            pl.BlockSpec((8, d), lambda l, j: (0, 0)),
            pl.BlockSpec((1, d, tn), lambda l, j: (l, 0, j)),
            pl.BlockSpec((1, 1, tn), lambda l, j: (l, 0, j)),
        ],
        out_specs=pl.BlockSpec((1, 8, tn), lambda l, j: (l, 0, j)),
        out_shape=jax.ShapeDtypeStruct((depth, 8, n), F32),
        compiler_params=_params(2),
        name="ada_mod",
    )(cp, ada_w, ada_b.reshape(depth, 1, n))


def _norm_mod(x, g, shift, scale):
    y = x * lax.rsqrt(jnp.mean(x * x, axis=-1, keepdims=True) + NORM_EPS)
    return (y * g) * (1.0 + scale) + shift


def _att_proj_kernel(x_ref, g_ref, sh_ref, sc_ref, cos_ref, sina_ref, sinb_ref, w_ref, o_ref):
    h = _norm_mod(x_ref[...], g_ref[...], sh_ref[0], sc_ref[0]).astype(BF16)
    cos, sina, sinb = cos_ref[...], sina_ref[...], sinb_ref[...]
    sb_w = N_SB_HEADS * HEAD_DIM
    for c in range(3 * ATT_W // COL_CHUNK):
        lo = c * COL_CHUNK
        y = jnp.dot(h, w_ref[:, lo:lo + COL_CHUNK], preferred_element_type=F32)
        is_q = lo < ATT_W
        rotated = (lo % ATT_W) >= sb_w and lo < 2 * ATT_W
        for s in range(COL_CHUNK // LANES):
            ys = y[:, s * LANES:(s + 1) * LANES]
            if rotated:
                ys = (ys * cos + pltpu.roll(ys, LANES - HEAD_DIM // 2, 1) * sina
                      + pltpu.roll(ys, HEAD_DIM // 2, 1) * sinb)
            if is_q:
                ys = ys * Q_SCALE
            o_ref[:, lo + s * LANES:lo + (s + 1) * LANES] = ys.astype(BF16)


def _ret_proj_kernel(x_ref, g_ref, sh_ref, sc_ref, cos_ref, sin_ref, w_ref, o_ref):
    h = _norm_mod(x_ref[...], g_ref[...], sh_ref[0], sc_ref[0]).astype(BF16)
    cos, sin = cos_ref[...], sin_ref[...]
    n = 2 * RET_QK_W + 2 * RET_V_W
    for c in range(n // COL_CHUNK):
        lo = c * COL_CHUNK
        y = jnp.dot(h, w_ref[:, lo:lo + COL_CHUNK], preferred_element_type=F32)
        if lo < 2 * RET_QK_W:
            mul = 1.0 if lo < RET_QK_W else RET_DK ** -0.5
            half = RET_DK // 2
            for hd in range(COL_CHUNK // RET_DK):
                x1 = y[:, hd * RET_DK:hd * RET_DK + half]
                x2 = y[:, hd * RET_DK + half:(hd + 1) * RET_DK]
                o1 = (x1 * cos - x2 * sin) * mul
                o2 = (x2 * cos + x1 * sin) * mul
                o_ref[:, lo + hd * RET_DK:lo + hd * RET_DK + half] = o1.astype(BF16)
                o_ref[:, lo + hd * RET_DK + half:lo + (hd + 1) * RET_DK] = o2.astype(BF16)
        else:
            o_ref[:, lo:lo + COL_CHUNK] = y.astype(BF16)


def _proj(kernel, x, gain, shift, scale, tables, w, seq, name):
    t, d = x.shape
    n = w.shape[1]
    tm = ROW_TILE
    per_seq = seq // tm
    row = lambda i: (i, 0)
    batch = lambda i: (i // per_seq, 0, 0)
    pos = lambda i: (i % per_seq, 0)
    return pl.pallas_call(
        kernel,
        grid=(t // tm,),
        in_specs=[
            pl.BlockSpec((tm, d), row),
            pl.BlockSpec((1, d), lambda i: (0, 0)),
            pl.BlockSpec((1, 1, d), batch),
            pl.BlockSpec((1, 1, d), batch),
            *[pl.BlockSpec((tm, LANES), pos) for _ in tables],
            _resident((d, n), lambda i: (0, 0)),
        ],
        out_specs=pl.BlockSpec((tm, n), row),
        out_shape=jax.ShapeDtypeStruct((t, n), BF16),
        compiler_params=_params(1),
        name=name,
    )(x, gain, shift, scale, *tables, w)


def _pair_rows(q, head0):
    zero = jnp.zeros_like(q)
    return jnp.concatenate([jnp.where(head0, q, zero), jnp.where(head0, zero, q)], axis=0)


def _pair_lanes(p):
    return slice(p * LANES, (p + 1) * LANES)


def _fill_vt(v_ref, vt_ref):
    t = ATT_TILE
    for p in range(v_ref.shape[1] // LANES):
        for n in range(v_ref.shape[0] // t):
            vt_ref[p, n] = v_ref[n * t:(n + 1) * t, _pair_lanes(p)].T


def _tile_masks(strict):
    t = ATT_TILE
    key = lax.broadcasted_iota(jnp.int32, (t, 2 * t), 0)
    qry = lax.broadcasted_iota(jnp.int32, (t, 2 * t), 1) % t
    chan0 = lax.broadcasted_iota(jnp.int32, (LANES, 1), 0) < HEAD_DIM
    return (key < qry) if strict else (key <= qry), chan0


def _load_super_tile(q_ref, p, i, head0):
    t = ATT_TILE
    base = pl.multiple_of(i * 2 * t, 2 * t)
    qa = _pair_rows(q_ref[pl.ds(base, t), _pair_lanes(p)], head0)
    qb = _pair_rows(q_ref[pl.ds(base + t, t), _pair_lanes(p)], head0)
    return jnp.concatenate([qa, qb], axis=0), qb


def _store_super_tile(o_ref, p, i, acc_t, chan0):
    t = ATT_TILE
    base = pl.multiple_of(i * 2 * t, 2 * t)
    for cell in range(2):
        lo = cell * 2 * t
        o_t = jnp.where(chan0, acc_t[:, lo:lo + t], acc_t[:, lo + t:lo + 2 * t])
        o_ref[pl.ds(base + cell * t, t), _pair_lanes(p)] = o_t.T.astype(BF16)


def _scores(k_ref, p, j, q_rows):
    t = ATT_TILE
    return lax.dot_general(k_ref[pl.ds(pl.multiple_of(j * t, t), t), _pair_lanes(p)], q_rows,
                           (((1,), (1,)), ((), ())), preferred_element_type=F32)


def _sb_kernel(q_ref, k_ref, v_ref, o_ref, vt_ref, acc_ref, z_ref):
    t = ATT_TILE
    w = 2 * t
    seq = q_ref.shape[0]
    pairs = range(q_ref.shape[1] // LANES)
    head0 = lax.broadcasted_iota(jnp.int32, (1, LANES), 1) < HEAD_DIM
    strict, chan0 = _tile_masks(True)
    later = (lax.broadcasted_iota(jnp.int32, (t, t), 1)
             > lax.broadcasted_iota(jnp.int32, (t, t), 0)).astype(BF16)
    tri2 = jnp.concatenate([later, later], axis=1)
    _fill_vt(v_ref, vt_ref)

    def block(p, j, q_rows, run, mask, z=None):
        if z is None:
            z = _scores(k_ref, p, j, q_rows)
        log_b = jnp.minimum(z, 0.0) - jnp.log2(1.0 + jnp.exp2(-jnp.abs(z)))
        log_1mb = log_b - z
        if mask is not None:
            log_1mb = jnp.where(mask, log_1mb, 0.0)
        hi = log_1mb.astype(BF16)
        lo = (log_1mb - hi.astype(F32)).astype(BF16)
        after = jnp.dot(tri2, jnp.concatenate([hi, lo], axis=0), preferred_element_type=F32)
        a = jnp.exp2(log_b + after)
        if mask is not None:
            a = jnp.where(mask, a, 0.0)
        part = jnp.dot(vt_ref[p, j], a.astype(BF16), preferred_element_type=F32)
        return run + (after[0:1, :] + log_1mb[0:1, :]), part * jnp.exp2(run)

    def q_super(i, _):
        tiles = [_load_super_tile(q_ref, p, i, head0) for p in pairs]
        mask = jnp.concatenate([strict, jnp.ones((t, w), jnp.bool_)], axis=1)
        runs = []
        half = lax.rem(i, 2) * (2 * len(pairs))
        for p in pairs:
            q2, qb = tiles[p]
            z_ref[half + 2 * p + 1, :, w:] = _scores(k_ref, p, 2 * i + 1, qb)
            z_ref[half + 2 * p] = _scores(k_ref, p, 2 * i, q2)
        for p in pairs:
            run_b, part = block(p, 2 * i + 1, None, jnp.zeros((1, w), F32), strict,
                                z_ref[half + 2 * p + 1, :, w:])
            acc_ref[p, :, w:] = part
            run = jnp.concatenate([jnp.zeros((1, w), F32), run_b], axis=1)
            run, part = block(p, 2 * i, None, run, mask, z_ref[half + 2 * p])
            acc_ref[p, :, :w] = part[:, :w]
            acc_ref[p, :, w:] += part[:, w:]
            runs.append(run)

        def past(s, runs):
            j = 2 * (i - 1 - s)
            half = lax.rem(s, 2) * (2 * len(pairs))
            for p in pairs:
                for b in (1, 0):
                    z_ref[half + 2 * p + b] = _scores(k_ref, p, j + b, tiles[p][0])
            out = []
            for p in pairs:
                run, part_b = block(p, j + 1, None, runs[p], None, z_ref[half + 2 * p + 1])
                run, part_a = block(p, j, None, run, None, z_ref[half + 2 * p])
                acc_ref[p] += part_b + part_a
                out.append(run)
            return tuple(out)

        lax.fori_loop(0, i, past, tuple(runs))
        for p in pairs:
            _store_super_tile(o_ref, p, i, acc_ref[p], chan0)
        return 0

    lax.fori_loop(0, seq // w, q_super, 0)


def _pair_specs(seq, q_col, k_col, v_col):
    width = PAIRS_PER_STEP * LANES
    return [
        pl.BlockSpec((seq, width), lambda b, p: (b, q_col + p)),
        pl.BlockSpec((seq, width), lambda b, p: (b, k_col + p)),
        pl.BlockSpec((seq, width), lambda b, p: (b, v_col + p)),
    ]


def _attention_call(kernel, qkv, batch, seq, n_heads, first_head, extra_scratch, name):
    width = PAIRS_PER_STEP * LANES
    groups = n_heads * HEAD_DIM // width
    third = ATT_W // width
    first = first_head * HEAD_DIM // width
    return pl.pallas_call(
        kernel,
        grid=(batch, groups),
        in_specs=_pair_specs(seq, first, third + first, 2 * third + first),
        out_specs=pl.BlockSpec((seq, width), lambda b, p: (b, p)),
        out_shape=jax.ShapeDtypeStruct((batch * seq, groups * width), BF16),
        scratch_shapes=[pltpu.VMEM((PAIRS_PER_STEP, seq // ATT_TILE, LANES, ATT_TILE), BF16),
                        pltpu.VMEM((PAIRS_PER_STEP, LANES, 4 * ATT_TILE), F32),
                        *extra_scratch],
        compiler_params=_params(2, ATT_FLAGS),
        name=name,
    )(qkv, qkv, qkv)


def _sb_attention(qkv, batch, seq):
    staged = pltpu.VMEM((4 * PAIRS_PER_STEP, ATT_TILE, 4 * ATT_TILE), F32)
    return _attention_call(_sb_kernel, qkv, batch, seq, N_SB_HEADS, 0, [staged], "sb_attention")


def _moba_kernel(q_ref, k_ref, v_ref, o_ref, vt_ref, acc_ref, sel_ref, s_ref):
    t = ATT_TILE
    w = 2 * t
    seq = q_ref.shape[0]
    nb = seq // MOBA_BLOCK
    pairs = range(q_ref.shape[1] // LANES)
    head0 = lax.broadcasted_iota(jnp.int32, (1, LANES), 1) < HEAD_DIM
    causal, chan0 = _tile_masks(False)
    blk = lax.broadcasted_iota(jnp.int32, (nb, 2 * w), 0)
    cell = lax.broadcasted_iota(jnp.int32, (nb, 2 * w), 1) // w
    blk_f = blk.astype(F32)
    kmean = [(jnp.sum(k_ref[:, _pair_lanes(p)].astype(F32).reshape(nb, MOBA_BLOCK, LANES), axis=1)
              * (1.0 / MOBA_BLOCK)).astype(BF16) for p in pairs]
    _fill_vt(v_ref, vt_ref)


    def select(p, i, q2):
        gate = lax.dot_general(kmean[p], q2, (((1,), (1,)), ((), ())),
                               preferred_element_type=F32)
        past_blk = blk < 2 * i + cell
        gate = jnp.where(past_blk, gate, NEG)
        sel = jnp.zeros((nb, 2 * w), F32)
        for _r in range(MOBA_TOPK):
            top = jnp.max(gate, axis=0, keepdims=True)
            idx = jnp.min(jnp.where(gate == top, blk_f, float(nb)), axis=0, keepdims=True)
            hit = blk_f == idx
            sel = jnp.where(hit & past_blk, 1.0, sel)
            gate = jnp.where(hit, -jnp.inf, gate)
        sel_ref[p] = sel

    def own_blocks(p, i, raw_a, raw_b):
        s = jnp.where(causal, raw_b, NEG)
        m_b = jnp.max(s, axis=0, keepdims=True)
        pr = jnp.exp2(s - m_b)
        acc_ref[p, :, :w] = jnp.zeros((LANES, w), F32)
        acc_ref[p, :, w:] = jnp.dot(vt_ref[p, 2 * i + 1], pr.astype(BF16),
                                    preferred_element_type=F32)
        m = jnp.concatenate([jnp.full((1, w), -jnp.inf, F32), m_b], axis=1)
        l = jnp.concatenate([jnp.zeros((1, w), F32), jnp.sum(pr, axis=0, keepdims=True)], axis=1)
        keep = jnp.concatenate(
            [causal, jnp.broadcast_to(sel_ref[p, pl.ds(2 * i, 1), w:] > 0.0, (t, w))], axis=1)
        s = jnp.where(keep, raw_a, NEG)
        m_new = jnp.maximum(m, jnp.max(s, axis=0, keepdims=True))
        alpha = jnp.exp2(m - m_new)
        pr = jnp.exp2(s - m_new)
        acc_ref[p] = alpha * acc_ref[p] + jnp.dot(
            vt_ref[p, 2 * i], pr.astype(BF16), preferred_element_type=F32)
        return m_new, alpha * l + jnp.sum(pr, axis=0, keepdims=True)

    def past_pair(p, j, raw_a, raw_b, m_old, l_old):
        s_a = jnp.where(sel_ref[p, pl.ds(j, 1), :] > 0.0, raw_a, NEG)
        s_b = jnp.where(sel_ref[p, pl.ds(j + 1, 1), :] > 0.0, raw_b, NEG)
        m_new = jnp.maximum(m_old, jnp.maximum(jnp.max(s_a, axis=0, keepdims=True),
                                               jnp.max(s_b, axis=0, keepdims=True)))
        alpha = jnp.exp2(m_old - m_new)
        p_a = jnp.exp2(s_a - m_new)
        p_b = jnp.exp2(s_b - m_new)
        part = (jnp.dot(vt_ref[p, j], p_a.astype(BF16), preferred_element_type=F32)
                + jnp.dot(vt_ref[p, j + 1], p_b.astype(BF16), preferred_element_type=F32))
        acc_ref[p] = alpha * acc_ref[p] + part
        l_new = (alpha * l_old + jnp.sum(p_a, axis=0, keepdims=True)
                 + jnp.sum(p_b, axis=0, keepdims=True))
        return m_new, l_new

    def q_super(i, _):
        tiles = [_load_super_tile(q_ref, p, i, head0) for p in pairs]
        half = lax.rem(i, 2) * (2 * len(pairs))
        for p in pairs:
            q2, qb = tiles[p]
            s_ref[half + 2 * p + 1, :, w:] = _scores(k_ref, p, 2 * i + 1, qb)
            s_ref[half + 2 * p] = _scores(k_ref, p, 2 * i, q2)
            select(p, i, q2)
        stats = tuple(own_blocks(p, i, s_ref[half + 2 * p], s_ref[half + 2 * p + 1, :, w:])
                      for p in pairs)

        def past(jj, stats):
            half = lax.rem(jj, 2) * (2 * len(pairs))
            for p in pairs:
                for b in range(2):
                    s_ref[half + 2 * p + b] = _scores(k_ref, p, 2 * jj + b, tiles[p][0])
            return tuple(past_pair(p, 2 * jj, s_ref[half + 2 * p], s_ref[half + 2 * p + 1],
                                   *stats[p]) for p in pairs)

        stats = lax.fori_loop(0, i, past, stats)
        for p in pairs:
            _store_super_tile(o_ref, p, i, acc_ref[p] / stats[p][1], chan0)
        return 0

    lax.fori_loop(0, seq // w, q_super, 0)


def _moba_attention(qkv, batch, seq):
    sel = pltpu.VMEM((PAIRS_PER_STEP, seq // MOBA_BLOCK, 4 * ATT_TILE), F32)
    staged = pltpu.VMEM((4 * PAIRS_PER_STEP, ATT_TILE, 4 * ATT_TILE), F32)
    return _attention_call(_moba_kernel, qkv, batch, seq, N_MOBA_HEADS, N_SB_HEADS,
                           [sel, staged], "moba_attention")


def _ret_kernel(q_ref, k_ref, v_ref, g_ref, gn_ref, o_ref, state_ref, decay_ref):
    c = RET_CHUNK
    seq = q_ref.shape[0]
    head = (pl.program_id(1) + 5).astype(F32)
    log_g = jnp.log(1.0 - jnp.exp2(-jnp.full((1, 1), head, F32)))
    n_row = lax.broadcasted_iota(jnp.int32, (c, 1), 0).astype(F32)
    diff = (lax.broadcasted_iota(jnp.int32, (c, c), 0)
            - lax.broadcasted_iota(jnp.int32, (c, c), 1)).astype(F32)
    decay_ref[...] = jnp.where(diff >= 0, jnp.exp(log_g * jnp.maximum(diff, 0.0)), 0.0)
    xi = jnp.exp(log_g * (n_row + 1.0))
    zeta = jnp.exp(log_g * (c - 1.0 - n_row))
    chunk_decay = jnp.exp(log_g * c)
    state_ref[...] = jnp.zeros_like(state_ref)
    gain = gn_ref[...]

    def chunk(i, _):
        rows = pl.ds(pl.multiple_of(i * c, c), c)
        q, k, v = q_ref[rows, :], k_ref[rows, :], v_ref[rows, :]
        inner = lax.dot_general(q, k, (((1,), (1,)), ((), ())),
                                preferred_element_type=F32) * decay_ref[...]
        state = state_ref[...]
        o = (jnp.dot(inner.astype(BF16), v, preferred_element_type=F32)
             + jnp.dot(q, state.astype(BF16), preferred_element_type=F32) * xi)
        kz = (k.astype(F32) * zeta).astype(BF16)
        state_ref[...] = state * chunk_decay + lax.dot_general(
            kz, v, (((0,), (0,)), ((), ())), preferred_element_type=F32)
        mu = jnp.mean(o, axis=-1, keepdims=True)
        var = jnp.mean(jnp.square(o - mu), axis=-1, keepdims=True)
        on = (o - mu) * lax.rsqrt(var + GN_EPS) * gain
        o_ref[rows, :] = (_silu(g_ref[rows, :].astype(F32)) * on).astype(BF16)
        return 0

    lax.fori_loop(0, seq // c, chunk, 0)


def _retention(proj, gn_gain, batch, seq):
    qb = RET_QK_W // RET_DK
    vb = 2 * RET_QK_W // RET_DV
    return pl.pallas_call(
        _ret_kernel,
        grid=(batch, N_RET_HEADS),
        in_specs=[
            pl.BlockSpec((seq, RET_DK), lambda b, h: (b, h)),
            pl.BlockSpec((seq, RET_DK), lambda b, h: (b, qb + h)),
            pl.BlockSpec((seq, RET_DV), lambda b, h: (b, vb + h)),
            pl.BlockSpec((seq, RET_DV), lambda b, h: (b, vb + N_RET_HEADS + h)),
            pl.BlockSpec((1, RET_DV), lambda b, h: (0, h)),
        ],
        out_specs=pl.BlockSpec((seq, RET_DV), lambda b, h: (b, h)),
        out_shape=jax.ShapeDtypeStruct((batch * seq, RET_V_W), BF16),
        scratch_shapes=[pltpu.VMEM((RET_DK, RET_DV), F32), pltpu.VMEM((RET_CHUNK, RET_CHUNK), F32)],
        compiler_params=_params(2),
        name="retention",
    )(proj, proj, proj, proj, gn_gain)


def _out_proj_kernel(*refs, n_in):
    x_ref, gate_ref = refs[0], refs[1]
    a_refs, w_refs, o_ref = refs[2:2 + n_in], refs[2 + n_in:2 + 2 * n_in], refs[2 + 2 * n_in]
    y = jnp.dot(a_refs[0][...], w_refs[0][...], preferred_element_type=F32)
    for a_ref, w_ref in zip(a_refs[1:], w_refs[1:]):
        y += jnp.dot(a_ref[...], w_ref[...], preferred_element_type=F32)
    o_ref[...] = x_ref[...] + gate_ref[0] * y


def _out_proj(x, gate, acts, weights, seq, name):
    t, d = x.shape
    tm = ROW_TILE
    per_seq = seq // tm
    row = lambda i: (i, 0)
    return pl.pallas_call(
        functools.partial(_out_proj_kernel, n_in=len(acts)),
        grid=(t // tm,),
        in_specs=[
            pl.BlockSpec((tm, d), row),
            pl.BlockSpec((1, 1, d), lambda i: (i // per_seq, 0, 0)),
            *[pl.BlockSpec((tm, a.shape[1]), row) for a in acts],
            *[_resident(w.shape, lambda i: (0, 0)) for w in weights],
        ],
        out_specs=pl.BlockSpec((tm, d), row),
        out_shape=jax.ShapeDtypeStruct((t, d), F32),
        compiler_params=_params(1),
        name=name,
    )(x, gate, *acts, *weights)


def _ffn_kernel(x_ref, g_ref, sh_ref, sc_ref, gate_ref, wgu_ref, wd_ref, fin_ref, o_ref,
                *, final_norm):
    x = x_ref[...]
    h = _norm_mod(x, g_ref[...], sh_ref[0], sc_ref[0]).astype(BF16)
    half = D_FF // 2
    y = None
    for c in range(2):
        cols = slice(c * half, (c + 1) * half)
        gt = jnp.dot(h, wgu_ref[:, cols], preferred_element_type=F32)
        up = jnp.dot(h, wgu_ref[:, D_FF + c * half:D_FF + (c + 1) * half],
                     preferred_element_type=F32)
        part = jnp.dot((_silu(gt) * up).astype(BF16), wd_ref[cols, :], preferred_element_type=F32)
        y = part if y is None else y + part
    out = x + gate_ref[0] * y
    if final_norm:
        out = out * lax.rsqrt(jnp.mean(out * out, axis=-1, keepdims=True) + NORM_EPS) * fin_ref[...]
    o_ref[...] = out


def _ffn(x, gain, shift, scale, gate, w_gate_up, w_down, fin_gain, seq, final_norm, name):
    t, d = x.shape
    tm = ROW_TILE
    per_seq = seq // tm
    row = lambda i: (i, 0)
    batch = lambda i: (i // per_seq, 0, 0)
    const = lambda i: (0, 0)
    return pl.pallas_call(
        functools.partial(_ffn_kernel, final_norm=final_norm),
        grid=(t // tm,),
        in_specs=[
            pl.BlockSpec((tm, d), row),
            pl.BlockSpec((1, d), const),
            pl.BlockSpec((1, 1, d), batch),
            pl.BlockSpec((1, 1, d), batch),
            pl.BlockSpec((1, 1, d), batch),
            _resident(w_gate_up.shape, const),
            _resident(w_down.shape, const),
            pl.BlockSpec((1, d), const),
        ],
        out_specs=pl.BlockSpec((tm, d), row),
        out_shape=jax.ShapeDtypeStruct((t, d), F32),
        compiler_params=_params(1),
        name=name,
    )(x, gain, shift, scale, gate, w_gate_up, w_down, fin_gain)


def _rope_tables(seq, dim, tile):
    inv = ROPE_THETA ** (-jnp.arange(0, dim, 2, dtype=F32) / dim)
    ang = jnp.arange(seq, dtype=jnp.int32).astype(F32)[:, None] * inv[None, :]
    return jnp.tile(jnp.cos(ang), (1, tile)), jnp.tile(jnp.sin(ang), (1, tile))


def kernel(x, c, ada_w, ada_b, norm_gains, att_w_qkv, att_w_o, ret_w_in, ret_gn, ret_w_o,
           ffn_w_gate_up, ffn_w_down, final_norm):
    batch, seq, d = x.shape
    depth = ada_w.shape[0]
    assert d == D_MODEL and batch <= 8 and seq % ROW_TILE == 0 and seq % (2 * ATT_TILE) == 0

    mod = _ada_mod(c, ada_w, ada_b)[:, :batch].reshape(depth, batch, 1, 6, d)
    xt = x.reshape(batch * seq, d)

    half = HEAD_DIM // 2
    cos_a, sin_a = _rope_tables(seq, HEAD_DIM, LANES // half)
    first_half = (jnp.arange(LANES) % HEAD_DIM) < half
    att_tables = (cos_a, jnp.where(first_half, -sin_a, 0.0), jnp.where(first_half, 0.0, sin_a))
    ret_tables = _rope_tables(seq, RET_DK, 1)

    for layer in range(depth):
        shift_m, scale_m, gate_m, shift_f, scale_f, gate_f = (mod[layer, :, :, i] for i in range(6))
        gains = norm_gains[layer]
        j = layer // 2
        if layer % 2 == 0:
            qkv = _proj(_att_proj_kernel, xt, gains[0:1], shift_m, scale_m, att_tables,
                        att_w_qkv[j].astype(BF16), seq, "att_proj")
            sb = _sb_attention(qkv, batch, seq)
            mb = _moba_attention(qkv, batch, seq)
            w_o = att_w_o[j].astype(BF16)
            split = N_SB_HEADS * HEAD_DIM
            xt = _out_proj(xt, gate_m, (sb, mb), (w_o[:split], w_o[split:]), seq, "att_out")
        else:
            proj = _proj(_ret_proj_kernel, xt, gains[0:1], shift_m, scale_m, ret_tables,
                         ret_w_in[j].astype(BF16), seq, "ret_proj")
            ro = _retention(proj, ret_gn[j].reshape(1, RET_V_W), batch, seq)
            xt = _out_proj(xt, gate_m, (ro,), (ret_w_o[j].astype(BF16),), seq, "ret_out")
        xt = _ffn(xt, gains[1:2], shift_f, scale_f, gate_f, ffn_w_gate_up[layer].astype(BF16),
                  ffn_w_down[layer].astype(BF16), final_norm.reshape(1, d), seq,
                  layer == depth - 1, "ffn%d" % layer)
    return xt.reshape(batch, seq, d)
```

```python
import functools
import math

import jax
import jax.numpy as jnp
from jax import lax
from jax.experimental import pallas as pl
from jax.experimental.pallas import tpu as pltpu

F32 = jnp.float32
BF16 = jnp.bfloat16

D_MODEL = 1024
HEAD_DIM = 64
N_SB_HEADS = 8
N_MOBA_HEADS = 8
ATT_W = (N_SB_HEADS + N_MOBA_HEADS) * HEAD_DIM
MOBA_BLOCK = 256
MOBA_TOPK = 3
N_RET_HEADS = 4
RET_DK = D_MODEL // N_RET_HEADS
RET_DV = 2 * RET_DK
RET_QK_W = N_RET_HEADS * RET_DK
RET_V_W = N_RET_HEADS * RET_DV
D_FF = -(-8 * D_MODEL // 768) * 256
ROPE_THETA = 10000.0
NORM_EPS = 1e-6
GN_EPS = 1e-5
NEG = -1e30
Q_SCALE = HEAD_DIM ** -0.5 * math.log2(math.e)

LANES = 128
ROW_TILE = 512
COL_CHUNK = 512
ATT_TILE = 256
PAIRS_PER_STEP = 2
ATT_FLAGS = None
RET_CHUNK = 256
RET_SEQ_TILE = 1024
RET_HEADS_PER_STEP = 2
VMEM_LIMIT = 56 * 1024 * 1024


def _params(n_axes, flags=None):
    return pltpu.CompilerParams(
        dimension_semantics=("arbitrary",) * n_axes, vmem_limit_bytes=VMEM_LIMIT, flags=flags)


def _resident(shape, index_map):
    return pl.BlockSpec(shape, index_map, pipeline_mode=pl.Buffered(1))


def _silu(v):
    return v * (1.0 / (1.0 + jnp.exp(-v)))


def _ada_kernel(c_ref, w_ref, b_ref, o_ref):
    o_ref[0] = jnp.dot(_silu(c_ref[...]), w_ref[0], preferred_element_type=F32) + b_ref[0]


def _ada_mod(c, ada_w, ada_b):
    depth, d, n = ada_w.shape
    cp = jnp.pad(c, ((0, 8 - c.shape[0]), (0, 0)))
    tn = 1536
    return pl.pallas_call(
        _ada_kernel,
        grid=(depth, n // tn),
        in_specs=[
            pl.BlockSpec((8, d), lambda l, j: (0, 0)),
            pl.BlockSpec((1, d, tn), lambda l, j: (l, 0, j)),
            pl.BlockSpec((1, 1, tn), lambda l, j: (l, 0, j)),
        ],
        out_specs=pl.BlockSpec((1, 8, tn), lambda l, j: (l, 0, j)),
        out_shape=jax.ShapeDtypeStruct((depth, 8, n), F32),
        compiler_params=_params(2),
        name="ada_mod",
    )(cp, ada_w, ada_b.reshape(depth, 1, n))


def _norm_mod(x, g, shift, scale):
    y = x * lax.rsqrt(jnp.mean(x * x, axis=-1, keepdims=True) + NORM_EPS)
    return (y * g) * (1.0 + scale) + shift


def _att_proj_kernel(x_ref, g_ref, sh_ref, sc_ref, cos_ref, sina_ref, sinb_ref, w_ref, o_ref):
    h = _norm_mod(x_ref[...], g_ref[...], sh_ref[0], sc_ref[0]).astype(BF16)
    cos, sina, sinb = cos_ref[...], sina_ref[...], sinb_ref[...]
    sb_w = N_SB_HEADS * HEAD_DIM
    for c in range(3 * ATT_W // COL_CHUNK):
        lo = c * COL_CHUNK
        y = jnp.dot(h, w_ref[:, lo:lo + COL_CHUNK], preferred_element_type=F32)
        is_q = lo < ATT_W
        rotated = (lo % ATT_W) >= sb_w and lo < 2 * ATT_W
        for s in range(COL_CHUNK // LANES):
            ys = y[:, s * LANES:(s + 1) * LANES]
            if rotated:
                ys = (ys * cos + pltpu.roll(ys, LANES - HEAD_DIM // 2, 1) * sina
                      + pltpu.roll(ys, HEAD_DIM // 2, 1) * sinb)
            if is_q:
                ys = ys * Q_SCALE
            o_ref[:, lo + s * LANES:lo + (s + 1) * LANES] = ys.astype(BF16)


def _ret_proj_kernel(x_ref, g_ref, sh_ref, sc_ref, cos_ref, sin_ref, w_ref, o_ref):
    h = _norm_mod(x_ref[...], g_ref[...], sh_ref[0], sc_ref[0]).astype(BF16)
    cos, sin = cos_ref[...], sin_ref[...]
    n = 2 * RET_QK_W + 2 * RET_V_W
    for c in range(n // COL_CHUNK):
        lo = c * COL_CHUNK
        y = jnp.dot(h, w_ref[:, lo:lo + COL_CHUNK], preferred_element_type=F32)
        if lo < 2 * RET_QK_W:
            mul = 1.0 if lo < RET_QK_W else RET_DK ** -0.5
            half = RET_DK // 2
            for hd in range(COL_CHUNK // RET_DK):
                x1 = y[:, hd * RET_DK:hd * RET_DK + half]
                x2 = y[:, hd * RET_DK + half:(hd + 1) * RET_DK]
                o1 = (x1 * cos - x2 * sin) * mul
                o2 = (x2 * cos + x1 * sin) * mul
                o_ref[:, lo + hd * RET_DK:lo + hd * RET_DK + half] = o1.astype(BF16)
                o_ref[:, lo + hd * RET_DK + half:lo + (hd + 1) * RET_DK] = o2.astype(BF16)
        else:
            o_ref[:, lo:lo + COL_CHUNK] = y.astype(BF16)


def _proj(kernel, x, gain, shift, scale, tables, w, seq, name):
    t, d = x.shape
    n = w.shape[1]
    tm = ROW_TILE
    per_seq = seq // tm
    row = lambda i: (i, 0)
    batch = lambda i: (i // per_seq, 0, 0)
    pos = lambda i: (i % per_seq, 0)
    return pl.pallas_call(
        kernel,
        grid=(t // tm,),
        in_specs=[
            pl.BlockSpec((tm, d), row),
            pl.BlockSpec((1, d), lambda i: (0, 0)),
            pl.BlockSpec((1, 1, d), batch),
            pl.BlockSpec((1, 1, d), batch),
            *[pl.BlockSpec((tm, LANES), pos) for _ in tables],
            _resident((d, n), lambda i: (0, 0)),
        ],
        out_specs=pl.BlockSpec((tm, n), row),
        out_shape=jax.ShapeDtypeStruct((t, n), BF16),
        compiler_params=_params(1),
        name=name,
    )(x, gain, shift, scale, *tables, w)


def _pair_rows(q, head0):
    zero = jnp.zeros_like(q)
    return jnp.concatenate([jnp.where(head0, q, zero), jnp.where(head0, zero, q)], axis=0)


def _pair_lanes(p):
    return slice(p * LANES, (p + 1) * LANES)


def _fill_vt(v_ref, vt_ref):
    t = ATT_TILE
    for p in range(v_ref.shape[1] // LANES):
        for n in range(v_ref.shape[0] // t):
            vt_ref[p, n] = v_ref[n * t:(n + 1) * t, _pair_lanes(p)].T


def _tile_masks(strict):
    t = ATT_TILE
    key = lax.broadcasted_iota(jnp.int32, (t, 2 * t), 0)
    qry = lax.broadcasted_iota(jnp.int32, (t, 2 * t), 1) % t
    chan0 = lax.broadcasted_iota(jnp.int32, (LANES, 1), 0) < HEAD_DIM
    return (key < qry) if strict else (key <= qry), chan0


def _load_super_tile(q_ref, p, i, head0):
    t = ATT_TILE
    base = pl.multiple_of(i * 2 * t, 2 * t)
    qa = _pair_rows(q_ref[pl.ds(base, t), _pair_lanes(p)], head0)
    qb = _pair_rows(q_ref[pl.ds(base + t, t), _pair_lanes(p)], head0)
    return jnp.concatenate([qa, qb], axis=0), qb


def _store_super_tile(o_ref, p, i, acc_t, chan0):
    t = ATT_TILE
    base = pl.multiple_of(i * 2 * t, 2 * t)
    for cell in range(2):
        lo = cell * 2 * t
        o_t = jnp.where(chan0, acc_t[:, lo:lo + t], acc_t[:, lo + t:lo + 2 * t])
        o_ref[pl.ds(base + cell * t, t), _pair_lanes(p)] = o_t.T.astype(BF16)


def _scores(k_ref, p, j, q_rows):
    t = ATT_TILE
    return lax.dot_general(k_ref[pl.ds(pl.multiple_of(j * t, t), t), _pair_lanes(p)], q_rows,
                           (((1,), (1,)), ((), ())), preferred_element_type=F32)


def _sb_kernel(q_ref, k_ref, v_ref, o_ref, vt_ref, acc_ref, z_ref):
    t = ATT_TILE
    w = 2 * t
    seq = q_ref.shape[0]
    pairs = range(q_ref.shape[1] // LANES)
    per_half = 2 * len(pairs)
    head0 = lax.broadcasted_iota(jnp.int32, (1, LANES), 1) < HEAD_DIM
    strict, chan0 = _tile_masks(True)
    from_key = -(lax.broadcasted_iota(jnp.int32, (t, t), 1)
                 >= lax.broadcasted_iota(jnp.int32, (t, t), 0)).astype(BF16)
    tri2 = jnp.concatenate([from_key, from_key], axis=1)
    _fill_vt(v_ref, vt_ref)

    def block(p, j, slot, lanes, run, mask):
        z = z_ref[slot, :, lanes]
        neg_l = jnp.maximum(z, 0.0) + jnp.log2(1.0 + jnp.exp2(-jnp.abs(z)))
        if mask is not None:
            neg_l = jnp.where(mask, neg_l, 0.0)
        hi = neg_l.astype(BF16)
        lo = (neg_l - hi.astype(F32)).astype(BF16)
        since = jnp.dot(tri2, jnp.concatenate([hi, lo], axis=0), preferred_element_type=F32)
        a = jnp.exp2(z_ref[slot, :, lanes] + since)
        if mask is not None:
            a = jnp.where(mask, a, 0.0)
        part = jnp.dot(vt_ref[p, j], a.astype(BF16), preferred_element_type=F32)
        return run + since[0:1, :], part * jnp.exp2(run)

    def stage_pair_of_blocks(half, j, tiles):
        for p in pairs:
            for b in range(2):
                z_ref[half + 2 * p + b] = _scores(k_ref, p, j + b, tiles[p][0])

    def q_super(i, _):
        tiles = [_load_super_tile(q_ref, p, i, head0) for p in pairs]
        mask = jnp.concatenate([strict, jnp.ones((t, w), jnp.bool_)], axis=1)
        cell1 = slice(w, 2 * w)
        diag = lax.rem(i, 2) * per_half
        for p in pairs:
            q2, qb = tiles[p]
            z_ref[diag + 2 * p + 1, :, cell1] = _scores(k_ref, p, 2 * i + 1, qb)
            z_ref[diag + 2 * p] = _scores(k_ref, p, 2 * i, q2)
        runs = []
        for p in pairs:
            run_b, part = block(p, 2 * i + 1, diag + 2 * p + 1, cell1, jnp.zeros((1, w), F32),
                                strict)
            acc_ref[p, :, w:] = part
            run = jnp.concatenate([jnp.zeros((1, w), F32), run_b], axis=1)
            run, part = block(p, 2 * i, diag + 2 * p, slice(None), run, mask)
            acc_ref[p, :, :w] = part[:, :w]
            acc_ref[p, :, w:] += part[:, w:]
            runs.append(run)

        def past(s, runs):
            j = 2 * (i - 1 - s)
            half = lax.rem(s, 2) * per_half
            stage_pair_of_blocks(half, j, tiles)
            out = []
            for p in pairs:
                run, part_b = block(p, j + 1, half + 2 * p + 1, slice(None), runs[p], None)
                run, part_a = block(p, j, half + 2 * p, slice(None), run, None)
                acc_ref[p] += part_b + part_a
                out.append(run)
            return tuple(out)

        lax.fori_loop(0, i, past, tuple(runs))
        for p in pairs:
            _store_super_tile(o_ref, p, i, acc_ref[p], chan0)
        return 0

    lax.fori_loop(0, seq // w, q_super, 0)


def _pair_specs(seq, q_col, k_col, v_col):
    width = PAIRS_PER_STEP * LANES
    return [
        pl.BlockSpec((seq, width), lambda b, p: (b, q_col + p)),
        pl.BlockSpec((seq, width), lambda b, p: (b, k_col + p)),
        pl.BlockSpec((seq, width), lambda b, p: (b, v_col + p)),
    ]


def _attention_call(kernel, qkv, batch, seq, n_heads, first_head, extra_scratch, name):
    width = PAIRS_PER_STEP * LANES
    groups = n_heads * HEAD_DIM // width
    third = ATT_W // width
    first = first_head * HEAD_DIM // width
    return pl.pallas_call(
        kernel,
        grid=(batch, groups),
        in_specs=_pair_specs(seq, first, third + first, 2 * third + first),
        out_specs=pl.BlockSpec((seq, width), lambda b, p: (b, p)),
        out_shape=jax.ShapeDtypeStruct((batch * seq, groups * width), BF16),
        scratch_shapes=[pltpu.VMEM((PAIRS_PER_STEP, seq // ATT_TILE, LANES, ATT_TILE), BF16),
                        pltpu.VMEM((PAIRS_PER_STEP, LANES, 4 * ATT_TILE), F32),
                        *extra_scratch],
        compiler_params=_params(2, ATT_FLAGS),
        name=name,
    )(qkv, qkv, qkv)


def _sb_attention(qkv, batch, seq):
    staged = pltpu.VMEM((4 * PAIRS_PER_STEP, ATT_TILE, 4 * ATT_TILE), F32)
    return _attention_call(_sb_kernel, qkv, batch, seq, N_SB_HEADS, 0, [staged], "sb_attention")


def _moba_kernel(q_ref, k_ref, v_ref, o_ref, vt_ref, acc_ref, sel_ref, s_ref):
    t = ATT_TILE
    w = 2 * t
    seq = q_ref.shape[0]
    nb = seq // MOBA_BLOCK
    pairs = range(q_ref.shape[1] // LANES)
    per_half = 2 * len(pairs)
    head0 = lax.broadcasted_iota(jnp.int32, (1, LANES), 1) < HEAD_DIM
    causal, chan0 = _tile_masks(False)
    blk = lax.broadcasted_iota(jnp.int32, (nb, 2 * w), 0)
    cell = lax.broadcasted_iota(jnp.int32, (nb, 2 * w), 1) // w
    blk_f = blk.astype(F32)
    kmean = [(jnp.sum(k_ref[:, _pair_lanes(p)].astype(F32).reshape(nb, MOBA_BLOCK, LANES), axis=1)
              * (1.0 / MOBA_BLOCK)).astype(BF16) for p in pairs]
    _fill_vt(v_ref, vt_ref)


    def select(p, i, q2):
        gate = lax.dot_general(kmean[p], q2, (((1,), (1,)), ((), ())),
                               preferred_element_type=F32)
        past_blk = blk < 2 * i + cell
        gate = jnp.where(past_blk, gate, NEG)
        sel = jnp.zeros((nb, 2 * w), F32)
        for _r in range(MOBA_TOPK):
            top = jnp.max(gate, axis=0, keepdims=True)
            idx = jnp.min(jnp.where(gate == top, blk_f, float(nb)), axis=0, keepdims=True)
            hit = blk_f == idx
            sel = jnp.where(hit & past_blk, 1.0, sel)
            gate = jnp.where(hit, -jnp.inf, gate)
        sel_ref[p] = sel

    def own_blocks(p, i, raw_a, raw_b):
        s = jnp.where(causal, raw_b, NEG)
        m_b = jnp.max(s, axis=0, keepdims=True)
        pr = jnp.exp2(s - m_b)
        acc_ref[p, :, :w] = jnp.zeros((LANES, w), F32)
        acc_ref[p, :, w:] = jnp.dot(vt_ref[p, 2 * i + 1], pr.astype(BF16),
                                    preferred_element_type=F32)
        m = jnp.concatenate([jnp.full((1, w), -jnp.inf, F32), m_b], axis=1)
        l = jnp.concatenate([jnp.zeros((1, w), F32), jnp.sum(pr, axis=0, keepdims=True)], axis=1)
        keep = jnp.concatenate(
            [causal, jnp.broadcast_to(sel_ref[p, pl.ds(2 * i, 1), w:] > 0.0, (t, w))], axis=1)
        s = jnp.where(keep, raw_a, NEG)
        m_new = jnp.maximum(m, jnp.max(s, axis=0, keepdims=True))
        alpha = jnp.exp2(m - m_new)
        pr = jnp.exp2(s - m_new)
        acc_ref[p] = alpha * acc_ref[p] + jnp.dot(
            vt_ref[p, 2 * i], pr.astype(BF16), preferred_element_type=F32)
        return m_new, alpha * l + jnp.sum(pr, axis=0, keepdims=True)

    def past_pair(p, j, raw_a, raw_b, m_old, l_old):
        s_a = jnp.where(sel_ref[p, pl.ds(j, 1), :] > 0.0, raw_a, NEG)
        s_b = jnp.where(sel_ref[p, pl.ds(j + 1, 1), :] > 0.0, raw_b, NEG)
        m_new = jnp.maximum(m_old, jnp.maximum(jnp.max(s_a, axis=0, keepdims=True),
                                               jnp.max(s_b, axis=0, keepdims=True)))
        alpha = jnp.exp2(m_old - m_new)
        p_a = jnp.exp2(s_a - m_new)
        p_b = jnp.exp2(s_b - m_new)
        part = (jnp.dot(vt_ref[p, j], p_a.astype(BF16), preferred_element_type=F32)
                + jnp.dot(vt_ref[p, j + 1], p_b.astype(BF16), preferred_element_type=F32))
        acc_ref[p] = alpha * acc_ref[p] + part
        l_new = (alpha * l_old + jnp.sum(p_a, axis=0, keepdims=True)
                 + jnp.sum(p_b, axis=0, keepdims=True))
        return m_new, l_new

    def stage_pair_of_blocks(half, j, tiles):
        for p in pairs:
            for b in range(2):
                s_ref[half + 2 * p + b] = _scores(k_ref, p, j + b, tiles[p][0])

    def q_super(i, _):
        tiles = [_load_super_tile(q_ref, p, i, head0) for p in pairs]
        own = lax.rem(i, 2) * per_half
        for p in pairs:
            q2, qb = tiles[p]
            s_ref[own + 2 * p + 1, :, w:] = _scores(k_ref, p, 2 * i + 1, qb)
            s_ref[own + 2 * p] = _scores(k_ref, p, 2 * i, q2)
            select(p, i, q2)
        stats = tuple(own_blocks(p, i, s_ref[own + 2 * p], s_ref[own + 2 * p + 1, :, w:])
                      for p in pairs)

        def past(jj, stats):
            half = lax.rem(jj, 2) * per_half
            stage_pair_of_blocks(half, 2 * jj, tiles)
            return tuple(past_pair(p, 2 * jj, s_ref[half + 2 * p], s_ref[half + 2 * p + 1],
                                   *stats[p]) for p in pairs)

        stats = lax.fori_loop(0, i, past, stats)
        for p in pairs:
            _store_super_tile(o_ref, p, i, acc_ref[p] / stats[p][1], chan0)
        return 0

    lax.fori_loop(0, seq // w, q_super, 0)


def _moba_attention(qkv, batch, seq):
    sel = pltpu.VMEM((PAIRS_PER_STEP, seq // MOBA_BLOCK, 4 * ATT_TILE), F32)
    staged = pltpu.VMEM((4 * PAIRS_PER_STEP, ATT_TILE, 4 * ATT_TILE), F32)
    return _attention_call(_moba_kernel, qkv, batch, seq, N_MOBA_HEADS, N_SB_HEADS,
                           [sel, staged], "moba_attention")


def _ret_kernel(q_ref, k_ref, v_ref, g_ref, gn_ref, o_ref, state_ref, decay_ref):
    c = RET_CHUNK
    heads = range(RET_HEADS_PER_STEP)
    n_row = lax.broadcasted_iota(jnp.int32, (c, 1), 0).astype(F32)
    log_g, xi, zeta, chunk_decay = [], [], [], []
    for hh in heads:
        head = (pl.program_id(1) * RET_HEADS_PER_STEP + hh + 5).astype(F32)
        lg = jnp.log(1.0 - jnp.exp2(-jnp.full((1, 1), head, F32)))
        log_g.append(lg)
        xi.append(jnp.exp(lg * (n_row + 1.0)))
        zeta.append(jnp.exp(lg * (c - 1.0 - n_row)))
        chunk_decay.append(jnp.exp(lg * c))

    @pl.when(pl.program_id(2) == 0)
    def _():
        diff = (lax.broadcasted_iota(jnp.int32, (c, c), 0)
                - lax.broadcasted_iota(jnp.int32, (c, c), 1)).astype(F32)
        for hh in heads:
            decay_ref[hh] = jnp.where(diff >= 0, jnp.exp(log_g[hh] * jnp.maximum(diff, 0.0)), 0.0)
        state_ref[...] = jnp.zeros_like(state_ref)

    def chunk(i, _):
        rows = pl.ds(pl.multiple_of(i * c, c), c)
        for hh in heads:
            qk_cols = slice(hh * RET_DK, (hh + 1) * RET_DK)
            v_cols = slice(hh * RET_DV, (hh + 1) * RET_DV)
            q, k, v = q_ref[rows, qk_cols], k_ref[rows, qk_cols], v_ref[rows, v_cols]
            inner = lax.dot_general(q, k, (((1,), (1,)), ((), ())),
                                    preferred_element_type=F32) * decay_ref[hh]
            state = state_ref[hh]
            o = (jnp.dot(inner.astype(BF16), v, preferred_element_type=F32)
                 + jnp.dot(q, state.astype(BF16), preferred_element_type=F32) * xi[hh])
            kz = (k.astype(F32) * zeta[hh]).astype(BF16)
            state_ref[hh] = state * chunk_decay[hh] + lax.dot_general(
                kz, v, (((0,), (0,)), ((), ())), preferred_element_type=F32)
            mu = jnp.mean(o, axis=-1, keepdims=True)
            var = jnp.mean(jnp.square(o - mu), axis=-1, keepdims=True)
            on = (o - mu) * lax.rsqrt(var + GN_EPS) * gn_ref[:, v_cols]
            o_ref[rows, v_cols] = (_silu(g_ref[rows, v_cols].astype(F32)) * on).astype(BF16)
        return 0

    lax.fori_loop(0, q_ref.shape[0] // c, chunk, 0)


def _retention(proj, gn_gain, batch, seq):
    hps = RET_HEADS_PER_STEP
    tile = RET_SEQ_TILE
    tiles = seq // tile
    qk_w, v_w = hps * RET_DK, hps * RET_DV
    k0 = RET_QK_W // qk_w
    v0 = 2 * RET_QK_W // v_w
    g0 = v0 + RET_V_W // v_w
    row = lambda b, h, s: b * tiles + s
    return pl.pallas_call(
        _ret_kernel,
        grid=(batch, N_RET_HEADS // hps, tiles),
        in_specs=[
            pl.BlockSpec((tile, qk_w), lambda b, h, s: (row(b, h, s), h)),
            pl.BlockSpec((tile, qk_w), lambda b, h, s: (row(b, h, s), k0 + h)),
            pl.BlockSpec((tile, v_w), lambda b, h, s: (row(b, h, s), v0 + h)),
            pl.BlockSpec((tile, v_w), lambda b, h, s: (row(b, h, s), g0 + h)),
            pl.BlockSpec((1, v_w), lambda b, h, s: (0, h)),
        ],
        out_specs=pl.BlockSpec((tile, v_w), lambda b, h, s: (row(b, h, s), h)),
        out_shape=jax.ShapeDtypeStruct((batch * seq, RET_V_W), BF16),
        scratch_shapes=[pltpu.VMEM((hps, RET_DK, RET_DV), F32),
                        pltpu.VMEM((hps, RET_CHUNK, RET_CHUNK), F32)],
        compiler_params=_params(3),
        name="retention",
    )(proj, proj, proj, proj, gn_gain)


def _out_proj_kernel(*refs, n_in):
    x_ref, gate_ref = refs[0], refs[1]
    a_refs, w_refs, o_ref = refs[2:2 + n_in], refs[2 + n_in:2 + 2 * n_in], refs[2 + 2 * n_in]
    y = jnp.dot(a_refs[0][...], w_refs[0][...], preferred_element_type=F32)
    for a_ref, w_ref in zip(a_refs[1:], w_refs[1:]):
        y += jnp.dot(a_ref[...], w_ref[...], preferred_element_type=F32)
    o_ref[...] = x_ref[...] + gate_ref[0] * y


def _out_proj(x, gate, acts, weights, seq, name):
    t, d = x.shape
    tm = ROW_TILE
    per_seq = seq // tm
    row = lambda i: (i, 0)
    return pl.pallas_call(
        functools.partial(_out_proj_kernel, n_in=len(acts)),
        grid=(t // tm,),
        in_specs=[
            pl.BlockSpec((tm, d), row),
            pl.BlockSpec((1, 1, d), lambda i: (i // per_seq, 0, 0)),
            *[pl.BlockSpec((tm, a.shape[1]), row) for a in acts],
            *[_resident(w.shape, lambda i: (0, 0)) for w in weights],
        ],
        out_specs=pl.BlockSpec((tm, d), row),
        out_shape=jax.ShapeDtypeStruct((t, d), F32),
        compiler_params=_params(1),
        name=name,
    )(x, gate, *acts, *weights)


def _ffn_kernel(x_ref, g_ref, sh_ref, sc_ref, gate_ref, wgu_ref, wd_ref, fin_ref, o_ref,
                *, final_norm):
    x = x_ref[...]
    h = _norm_mod(x, g_ref[...], sh_ref[0], sc_ref[0]).astype(BF16)
    half = D_FF // 2
    y = None
    for c in range(2):
        cols = slice(c * half, (c + 1) * half)
        gt = jnp.dot(h, wgu_ref[:, cols], preferred_element_type=F32)
        up = jnp.dot(h, wgu_ref[:, D_FF + c * half:D_FF + (c + 1) * half],
                     preferred_element_type=F32)
        part = jnp.dot((_silu(gt) * up).astype(BF16), wd_ref[cols, :], preferred_element_type=F32)
        y = part if y is None else y + part
    out = x + gate_ref[0] * y
    if final_norm:
        out = out * lax.rsqrt(jnp.mean(out * out, axis=-1, keepdims=True) + NORM_EPS) * fin_ref[...]
    o_ref[...] = out


def _ffn(x, gain, shift, scale, gate, w_gate_up, w_down, layer, fin_gain, seq, final_norm, name):
    t, d = x.shape
    tm = ROW_TILE
    per_seq = seq // tm
    row = lambda i: (i, 0)
    batch = lambda i: (i // per_seq, 0, 0)
    const = lambda i: (0, 0)
    this_layer = lambda i: (layer, 0, 0)
    return pl.pallas_call(
        functools.partial(_ffn_kernel, final_norm=final_norm),
        grid=(t // tm,),
        in_specs=[
            pl.BlockSpec((tm, d), row),
            pl.BlockSpec((1, d), const),
            pl.BlockSpec((1, 1, d), batch),
            pl.BlockSpec((1, 1, d), batch),
            pl.BlockSpec((1, 1, d), batch),
            _resident((None,) + w_gate_up.shape[1:], this_layer),
            _resident((None,) + w_down.shape[1:], this_layer),
            pl.BlockSpec((1, d), const),
        ],
        out_specs=pl.BlockSpec((tm, d), row),
        out_shape=jax.ShapeDtypeStruct((t, d), F32),
        compiler_params=_params(1),
        name=name,
    )(x, gain, shift, scale, gate, w_gate_up, w_down, fin_gain)


def _rope_tables(seq, dim, tile):
    inv = ROPE_THETA ** (-jnp.arange(0, dim, 2, dtype=F32) / dim)
    ang = jnp.arange(seq, dtype=jnp.int32).astype(F32)[:, None] * inv[None, :]
    return jnp.tile(jnp.cos(ang), (1, tile)), jnp.tile(jnp.sin(ang), (1, tile))


def kernel(x, c, ada_w, ada_b, norm_gains, att_w_qkv, att_w_o, ret_w_in, ret_gn, ret_w_o,
           ffn_w_gate_up, ffn_w_down, final_norm):
    batch, seq, d = x.shape
    depth = ada_w.shape[0]
    assert d == D_MODEL and batch <= 8 and seq % ROW_TILE == 0
    assert seq % (2 * ATT_TILE) == 0 and seq % RET_SEQ_TILE == 0

    mod = _ada_mod(c, ada_w, ada_b)[:, :batch].reshape(depth, batch, 1, 6, d)
    xt = x.reshape(batch * seq, d)

    half = HEAD_DIM // 2
    cos_a, sin_a = _rope_tables(seq, HEAD_DIM, LANES // half)
    first_half = (jnp.arange(LANES) % HEAD_DIM) < half
    att_tables = (cos_a, jnp.where(first_half, -sin_a, 0.0), jnp.where(first_half, 0.0, sin_a))
    ret_tables = _rope_tables(seq, RET_DK, 1)
    w_gate_up, w_down = ffn_w_gate_up.astype(BF16), ffn_w_down.astype(BF16)

    for layer in range(depth):
        shift_m, scale_m, gate_m, shift_f, scale_f, gate_f = (mod[layer, :, :, i] for i in range(6))
        gains = norm_gains[layer]
        j = layer // 2
        if layer % 2 == 0:
            qkv = _proj(_att_proj_kernel, xt, gains[0:1], shift_m, scale_m, att_tables,
                        att_w_qkv[j].astype(BF16), seq, "att_proj")
            sb = _sb_attention(qkv, batch, seq)
            mb = _moba_attention(qkv, batch, seq)
            w_o = att_w_o[j].astype(BF16)
            split = N_SB_HEADS * HEAD_DIM
            xt = _out_proj(xt, gate_m, (sb, mb), (w_o[:split], w_o[split:]), seq, "att_out")
        else:
            proj = _proj(_ret_proj_kernel, xt, gains[0:1], shift_m, scale_m, ret_tables,
                         ret_w_in[j].astype(BF16), seq, "ret_proj")
            ro = _retention(proj, ret_gn[j].reshape(1, RET_V_W), batch, seq)
            xt = _out_proj(xt, gate_m, (ro,), (ret_w_o[j].astype(BF16),), seq, "ret_out")
        xt = _ffn(xt, gains[1:2], shift_f, scale_f, gate_f, w_gate_up, w_down, layer,
                  final_norm.reshape(1, d), seq, layer == depth - 1, "ffn%d" % layer)
    return xt.reshape(batch, seq, d)
```

```python
import functools
import math

import jax
import jax.numpy as jnp
from jax import lax
from jax.experimental import pallas as pl
from jax.experimental.pallas import tpu as pltpu

F32 = jnp.float32
BF16 = jnp.bfloat16

D_MODEL = 1024
HEAD_DIM = 64
N_SB_HEADS = 8
N_MOBA_HEADS = 8
ATT_W = (N_SB_HEADS + N_MOBA_HEADS) * HEAD_DIM
MOBA_BLOCK = 256
MOBA_TOPK = 3
N_RET_HEADS = 4
RET_DK = D_MODEL // N_RET_HEADS
RET_DV = 2 * RET_DK
RET_QK_W = N_RET_HEADS * RET_DK
RET_V_W = N_RET_HEADS * RET_DV
D_FF = -(-8 * D_MODEL // 768) * 256
ROPE_THETA = 10000.0
NORM_EPS = 1e-6
GN_EPS = 1e-5
NEG = -1e30
Q_SCALE = HEAD_DIM ** -0.5 * math.log2(math.e)

LANES = 128
ROW_TILE = 512
COL_CHUNK = 512
ATT_TILE = 256
PAIRS_PER_STEP = 2
ATT_FLAGS = None
RET_CHUNK = 256
RET_SEQ_TILE = 1024
RET_HEADS_PER_STEP = 2
VMEM_LIMIT = 56 * 1024 * 1024


def _params(n_axes, flags=None):
    return pltpu.CompilerParams(
        dimension_semantics=("arbitrary",) * n_axes, vmem_limit_bytes=VMEM_LIMIT, flags=flags)


def _resident(shape, index_map):
    return pl.BlockSpec(shape, index_map, pipeline_mode=pl.Buffered(1))


def _silu(v):
    return v * (1.0 / (1.0 + jnp.exp(-v)))


def _ada_kernel(c_ref, w_ref, b_ref, o_ref):
    o_ref[0] = jnp.dot(_silu(c_ref[...]), w_ref[0], preferred_element_type=F32) + b_ref[0]


def _ada_mod(c, ada_w, ada_b):
    depth, d, n = ada_w.shape
    cp = jnp.pad(c, ((0, 8 - c.shape[0]), (0, 0)))
    tn = 1536
    return pl.pallas_call(
        _ada_kernel,
        grid=(depth, n // tn),
        in_specs=[
            pl.BlockSpec((8, d), lambda l, j: (0, 0)),
            pl.BlockSpec((1, d, tn), lambda l, j: (l, 0, j)),
            pl.BlockSpec((1, 1, tn), lambda l, j: (l, 0, j)),
        ],
        out_specs=pl.BlockSpec((1, 8, tn), lambda l, j: (l, 0, j)),
        out_shape=jax.ShapeDtypeStruct((depth, 8, n), F32),
        compiler_params=_params(2),
        name="ada_mod",
    )(cp, ada_w, ada_b.reshape(depth, 1, n))


def _norm_mod(x, g, shift, scale):
    y = x * lax.rsqrt(jnp.mean(x * x, axis=-1, keepdims=True) + NORM_EPS)
    return (y * g) * (1.0 + scale) + shift


def _att_proj_kernel(x_ref, g_ref, sh_ref, sc_ref, cos_ref, sina_ref, sinb_ref, w_ref, o_ref):
    h = _norm_mod(x_ref[...], g_ref[...], sh_ref[0], sc_ref[0]).astype(BF16)
    cos, sina, sinb = cos_ref[...], sina_ref[...], sinb_ref[...]
    sb_w = N_SB_HEADS * HEAD_DIM
    for c in range(3 * ATT_W // COL_CHUNK):
        lo = c * COL_CHUNK
        y = jnp.dot(h, w_ref[:, lo:lo + COL_CHUNK], preferred_element_type=F32)
        is_q = lo < ATT_W
        rotated = (lo % ATT_W) >= sb_w and lo < 2 * ATT_W
        for s in range(COL_CHUNK // LANES):
            ys = y[:, s * LANES:(s + 1) * LANES]
            if rotated:
                ys = (ys * cos + pltpu.roll(ys, LANES - HEAD_DIM // 2, 1) * sina
                      + pltpu.roll(ys, HEAD_DIM // 2, 1) * sinb)
            if is_q:
                ys = ys * Q_SCALE
            o_ref[:, lo + s * LANES:lo + (s + 1) * LANES] = ys.astype(BF16)


def _ret_proj_kernel(x_ref, g_ref, sh_ref, sc_ref, cos_ref, sin_ref, w_ref, o_ref):
    h = _norm_mod(x_ref[...], g_ref[...], sh_ref[0], sc_ref[0]).astype(BF16)
    cos, sin = cos_ref[...], sin_ref[...]
    n = 2 * RET_QK_W + 2 * RET_V_W
    for c in range(n // COL_CHUNK):
        lo = c * COL_CHUNK
        y = jnp.dot(h, w_ref[:, lo:lo + COL_CHUNK], preferred_element_type=F32)
        if lo < 2 * RET_QK_W:
            mul = 1.0 if lo < RET_QK_W else RET_DK ** -0.5
            half = RET_DK // 2
            for hd in range(COL_CHUNK // RET_DK):
                x1 = y[:, hd * RET_DK:hd * RET_DK + half]
                x2 = y[:, hd * RET_DK + half:(hd + 1) * RET_DK]
                o1 = (x1 * cos - x2 * sin) * mul
                o2 = (x2 * cos + x1 * sin) * mul
                o_ref[:, lo + hd * RET_DK:lo + hd * RET_DK + half] = o1.astype(BF16)
                o_ref[:, lo + hd * RET_DK + half:lo + (hd + 1) * RET_DK] = o2.astype(BF16)
        else:
            o_ref[:, lo:lo + COL_CHUNK] = y.astype(BF16)


def _proj(kernel, x, gain, shift, scale, tables, w, seq, name):
    t, d = x.shape
    n = w.shape[1]
    tm = ROW_TILE
    per_seq = seq // tm
    row = lambda i: (i, 0)
    batch = lambda i: (i // per_seq, 0, 0)
    pos = lambda i: (i % per_seq, 0)
    return pl.pallas_call(
        kernel,
        grid=(t // tm,),
        in_specs=[
            pl.BlockSpec((tm, d), row),
            pl.BlockSpec((1, d), lambda i: (0, 0)),
            pl.BlockSpec((1, 1, d), batch),
            pl.BlockSpec((1, 1, d), batch),
            *[pl.BlockSpec((tm, LANES), pos) for _ in tables],
            _resident((d, n), lambda i: (0, 0)),
        ],
        out_specs=pl.BlockSpec((tm, n), row),
        out_shape=jax.ShapeDtypeStruct((t, n), BF16),
        compiler_params=_params(1),
        name=name,
    )(x, gain, shift, scale, *tables, w)


def _pair_lanes(p):
    return slice(p * LANES, (p + 1) * LANES)


def _fill_vt(v_ref, vt_ref):
    t = ATT_TILE
    for p in range(v_ref.shape[1] // LANES):
        for n in range(v_ref.shape[0] // t):
            vt_ref[p, n] = v_ref[n * t:(n + 1) * t, _pair_lanes(p)].T


def _tile_masks(strict):
    t = ATT_TILE
    key = lax.broadcasted_iota(jnp.int32, (t, 2 * t), 0)
    qry = lax.broadcasted_iota(jnp.int32, (t, 2 * t), 1) % t
    chan0 = lax.broadcasted_iota(jnp.int32, (LANES, 1), 0) < HEAD_DIM
    return (key < qry) if strict else (key <= qry), chan0


def _load_super_tile(q_ref, p, i, chan0):
    t = ATT_TILE
    base = pl.multiple_of(i * 2 * t, 2 * t)
    cells = []
    for c in range(2):
        q_t = q_ref[pl.ds(base + c * t, t), _pair_lanes(p)].T
        zero = jnp.zeros_like(q_t)
        cells += [jnp.where(chan0, q_t, zero), jnp.where(chan0, zero, q_t)]
    return jnp.concatenate(cells, axis=1), jnp.concatenate(cells[2:], axis=1)


def _store_super_tile(o_ref, p, i, acc_t, chan0):
    t = ATT_TILE
    base = pl.multiple_of(i * 2 * t, 2 * t)
    for cell in range(2):
        lo = cell * 2 * t
        o_t = jnp.where(chan0, acc_t[:, lo:lo + t], acc_t[:, lo + t:lo + 2 * t])
        o_ref[pl.ds(base + cell * t, t), _pair_lanes(p)] = o_t.T.astype(BF16)


def _scores(k_ref, p, j, q_cols):
    t = ATT_TILE
    return jnp.dot(k_ref[pl.ds(pl.multiple_of(j * t, t), t), _pair_lanes(p)], q_cols,
                   preferred_element_type=F32)


def _sb_kernel(q_ref, k_ref, v_ref, o_ref, vt_ref, acc_ref, z_ref):
    t = ATT_TILE
    w = 2 * t
    seq = q_ref.shape[0]
    pairs = range(q_ref.shape[1] // LANES)
    per_half = 2 * len(pairs)
    strict, chan0 = _tile_masks(True)
    from_key = -(lax.broadcasted_iota(jnp.int32, (t, t), 1)
                 >= lax.broadcasted_iota(jnp.int32, (t, t), 0)).astype(BF16)
    tri2 = jnp.concatenate([from_key, from_key], axis=1)
    _fill_vt(v_ref, vt_ref)

    def block(p, j, slot, lanes, run, mask):
        z = z_ref[slot, :, lanes]
        neg_l = jnp.maximum(z, 0.0) + jnp.log2(1.0 + jnp.exp2(-jnp.abs(z)))
        if mask is not None:
            neg_l = jnp.where(mask, neg_l, 0.0)
        hi = neg_l.astype(BF16)
        lo = (neg_l - hi.astype(F32)).astype(BF16)
        since = jnp.dot(tri2, jnp.concatenate([hi, lo], axis=0), preferred_element_type=F32)
        a = jnp.exp2(z_ref[slot, :, lanes] + since)
        if mask is not None:
            a = jnp.where(mask, a, 0.0)
        part = jnp.dot(vt_ref[p, j], a.astype(BF16), preferred_element_type=F32)
        return run + since[0:1, :], part * jnp.exp2(run)

    def stage_pair_of_blocks(half, j, tiles):
        for p in pairs:
            for b in range(2):
                z_ref[half + 2 * p + b] = _scores(k_ref, p, j + b, tiles[p][0])

    def q_super(i, _):
        tiles = [_load_super_tile(q_ref, p, i, chan0) for p in pairs]
        mask = jnp.concatenate([strict, jnp.ones((t, w), jnp.bool_)], axis=1)
        cell1 = slice(w, 2 * w)
        diag = lax.rem(i, 2) * per_half
        for p in pairs:
            q2, qb = tiles[p]
            z_ref[diag + 2 * p + 1, :, cell1] = _scores(k_ref, p, 2 * i + 1, qb)
            z_ref[diag + 2 * p] = _scores(k_ref, p, 2 * i, q2)
        runs = []
        for p in pairs:
            run_b, part = block(p, 2 * i + 1, diag + 2 * p + 1, cell1, jnp.zeros((1, w), F32),
                                strict)
            acc_ref[p, :, w:] = part
            run = jnp.concatenate([jnp.zeros((1, w), F32), run_b], axis=1)
            run, part = block(p, 2 * i, diag + 2 * p, slice(None), run, mask)
            acc_ref[p, :, :w] = part[:, :w]
            acc_ref[p, :, w:] += part[:, w:]
            runs.append(run)

        def past(s, runs):
            j = 2 * (i - 1 - s)
            half = lax.rem(s, 2) * per_half
            stage_pair_of_blocks(half, j, tiles)
            out = []
            for p in pairs:
                run, part_b = block(p, j + 1, half + 2 * p + 1, slice(None), runs[p], None)
                run, part_a = block(p, j, half + 2 * p, slice(None), run, None)
                acc_ref[p] += part_b + part_a
                out.append(run)
            return tuple(out)

        lax.fori_loop(0, i, past, tuple(runs))
        for p in pairs:
            _store_super_tile(o_ref, p, i, acc_ref[p], chan0)
        return 0

    lax.fori_loop(0, seq // w, q_super, 0)


def _pair_specs(seq, q_col, k_col, v_col):
    width = PAIRS_PER_STEP * LANES
    return [
        pl.BlockSpec((seq, width), lambda b, p: (b, q_col + p)),
        pl.BlockSpec((seq, width), lambda b, p: (b, k_col + p)),
        pl.BlockSpec((seq, width), lambda b, p: (b, v_col + p)),
    ]


def _attention_call(kernel, qkv, batch, seq, n_heads, first_head, extra_scratch, name):
    width = PAIRS_PER_STEP * LANES
    groups = n_heads * HEAD_DIM // width
    third = ATT_W // width
    first = first_head * HEAD_DIM // width
    return pl.pallas_call(
        kernel,
        grid=(batch, groups),
        in_specs=_pair_specs(seq, first, third + first, 2 * third + first),
        out_specs=pl.BlockSpec((seq, width), lambda b, p: (b, p)),
        out_shape=jax.ShapeDtypeStruct((batch * seq, groups * width), BF16),
        scratch_shapes=[pltpu.VMEM((PAIRS_PER_STEP, seq // ATT_TILE, LANES, ATT_TILE), BF16),
                        pltpu.VMEM((PAIRS_PER_STEP, LANES, 4 * ATT_TILE), F32),
                        *extra_scratch],
        compiler_params=_params(2, ATT_FLAGS),
        name=name,
    )(qkv, qkv, qkv)


def _sb_attention(qkv, batch, seq):
    staged = pltpu.VMEM((4 * PAIRS_PER_STEP, ATT_TILE, 4 * ATT_TILE), F32)
    return _attention_call(_sb_kernel, qkv, batch, seq, N_SB_HEADS, 0, [staged], "sb_attention")


def _moba_kernel(q_ref, k_ref, v_ref, o_ref, vt_ref, acc_ref, sel_ref, s_ref):
    t = ATT_TILE
    w = 2 * t
    seq = q_ref.shape[0]
    nb = seq // MOBA_BLOCK
    pairs = range(q_ref.shape[1] // LANES)
    per_half = 2 * len(pairs)
    causal, chan0 = _tile_masks(False)
    blk = lax.broadcasted_iota(jnp.int32, (nb, 2 * w), 0)
    cell = lax.broadcasted_iota(jnp.int32, (nb, 2 * w), 1) // w
    blk_f = blk.astype(F32)
    kmean = [(jnp.sum(k_ref[:, _pair_lanes(p)].astype(F32).reshape(nb, MOBA_BLOCK, LANES), axis=1)
              * (1.0 / MOBA_BLOCK)).astype(BF16) for p in pairs]
    _fill_vt(v_ref, vt_ref)


    def select(p, i, q2):
        gate = jnp.dot(kmean[p], q2, preferred_element_type=F32)
        past_blk = blk < 2 * i + cell
        gate = jnp.where(past_blk, gate, NEG)
        sel = jnp.zeros((nb, 2 * w), F32)
        for _r in range(MOBA_TOPK):
            top = jnp.max(gate, axis=0, keepdims=True)
            idx = jnp.min(jnp.where(gate == top, blk_f, float(nb)), axis=0, keepdims=True)
            hit = blk_f == idx
            sel = jnp.where(hit & past_blk, 1.0, sel)
            gate = jnp.where(hit, -jnp.inf, gate)
        sel_ref[p] = sel

    def own_blocks(p, i, raw_a, raw_b):
        s = jnp.where(causal, raw_b, NEG)
        m_b = jnp.max(s, axis=0, keepdims=True)
        pr = jnp.exp2(s - m_b)
        acc_ref[p, :, :w] = jnp.zeros((LANES, w), F32)
        acc_ref[p, :, w:] = jnp.dot(vt_ref[p, 2 * i + 1], pr.astype(BF16),
                                    preferred_element_type=F32)
        m = jnp.concatenate([jnp.full((1, w), -jnp.inf, F32), m_b], axis=1)
        l = jnp.concatenate([jnp.zeros((1, w), F32), jnp.sum(pr, axis=0, keepdims=True)], axis=1)
        keep = jnp.concatenate(
            [causal, jnp.broadcast_to(sel_ref[p, pl.ds(2 * i, 1), w:] > 0.0, (t, w))], axis=1)
        s = jnp.where(keep, raw_a, NEG)
        m_new = jnp.maximum(m, jnp.max(s, axis=0, keepdims=True))
        alpha = jnp.exp2(m - m_new)
        pr = jnp.exp2(s - m_new)
        acc_ref[p] = alpha * acc_ref[p] + jnp.dot(
            vt_ref[p, 2 * i], pr.astype(BF16), preferred_element_type=F32)
        return m_new, alpha * l + jnp.sum(pr, axis=0, keepdims=True)

    def past_pair(p, j, raw_a, raw_b, m_old, l_old):
        s_a = jnp.where(sel_ref[p, pl.ds(j, 1), :] > 0.0, raw_a, NEG)
        s_b = jnp.where(sel_ref[p, pl.ds(j + 1, 1), :] > 0.0, raw_b, NEG)
        m_new = jnp.maximum(m_old, jnp.maximum(jnp.max(s_a, axis=0, keepdims=True),
                                               jnp.max(s_b, axis=0, keepdims=True)))
        alpha = jnp.exp2(m_old - m_new)
        p_a = jnp.exp2(s_a - m_new)
        p_b = jnp.exp2(s_b - m_new)
        part = (jnp.dot(vt_ref[p, j], p_a.astype(BF16), preferred_element_type=F32)
                + jnp.dot(vt_ref[p, j + 1], p_b.astype(BF16), preferred_element_type=F32))
        acc_ref[p] = alpha * acc_ref[p] + part
        l_new = (alpha * l_old + jnp.sum(p_a, axis=0, keepdims=True)
                 + jnp.sum(p_b, axis=0, keepdims=True))
        return m_new, l_new

    def stage_pair_of_blocks(half, j, tiles):
        for p in pairs:
            for b in range(2):
                s_ref[half + 2 * p + b] = _scores(k_ref, p, j + b, tiles[p][0])

    def q_super(i, _):
        tiles = [_load_super_tile(q_ref, p, i, chan0) for p in pairs]
        own = lax.rem(i, 2) * per_half
        for p in pairs:
            q2, qb = tiles[p]
            s_ref[own + 2 * p + 1, :, w:] = _scores(k_ref, p, 2 * i + 1, qb)
            s_ref[own + 2 * p] = _scores(k_ref, p, 2 * i, q2)
            select(p, i, q2)
        stats = tuple(own_blocks(p, i, s_ref[own + 2 * p], s_ref[own + 2 * p + 1, :, w:])
                      for p in pairs)

        def past(jj, stats):
            half = lax.rem(jj, 2) * per_half
            stage_pair_of_blocks(half, 2 * jj, tiles)
            return tuple(past_pair(p, 2 * jj, s_ref[half + 2 * p], s_ref[half + 2 * p + 1],
                                   *stats[p]) for p in pairs)

        stats = lax.fori_loop(0, i, past, stats)
        for p in pairs:
            _store_super_tile(o_ref, p, i, acc_ref[p] / stats[p][1], chan0)
        return 0

    lax.fori_loop(0, seq // w, q_super, 0)


def _moba_attention(qkv, batch, seq):
    sel = pltpu.VMEM((PAIRS_PER_STEP, seq // MOBA_BLOCK, 4 * ATT_TILE), F32)
    staged = pltpu.VMEM((4 * PAIRS_PER_STEP, ATT_TILE, 4 * ATT_TILE), F32)
    return _attention_call(_moba_kernel, qkv, batch, seq, N_MOBA_HEADS, N_SB_HEADS,
                           [sel, staged], "moba_attention")


def _ret_kernel(q_ref, k_ref, v_ref, g_ref, gn_ref, o_ref, state_ref, decay_ref):
    c = RET_CHUNK
    heads = range(RET_HEADS_PER_STEP)
    n_row = lax.broadcasted_iota(jnp.int32, (c, 1), 0).astype(F32)
    log_g, xi, zeta, chunk_decay = [], [], [], []
    for hh in heads:
        head = (pl.program_id(1) * RET_HEADS_PER_STEP + hh + 5).astype(F32)
        lg = jnp.log(1.0 - jnp.exp2(-jnp.full((1, 1), head, F32)))
        log_g.append(lg)
        xi.append(jnp.exp(lg * (n_row + 1.0)))
        zeta.append(jnp.exp(lg * (c - 1.0 - n_row)))
        chunk_decay.append(jnp.exp(lg * c))

    @pl.when(pl.program_id(2) == 0)
    def _():
        diff = (lax.broadcasted_iota(jnp.int32, (c, c), 0)
                - lax.broadcasted_iota(jnp.int32, (c, c), 1)).astype(F32)
        for hh in heads:
            decay_ref[hh] = jnp.where(diff >= 0, jnp.exp(log_g[hh] * jnp.maximum(diff, 0.0)), 0.0)
        state_ref[...] = jnp.zeros_like(state_ref)

    def chunk(i, _):
        rows = pl.ds(pl.multiple_of(i * c, c), c)
        for hh in heads:
            qk_cols = slice(hh * RET_DK, (hh + 1) * RET_DK)
            v_cols = slice(hh * RET_DV, (hh + 1) * RET_DV)
            q, k, v = q_ref[rows, qk_cols], k_ref[rows, qk_cols], v_ref[rows, v_cols]
            inner = lax.dot_general(q, k, (((1,), (1,)), ((), ())),
                                    preferred_element_type=F32) * decay_ref[hh]
            state = state_ref[hh]
            o = (jnp.dot(inner.astype(BF16), v, preferred_element_type=F32)
                 + jnp.dot(q, state.astype(BF16), preferred_element_type=F32) * xi[hh])
            kz = (k.astype(F32) * zeta[hh]).astype(BF16)
            state_ref[hh] = state * chunk_decay[hh] + lax.dot_general(
                kz, v, (((0,), (0,)), ((), ())), preferred_element_type=F32)
            mu = jnp.mean(o, axis=-1, keepdims=True)
            var = jnp.mean(jnp.square(o - mu), axis=-1, keepdims=True)
            on = (o - mu) * lax.rsqrt(var + GN_EPS) * gn_ref[:, v_cols]
            o_ref[rows, v_cols] = (_silu(g_ref[rows, v_cols].astype(F32)) * on).astype(BF16)
        return 0

    lax.fori_loop(0, q_ref.shape[0] // c, chunk, 0)


def _retention(proj, gn_gain, batch, seq):
    hps = RET_HEADS_PER_STEP
    tile = RET_SEQ_TILE
    tiles = seq // tile
    qk_w, v_w = hps * RET_DK, hps * RET_DV
    k0 = RET_QK_W // qk_w
    v0 = 2 * RET_QK_W // v_w
    g0 = v0 + RET_V_W // v_w
    row = lambda b, h, s: b * tiles + s
    return pl.pallas_call(
        _ret_kernel,
        grid=(batch, N_RET_HEADS // hps, tiles),
        in_specs=[
            pl.BlockSpec((tile, qk_w), lambda b, h, s: (row(b, h, s), h)),
            pl.BlockSpec((tile, qk_w), lambda b, h, s: (row(b, h, s), k0 + h)),
            pl.BlockSpec((tile, v_w), lambda b, h, s: (row(b, h, s), v0 + h)),
            pl.BlockSpec((tile, v_w), lambda b, h, s: (row(b, h, s), g0 + h)),
            pl.BlockSpec((1, v_w), lambda b, h, s: (0, h)),
        ],
        out_specs=pl.BlockSpec((tile, v_w), lambda b, h, s: (row(b, h, s), h)),
        out_shape=jax.ShapeDtypeStruct((batch * seq, RET_V_W), BF16),
        scratch_shapes=[pltpu.VMEM((hps, RET_DK, RET_DV), F32),
                        pltpu.VMEM((hps, RET_CHUNK, RET_CHUNK), F32)],
        compiler_params=_params(3),
        name="retention",
    )(proj, proj, proj, proj, gn_gain)


def _out_proj_kernel(*refs, n_in):
    x_ref, gate_ref = refs[0], refs[1]
    a_refs, w_refs, o_ref = refs[2:2 + n_in], refs[2 + n_in:2 + 2 * n_in], refs[2 + 2 * n_in]
    y = jnp.dot(a_refs[0][...], w_refs[0][...], preferred_element_type=F32)
    for a_ref, w_ref in zip(a_refs[1:], w_refs[1:]):
        y += jnp.dot(a_ref[...], w_ref[...], preferred_element_type=F32)
    o_ref[...] = x_ref[...] + gate_ref[0] * y


def _out_proj(x, gate, acts, weights, seq, name):
    t, d = x.shape
    tm = ROW_TILE
    per_seq = seq // tm
    row = lambda i: (i, 0)
    return pl.pallas_call(
        functools.partial(_out_proj_kernel, n_in=len(acts)),
        grid=(t // tm,),
        in_specs=[
            pl.BlockSpec((tm, d), row),
            pl.BlockSpec((1, 1, d), lambda i: (i // per_seq, 0, 0)),
            *[pl.BlockSpec((tm, a.shape[1]), row) for a in acts],
            *[_resident(w.shape, lambda i: (0, 0)) for w in weights],
        ],
        out_specs=pl.BlockSpec((tm, d), row),
        out_shape=jax.ShapeDtypeStruct((t, d), F32),
        compiler_params=_params(1),
        name=name,
    )(x, gate, *acts, *weights)


def _ffn_kernel(x_ref, g_ref, sh_ref, sc_ref, gate_ref, wgu_ref, wd_ref, fin_ref, o_ref,
                *, final_norm):
    x = x_ref[...]
    h = _norm_mod(x, g_ref[...], sh_ref[0], sc_ref[0]).astype(BF16)
    half = D_FF // 2
    y = None
    for c in range(2):
        cols = slice(c * half, (c + 1) * half)
        gt = jnp.dot(h, wgu_ref[:, cols], preferred_element_type=F32)
        up = jnp.dot(h, wgu_ref[:, D_FF + c * half:D_FF + (c + 1) * half],
                     preferred_element_type=F32)
        part = jnp.dot((_silu(gt) * up).astype(BF16), wd_ref[cols, :], preferred_element_type=F32)
        y = part if y is None else y + part
    out = x + gate_ref[0] * y
    if final_norm:
        out = out * lax.rsqrt(jnp.mean(out * out, axis=-1, keepdims=True) + NORM_EPS) * fin_ref[...]
    o_ref[...] = out


def _ffn(x, gain, shift, scale, gate, w_gate_up, w_down, layer, fin_gain, seq, final_norm, name):
    t, d = x.shape
    tm = ROW_TILE
    per_seq = seq // tm
    row = lambda i: (i, 0)
    batch = lambda i: (i // per_seq, 0, 0)
    const = lambda i: (0, 0)
    this_layer = lambda i: (layer, 0, 0)
    return pl.pallas_call(
        functools.partial(_ffn_kernel, final_norm=final_norm),
        grid=(t // tm,),
        in_specs=[
            pl.BlockSpec((tm, d), row),
            pl.BlockSpec((1, d), const),
            pl.BlockSpec((1, 1, d), batch),
            pl.BlockSpec((1, 1, d), batch),
            pl.BlockSpec((1, 1, d), batch),
            _resident((None,) + w_gate_up.shape[1:], this_layer),
            _resident((None,) + w_down.shape[1:], this_layer),
            pl.BlockSpec((1, d), const),
        ],
        out_specs=pl.BlockSpec((tm, d), row),
        out_shape=jax.ShapeDtypeStruct((t, d), F32),
        compiler_params=_params(1),
        name=name,
    )(x, gain, shift, scale, gate, w_gate_up, w_down, fin_gain)


def _rope_tables(seq, dim, tile):
    inv = ROPE_THETA ** (-jnp.arange(0, dim, 2, dtype=F32) / dim)
    ang = jnp.arange(seq, dtype=jnp.int32).astype(F32)[:, None] * inv[None, :]
    return jnp.tile(jnp.cos(ang), (1, tile)), jnp.tile(jnp.sin(ang), (1, tile))


def kernel(x, c, ada_w, ada_b, norm_gains, att_w_qkv, att_w_o, ret_w_in, ret_gn, ret_w_o,
           ffn_w_gate_up, ffn_w_down, final_norm):
    batch, seq, d = x.shape
    depth = ada_w.shape[0]
    assert d == D_MODEL and batch <= 8 and seq % ROW_TILE == 0
    assert seq % (2 * ATT_TILE) == 0 and seq % RET_SEQ_TILE == 0

    mod = _ada_mod(c, ada_w, ada_b)[:, :batch].reshape(depth, batch, 1, 6, d)
    xt = x.reshape(batch * seq, d)

    half = HEAD_DIM // 2
    cos_a, sin_a = _rope_tables(seq, HEAD_DIM, LANES // half)
    first_half = (jnp.arange(LANES) % HEAD_DIM) < half
    att_tables = (cos_a, jnp.where(first_half, -sin_a, 0.0), jnp.where(first_half, 0.0, sin_a))
    ret_tables = _rope_tables(seq, RET_DK, 1)
    w_gate_up, w_down = ffn_w_gate_up.astype(BF16), ffn_w_down.astype(BF16)

    for layer in range(depth):
        shift_m, scale_m, gate_m, shift_f, scale_f, gate_f = (mod[layer, :, :, i] for i in range(6))
        gains = norm_gains[layer]
        j = layer // 2
        if layer % 2 == 0:
            qkv = _proj(_att_proj_kernel, xt, gains[0:1], shift_m, scale_m, att_tables,
                        att_w_qkv[j].astype(BF16), seq, "att_proj")
            sb = _sb_attention(qkv, batch, seq)
            mb = _moba_attention(qkv, batch, seq)
            w_o = att_w_o[j].astype(BF16)
            split = N_SB_HEADS * HEAD_DIM
            xt = _out_proj(xt, gate_m, (sb, mb), (w_o[:split], w_o[split:]), seq, "att_out")
        else:
            proj = _proj(_ret_proj_kernel, xt, gains[0:1], shift_m, scale_m, ret_tables,
                         ret_w_in[j].astype(BF16), seq, "ret_proj")
            ro = _retention(proj, ret_gn[j].reshape(1, RET_V_W), batch, seq)
            xt = _out_proj(xt, gate_m, (ro,), (ret_w_o[j].astype(BF16),), seq, "ret_out")
        xt = _ffn(xt, gains[1:2], shift_f, scale_f, gate_f, w_gate_up, w_down, layer,
                  final_norm.reshape(1, d), seq, layer == depth - 1, "ffn%d" % layer)
    return xt.reshape(batch, seq, d)
```

```python
import functools
import math

import jax
import jax.numpy as jnp
from jax import lax
from jax.experimental import pallas as pl
from jax.experimental.pallas import tpu as pltpu

F32 = jnp.float32
BF16 = jnp.bfloat16

D_MODEL = 1024
HEAD_DIM = 64
N_SB_HEADS = 8
N_MOBA_HEADS = 8
ATT_W = (N_SB_HEADS + N_MOBA_HEADS) * HEAD_DIM
MOBA_BLOCK = 256
MOBA_TOPK = 3
N_RET_HEADS = 4
RET_DK = D_MODEL // N_RET_HEADS
RET_DV = 2 * RET_DK
RET_QK_W = N_RET_HEADS * RET_DK
RET_V_W = N_RET_HEADS * RET_DV
D_FF = -(-8 * D_MODEL // 768) * 256
ROPE_THETA = 10000.0
NORM_EPS = 1e-6
GN_EPS = 1e-5
NEG = -1e30
Q_SCALE = HEAD_DIM ** -0.5 * math.log2(math.e)

LANES = 128
ROW_TILE = 512
COL_CHUNK = 512
ATT_TILE = 256
PAIRS_PER_STEP = 2
ATT_FLAGS = None
RET_CHUNK = 256
RET_SEQ_TILE = 1024
RET_HEADS_PER_STEP = 2
VMEM_LIMIT = 56 * 1024 * 1024


def _params(n_axes, flags=None):
    return pltpu.CompilerParams(
        dimension_semantics=("arbitrary",) * n_axes, vmem_limit_bytes=VMEM_LIMIT, flags=flags)


def _resident(shape, index_map):
    return pl.BlockSpec(shape, index_map, pipeline_mode=pl.Buffered(1))


def _silu(v):
    return v * (1.0 / (1.0 + jnp.exp(-v)))


def _ada_kernel(c_ref, w_ref, b_ref, o_ref):
    o_ref[0] = jnp.dot(_silu(c_ref[...]), w_ref[0], preferred_element_type=F32) + b_ref[0]


def _ada_mod(c, ada_w, ada_b):
    depth, d, n = ada_w.shape
    cp = jnp.pad(c, ((0, 8 - c.shape[0]), (0, 0)))
    tn = 1536
    return pl.pallas_call(
        _ada_kernel,
        grid=(depth, n // tn),
        in_specs=[
            pl.BlockSpec((8, d), lambda l, j: (0, 0)),
            pl.BlockSpec((1, d, tn), lambda l, j: (l, 0, j)),
            pl.BlockSpec((1, 1, tn), lambda l, j: (l, 0, j)),
        ],
        out_specs=pl.BlockSpec((1, 8, tn), lambda l, j: (l, 0, j)),
        out_shape=jax.ShapeDtypeStruct((depth, 8, n), F32),
        compiler_params=_params(2),
        name="ada_mod",
    )(cp, ada_w, ada_b.reshape(depth, 1, n))


def _norm_mod(x, g, shift, scale):
    y = x * lax.rsqrt(jnp.mean(x * x, axis=-1, keepdims=True) + NORM_EPS)
    return (y * g) * (1.0 + scale) + shift


def _att_proj_kernel(x_ref, g_ref, sh_ref, sc_ref, cos_ref, sina_ref, sinb_ref, w_ref, o_ref):
    h = _norm_mod(x_ref[...], g_ref[...], sh_ref[0], sc_ref[0]).astype(BF16)
    cos, sina, sinb = cos_ref[...], sina_ref[...], sinb_ref[...]
    sb_w = N_SB_HEADS * HEAD_DIM
    for c in range(3 * ATT_W // COL_CHUNK):
        lo = c * COL_CHUNK
        y = jnp.dot(h, w_ref[:, lo:lo + COL_CHUNK], preferred_element_type=F32)
        is_q = lo < ATT_W
        rotated = (lo % ATT_W) >= sb_w and lo < 2 * ATT_W
        for s in range(COL_CHUNK // LANES):
            ys = y[:, s * LANES:(s + 1) * LANES]
            if rotated:
                ys = (ys * cos + pltpu.roll(ys, LANES - HEAD_DIM // 2, 1) * sina
                      + pltpu.roll(ys, HEAD_DIM // 2, 1) * sinb)
            if is_q:
                ys = ys * Q_SCALE
            o_ref[:, lo + s * LANES:lo + (s + 1) * LANES] = ys.astype(BF16)


def _ret_proj_kernel(x_ref, g_ref, sh_ref, sc_ref, cos_ref, sin_ref, w_ref, o_ref):
    h = _norm_mod(x_ref[...], g_ref[...], sh_ref[0], sc_ref[0]).astype(BF16)
    cos, sin = cos_ref[...], sin_ref[...]
    n = 2 * RET_QK_W + 2 * RET_V_W
    for c in range(n // COL_CHUNK):
        lo = c * COL_CHUNK
        y = jnp.dot(h, w_ref[:, lo:lo + COL_CHUNK], preferred_element_type=F32)
        if lo < 2 * RET_QK_W:
            mul = 1.0 if lo < RET_QK_W else RET_DK ** -0.5
            half = RET_DK // 2
            for hd in range(COL_CHUNK // RET_DK):
                x1 = y[:, hd * RET_DK:hd * RET_DK + half]
                x2 = y[:, hd * RET_DK + half:(hd + 1) * RET_DK]
                o1 = (x1 * cos - x2 * sin) * mul
                o2 = (x2 * cos + x1 * sin) * mul
                o_ref[:, lo + hd * RET_DK:lo + hd * RET_DK + half] = o1.astype(BF16)
                o_ref[:, lo + hd * RET_DK + half:lo + (hd + 1) * RET_DK] = o2.astype(BF16)
        else:
            o_ref[:, lo:lo + COL_CHUNK] = y.astype(BF16)


def _proj(kernel, x, gain, shift, scale, tables, w, seq, name):
    t, d = x.shape
    n = w.shape[1]
    tm = ROW_TILE
    per_seq = seq // tm
    row = lambda i: (i, 0)
    batch = lambda i: (i // per_seq, 0, 0)
    pos = lambda i: (i % per_seq, 0)
    return pl.pallas_call(
        kernel,
        grid=(t // tm,),
        in_specs=[
            pl.BlockSpec((tm, d), row),
            pl.BlockSpec((1, d), lambda i: (0, 0)),
            pl.BlockSpec((1, 1, d), batch),
            pl.BlockSpec((1, 1, d), batch),
            *[pl.BlockSpec((tm, LANES), pos) for _ in tables],
            _resident((d, n), lambda i: (0, 0)),
        ],
        out_specs=pl.BlockSpec((tm, n), row),
        out_shape=jax.ShapeDtypeStruct((t, n), BF16),
        compiler_params=_params(1),
        name=name,
    )(x, gain, shift, scale, *tables, w)


def _pair_lanes(p):
    return slice(p * LANES, (p + 1) * LANES)


def _fill_vt(v_ref, vt_ref):
    t = ATT_TILE
    for p in range(v_ref.shape[1] // LANES):
        for n in range(v_ref.shape[0] // t):
            vt_ref[p, n] = v_ref[n * t:(n + 1) * t, _pair_lanes(p)].T


def _tile_masks(strict):
    t = ATT_TILE
    key = lax.broadcasted_iota(jnp.int32, (t, 2 * t), 0)
    qry = lax.broadcasted_iota(jnp.int32, (t, 2 * t), 1) % t
    chan0 = lax.broadcasted_iota(jnp.int32, (LANES, 1), 0) < HEAD_DIM
    return (key < qry) if strict else (key <= qry), chan0


def _load_super_tile(q_ref, p, i, chan0):
    t = ATT_TILE
    base = pl.multiple_of(i * 2 * t, 2 * t)
    cells = []
    for c in range(2):
        q_t = q_ref[pl.ds(base + c * t, t), _pair_lanes(p)].T
        zero = jnp.zeros_like(q_t)
        cells += [jnp.where(chan0, q_t, zero), jnp.where(chan0, zero, q_t)]
    return jnp.concatenate(cells, axis=1), jnp.concatenate(cells[2:], axis=1)


def _store_super_tile(o_ref, p, i, acc_t, chan0):
    t = ATT_TILE
    base = pl.multiple_of(i * 2 * t, 2 * t)
    for cell in range(2):
        lo = cell * 2 * t
        o_t = jnp.where(chan0, acc_t[:, lo:lo + t], acc_t[:, lo + t:lo + 2 * t])
        o_ref[pl.ds(base + cell * t, t), _pair_lanes(p)] = o_t.T.astype(BF16)


def _scores(k_ref, p, j, q_cols):
    t = ATT_TILE
    return jnp.dot(k_ref[pl.ds(pl.multiple_of(j * t, t), t), _pair_lanes(p)], q_cols,
                   preferred_element_type=F32)


def _sb_kernel(q_ref, k_ref, v_ref, o_ref, vt_ref, acc_ref, z_ref):
    t = ATT_TILE
    w = 2 * t
    seq = q_ref.shape[0]
    pairs = range(q_ref.shape[1] // LANES)
    per_half = 2 * len(pairs)
    strict, chan0 = _tile_masks(True)
    from_key = -(lax.broadcasted_iota(jnp.int32, (t, t), 1)
                 >= lax.broadcasted_iota(jnp.int32, (t, t), 0)).astype(BF16)
    _fill_vt(v_ref, vt_ref)

    def block(p, j, slot, lanes, run, mask):
        z = z_ref[slot, :, lanes]
        neg_l = jnp.maximum(z, 0.0) + jnp.log2(1.0 + jnp.exp2(-jnp.abs(z)))
        if mask is not None:
            neg_l = jnp.where(mask, neg_l, 0.0)
        since = jnp.dot(from_key, neg_l.astype(BF16), preferred_element_type=F32)
        a = jnp.exp2(z_ref[slot, :, lanes] + since)
        if mask is not None:
            a = jnp.where(mask, a, 0.0)
        part = jnp.dot(vt_ref[p, j], a.astype(BF16), preferred_element_type=F32)
        return run + since[0:1, :], part * jnp.exp2(run)

    def stage_pair_of_blocks(half, j, tiles):
        for p in pairs:
            for b in range(2):
                z_ref[half + 2 * p + b] = _scores(k_ref, p, j + b, tiles[p][0])

    def q_super(i, _):
        tiles = [_load_super_tile(q_ref, p, i, chan0) for p in pairs]
        mask = jnp.concatenate([strict, jnp.ones((t, w), jnp.bool_)], axis=1)
        cell1 = slice(w, 2 * w)
        diag = lax.rem(i, 2) * per_half
        for p in pairs:
            q2, qb = tiles[p]
            z_ref[diag + 2 * p + 1, :, cell1] = _scores(k_ref, p, 2 * i + 1, qb)
            z_ref[diag + 2 * p] = _scores(k_ref, p, 2 * i, q2)
        runs = []
        for p in pairs:
            run_b, part = block(p, 2 * i + 1, diag + 2 * p + 1, cell1, jnp.zeros((1, w), F32),
                                strict)
            acc_ref[p, :, w:] = part
            run = jnp.concatenate([jnp.zeros((1, w), F32), run_b], axis=1)
            run, part = block(p, 2 * i, diag + 2 * p, slice(None), run, mask)
            acc_ref[p, :, :w] = part[:, :w]
            acc_ref[p, :, w:] += part[:, w:]
            runs.append(run)

        def past(s, runs):
            j = 2 * (i - 1 - s)
            half = lax.rem(s, 2) * per_half
            stage_pair_of_blocks(half, j, tiles)
            out = []
            for p in pairs:
                run, part_b = block(p, j + 1, half + 2 * p + 1, slice(None), runs[p], None)
                run, part_a = block(p, j, half + 2 * p, slice(None), run, None)
                acc_ref[p] += part_b + part_a
                out.append(run)
            return tuple(out)

        lax.fori_loop(0, i, past, tuple(runs))
        for p in pairs:
            _store_super_tile(o_ref, p, i, acc_ref[p], chan0)
        return 0

    lax.fori_loop(0, seq // w, q_super, 0)


def _pair_specs(seq, q_col, k_col, v_col):
    width = PAIRS_PER_STEP * LANES
    return [
        pl.BlockSpec((seq, width), lambda b, p: (b, q_col + p)),
        pl.BlockSpec((seq, width), lambda b, p: (b, k_col + p)),
        pl.BlockSpec((seq, width), lambda b, p: (b, v_col + p)),
    ]


def _attention_call(kernel, qkv, batch, seq, n_heads, first_head, extra_scratch, name):
    width = PAIRS_PER_STEP * LANES
    groups = n_heads * HEAD_DIM // width
    third = ATT_W // width
    first = first_head * HEAD_DIM // width
    return pl.pallas_call(
        kernel,
        grid=(batch, groups),
        in_specs=_pair_specs(seq, first, third + first, 2 * third + first),
        out_specs=pl.BlockSpec((seq, width), lambda b, p: (b, p)),
        out_shape=jax.ShapeDtypeStruct((batch * seq, groups * width), BF16),
        scratch_shapes=[pltpu.VMEM((PAIRS_PER_STEP, seq // ATT_TILE, LANES, ATT_TILE), BF16),
                        pltpu.VMEM((PAIRS_PER_STEP, LANES, 4 * ATT_TILE), F32),
                        *extra_scratch],
        compiler_params=_params(2, ATT_FLAGS),
        name=name,
    )(qkv, qkv, qkv)


def _sb_attention(qkv, batch, seq):
    staged = pltpu.VMEM((4 * PAIRS_PER_STEP, ATT_TILE, 4 * ATT_TILE), F32)
    return _attention_call(_sb_kernel, qkv, batch, seq, N_SB_HEADS, 0, [staged], "sb_attention")


def _moba_kernel(q_ref, k_ref, v_ref, o_ref, vt_ref, acc_ref, sel_ref, s_ref):
    t = ATT_TILE
    w = 2 * t
    seq = q_ref.shape[0]
    nb = seq // MOBA_BLOCK
    pairs = range(q_ref.shape[1] // LANES)
    per_half = 2 * len(pairs)
    causal, chan0 = _tile_masks(False)
    blk = lax.broadcasted_iota(jnp.int32, (nb, 2 * w), 0)
    cell = lax.broadcasted_iota(jnp.int32, (nb, 2 * w), 1) // w
    blk_f = blk.astype(F32)
    kmean = [(jnp.sum(k_ref[:, _pair_lanes(p)].astype(F32).reshape(nb, MOBA_BLOCK, LANES), axis=1)
              * (1.0 / MOBA_BLOCK)).astype(BF16) for p in pairs]
    _fill_vt(v_ref, vt_ref)


    def select(p, i, q2):
        gate = jnp.dot(kmean[p], q2, preferred_element_type=F32)
        past_blk = blk < 2 * i + cell
        gate = jnp.where(past_blk, gate, NEG)
        sel = jnp.zeros((nb, 2 * w), F32)
        for _r in range(MOBA_TOPK):
            top = jnp.max(gate, axis=0, keepdims=True)
            idx = jnp.min(jnp.where(gate == top, blk_f, float(nb)), axis=0, keepdims=True)
            hit = blk_f == idx
            sel = jnp.where(hit & past_blk, 1.0, sel)
            gate = jnp.where(hit, -jnp.inf, gate)
        sel_ref[p] = sel

    def own_blocks(p, i, raw_a, raw_b):
        s = jnp.where(causal, raw_b, NEG)
        m_b = jnp.max(s, axis=0, keepdims=True)
        pr = jnp.exp2(s - m_b)
        acc_ref[p, :, :w] = jnp.zeros((LANES, w), F32)
        acc_ref[p, :, w:] = jnp.dot(vt_ref[p, 2 * i + 1], pr.astype(BF16),
                                    preferred_element_type=F32)
        m = jnp.concatenate([jnp.full((1, w), -jnp.inf, F32), m_b], axis=1)
        l = jnp.concatenate([jnp.zeros((1, w), F32), jnp.sum(pr, axis=0, keepdims=True)], axis=1)
        keep = jnp.concatenate(
            [causal, jnp.broadcast_to(sel_ref[p, pl.ds(2 * i, 1), w:] > 0.0, (t, w))], axis=1)
        s = jnp.where(keep, raw_a, NEG)
        m_new = jnp.maximum(m, jnp.max(s, axis=0, keepdims=True))
        alpha = jnp.exp2(m - m_new)
        pr = jnp.exp2(s - m_new)
        acc_ref[p] = alpha * acc_ref[p] + jnp.dot(
            vt_ref[p, 2 * i], pr.astype(BF16), preferred_element_type=F32)
        return m_new, alpha * l + jnp.sum(pr, axis=0, keepdims=True)

    def past_pair(p, j, raw_a, raw_b, m_old, l_old):
        s_a = jnp.where(sel_ref[p, pl.ds(j, 1), :] > 0.0, raw_a, NEG)
        s_b = jnp.where(sel_ref[p, pl.ds(j + 1, 1), :] > 0.0, raw_b, NEG)
        m_new = jnp.maximum(m_old, jnp.maximum(jnp.max(s_a, axis=0, keepdims=True),
                                               jnp.max(s_b, axis=0, keepdims=True)))
        alpha = jnp.exp2(m_old - m_new)
        p_a = jnp.exp2(s_a - m_new)
        p_b = jnp.exp2(s_b - m_new)
        part = (jnp.dot(vt_ref[p, j], p_a.astype(BF16), preferred_element_type=F32)
                + jnp.dot(vt_ref[p, j + 1], p_b.astype(BF16), preferred_element_type=F32))
        acc_ref[p] = alpha * acc_ref[p] + part
        l_new = (alpha * l_old + jnp.sum(p_a, axis=0, keepdims=True)
                 + jnp.sum(p_b, axis=0, keepdims=True))
        return m_new, l_new

    def stage_pair_of_blocks(half, j, tiles):
        for p in pairs:
            for b in range(2):
                s_ref[half + 2 * p + b] = _scores(k_ref, p, j + b, tiles[p][0])

    def q_super(i, _):
        tiles = [_load_super_tile(q_ref, p, i, chan0) for p in pairs]
        own = lax.rem(i, 2) * per_half
        for p in pairs:
            q2, qb = tiles[p]
            s_ref[own + 2 * p + 1, :, w:] = _scores(k_ref, p, 2 * i + 1, qb)
            s_ref[own + 2 * p] = _scores(k_ref, p, 2 * i, q2)
            select(p, i, q2)
        stats = tuple(own_blocks(p, i, s_ref[own + 2 * p], s_ref[own + 2 * p + 1, :, w:])
                      for p in pairs)

        def past(jj, stats):
            half = lax.rem(jj, 2) * per_half
            stage_pair_of_blocks(half, 2 * jj, tiles)
            return tuple(past_pair(p, 2 * jj, s_ref[half + 2 * p], s_ref[half + 2 * p + 1],
                                   *stats[p]) for p in pairs)

        stats = lax.fori_loop(0, i, past, stats)
        for p in pairs:
            _store_super_tile(o_ref, p, i, acc_ref[p] / stats[p][1], chan0)
        return 0

    lax.fori_loop(0, seq // w, q_super, 0)


def _moba_attention(qkv, batch, seq):
    sel = pltpu.VMEM((PAIRS_PER_STEP, seq // MOBA_BLOCK, 4 * ATT_TILE), F32)
    staged = pltpu.VMEM((4 * PAIRS_PER_STEP, ATT_TILE, 4 * ATT_TILE), F32)
    return _attention_call(_moba_kernel, qkv, batch, seq, N_MOBA_HEADS, N_SB_HEADS,
                           [sel, staged], "moba_attention")


def _ret_kernel(q_ref, k_ref, v_ref, g_ref, gn_ref, o_ref, state_ref, decay_ref):
    c = RET_CHUNK
    heads = range(RET_HEADS_PER_STEP)
    n_row = lax.broadcasted_iota(jnp.int32, (c, 1), 0).astype(F32)
    log_g, xi, zeta, chunk_decay = [], [], [], []
    for hh in heads:
        head = (pl.program_id(1) * RET_HEADS_PER_STEP + hh + 5).astype(F32)
        lg = jnp.log(1.0 - jnp.exp2(-jnp.full((1, 1), head, F32)))
        log_g.append(lg)
        xi.append(jnp.exp(lg * (n_row + 1.0)))
        zeta.append(jnp.exp(lg * (c - 1.0 - n_row)))
        chunk_decay.append(jnp.exp(lg * c))

    @pl.when(pl.program_id(2) == 0)
    def _():
        diff = (lax.broadcasted_iota(jnp.int32, (c, c), 0)
                - lax.broadcasted_iota(jnp.int32, (c, c), 1)).astype(F32)
        for hh in heads:
            decay_ref[hh] = jnp.where(diff >= 0, jnp.exp(log_g[hh] * jnp.maximum(diff, 0.0)), 0.0)
        state_ref[...] = jnp.zeros_like(state_ref)

    def chunk(i, _):
        rows = pl.ds(pl.multiple_of(i * c, c), c)
        for hh in heads:
            qk_cols = slice(hh * RET_DK, (hh + 1) * RET_DK)
            v_cols = slice(hh * RET_DV, (hh + 1) * RET_DV)
            q, k, v = q_ref[rows, qk_cols], k_ref[rows, qk_cols], v_ref[rows, v_cols]
            inner = lax.dot_general(q, k, (((1,), (1,)), ((), ())),
                                    preferred_element_type=F32) * decay_ref[hh]
            state = state_ref[hh]
            o = (jnp.dot(inner.astype(BF16), v, preferred_element_type=F32)
                 + jnp.dot(q, state.astype(BF16), preferred_element_type=F32) * xi[hh])
            kz = (k.astype(F32) * zeta[hh]).astype(BF16)
            state_ref[hh] = state * chunk_decay[hh] + lax.dot_general(
                kz, v, (((0,), (0,)), ((), ())), preferred_element_type=F32)
            mu = jnp.mean(o, axis=-1, keepdims=True)
            var = jnp.mean(jnp.square(o - mu), axis=-1, keepdims=True)
            on = (o - mu) * lax.rsqrt(var + GN_EPS) * gn_ref[:, v_cols]
            o_ref[rows, v_cols] = (_silu(g_ref[rows, v_cols].astype(F32)) * on).astype(BF16)
        return 0

    lax.fori_loop(0, q_ref.shape[0] // c, chunk, 0)


def _retention(proj, gn_gain, batch, seq):
    hps = RET_HEADS_PER_STEP
    tile = RET_SEQ_TILE
    tiles = seq // tile
    qk_w, v_w = hps * RET_DK, hps * RET_DV
    k0 = RET_QK_W // qk_w
    v0 = 2 * RET_QK_W // v_w
    g0 = v0 + RET_V_W // v_w
    row = lambda b, h, s: b * tiles + s
    return pl.pallas_call(
        _ret_kernel,
        grid=(batch, N_RET_HEADS // hps, tiles),
        in_specs=[
            pl.BlockSpec((tile, qk_w), lambda b, h, s: (row(b, h, s), h)),
            pl.BlockSpec((tile, qk_w), lambda b, h, s: (row(b, h, s), k0 + h)),
            pl.BlockSpec((tile, v_w), lambda b, h, s: (row(b, h, s), v0 + h)),
            pl.BlockSpec((tile, v_w), lambda b, h, s: (row(b, h, s), g0 + h)),
            pl.BlockSpec((1, v_w), lambda b, h, s: (0, h)),
        ],
        out_specs=pl.BlockSpec((tile, v_w), lambda b, h, s: (row(b, h, s), h)),
        out_shape=jax.ShapeDtypeStruct((batch * seq, RET_V_W), BF16),
        scratch_shapes=[pltpu.VMEM((hps, RET_DK, RET_DV), F32),
                        pltpu.VMEM((hps, RET_CHUNK, RET_CHUNK), F32)],
        compiler_params=_params(3),
        name="retention",
    )(proj, proj, proj, proj, gn_gain)


def _out_proj_kernel(*refs, n_in):
    x_ref, gate_ref = refs[0], refs[1]
    a_refs, w_refs, o_ref = refs[2:2 + n_in], refs[2 + n_in:2 + 2 * n_in], refs[2 + 2 * n_in]
    y = jnp.dot(a_refs[0][...], w_refs[0][...], preferred_element_type=F32)
    for a_ref, w_ref in zip(a_refs[1:], w_refs[1:]):
        y += jnp.dot(a_ref[...], w_ref[...], preferred_element_type=F32)
    o_ref[...] = x_ref[...] + gate_ref[0] * y


def _out_proj(x, gate, acts, weights, seq, name):
    t, d = x.shape
    tm = ROW_TILE
    per_seq = seq // tm
    row = lambda i: (i, 0)
    return pl.pallas_call(
        functools.partial(_out_proj_kernel, n_in=len(acts)),
        grid=(t // tm,),
        in_specs=[
            pl.BlockSpec((tm, d), row),
            pl.BlockSpec((1, 1, d), lambda i: (i // per_seq, 0, 0)),
            *[pl.BlockSpec((tm, a.shape[1]), row) for a in acts],
            *[_resident(w.shape, lambda i: (0, 0)) for w in weights],
        ],
        out_specs=pl.BlockSpec((tm, d), row),
        out_shape=jax.ShapeDtypeStruct((t, d), F32),
        compiler_params=_params(1),
        name=name,
    )(x, gate, *acts, *weights)


def _ffn_kernel(x_ref, g_ref, sh_ref, sc_ref, gate_ref, wgu_ref, wd_ref, fin_ref, o_ref,
                *, final_norm):
    x = x_ref[...]
    h = _norm_mod(x, g_ref[...], sh_ref[0], sc_ref[0]).astype(BF16)
    half = D_FF // 2
    y = None
    for c in range(2):
        cols = slice(c * half, (c + 1) * half)
        gt = jnp.dot(h, wgu_ref[:, cols], preferred_element_type=F32)
        up = jnp.dot(h, wgu_ref[:, D_FF + c * half:D_FF + (c + 1) * half],
                     preferred_element_type=F32)
        part = jnp.dot((_silu(gt) * up).astype(BF16), wd_ref[cols, :], preferred_element_type=F32)
        y = part if y is None else y + part
    out = x + gate_ref[0] * y
    if final_norm:
        out = out * lax.rsqrt(jnp.mean(out * out, axis=-1, keepdims=True) + NORM_EPS) * fin_ref[...]
    o_ref[...] = out


def _ffn(x, gain, shift, scale, gate, w_gate_up, w_down, layer, fin_gain, seq, final_norm, name):
    t, d = x.shape
    tm = ROW_TILE
    per_seq = seq // tm
    row = lambda i: (i, 0)
    batch = lambda i: (i // per_seq, 0, 0)
    const = lambda i: (0, 0)
    this_layer = lambda i: (layer, 0, 0)
    return pl.pallas_call(
        functools.partial(_ffn_kernel, final_norm=final_norm),
        grid=(t // tm,),
        in_specs=[
            pl.BlockSpec((tm, d), row),
            pl.BlockSpec((1, d), const),
            pl.BlockSpec((1, 1, d), batch),
            pl.BlockSpec((1, 1, d), batch),
            pl.BlockSpec((1, 1, d), batch),
            _resident((None,) + w_gate_up.shape[1:], this_layer),
            _resident((None,) + w_down.shape[1:], this_layer),
            pl.BlockSpec((1, d), const),
        ],
        out_specs=pl.BlockSpec((tm, d), row),
        out_shape=jax.ShapeDtypeStruct((t, d), F32),
        compiler_params=_params(1),
        name=name,
    )(x, gain, shift, scale, gate, w_gate_up, w_down, fin_gain)


def _rope_tables(seq, dim, tile):
    inv = ROPE_THETA ** (-jnp.arange(0, dim, 2, dtype=F32) / dim)
    ang = jnp.arange(seq, dtype=jnp.int32).astype(F32)[:, None] * inv[None, :]
    return jnp.tile(jnp.cos(ang), (1, tile)), jnp.tile(jnp.sin(ang), (1, tile))


def kernel(x, c, ada_w, ada_b, norm_gains, att_w_qkv, att_w_o, ret_w_in, ret_gn, ret_w_o,
           ffn_w_gate_up, ffn_w_down, final_norm):
    batch, seq, d = x.shape
    depth = ada_w.shape[0]
    assert d == D_MODEL and batch <= 8 and seq % ROW_TILE == 0
    assert seq % (2 * ATT_TILE) == 0 and seq % RET_SEQ_TILE == 0

    mod = _ada_mod(c, ada_w, ada_b)[:, :batch].reshape(depth, batch, 1, 6, d)
    xt = x.reshape(batch * seq, d)

    half = HEAD_DIM // 2
    cos_a, sin_a = _rope_tables(seq, HEAD_DIM, LANES // half)
    first_half = (jnp.arange(LANES) % HEAD_DIM) < half
    att_tables = (cos_a, jnp.where(first_half, -sin_a, 0.0), jnp.where(first_half, 0.0, sin_a))
    ret_tables = _rope_tables(seq, RET_DK, 1)
    w_gate_up, w_down = ffn_w_gate_up.astype(BF16), ffn_w_down.astype(BF16)

    for layer in range(depth):
        shift_m, scale_m, gate_m, shift_f, scale_f, gate_f = (mod[layer, :, :, i] for i in range(6))
        gains = norm_gains[layer]
        j = layer // 2
        if layer % 2 == 0:
            qkv = _proj(_att_proj_kernel, xt, gains[0:1], shift_m, scale_m, att_tables,
                        att_w_qkv[j].astype(BF16), seq, "att_proj")
            sb = _sb_attention(qkv, batch, seq)
            mb = _moba_attention(qkv, batch, seq)
            w_o = att_w_o[j].astype(BF16)
            split = N_SB_HEADS * HEAD_DIM
            xt = _out_proj(xt, gate_m, (sb, mb), (w_o[:split], w_o[split:]), seq, "att_out")
        else:
            proj = _proj(_ret_proj_kernel, xt, gains[0:1], shift_m, scale_m, ret_tables,
                         ret_w_in[j].astype(BF16), seq, "ret_proj")
            ro = _retention(proj, ret_gn[j].reshape(1, RET_V_W), batch, seq)
            xt = _out_proj(xt, gate_m, (ro,), (ret_w_o[j].astype(BF16),), seq, "ret_out")
        xt = _ffn(xt, gains[1:2], shift_f, scale_f, gate_f, w_gate_up, w_down, layer,
                  final_norm.reshape(1, d), seq, layer == depth - 1, "ffn%d" % layer)
    return xt.reshape(batch, seq, d)
```

```python
import functools
import math

import jax
import jax.numpy as jnp
from jax import lax
from jax.experimental import pallas as pl
from jax.experimental.pallas import tpu as pltpu

F32 = jnp.float32
BF16 = jnp.bfloat16

D_MODEL = 1024
HEAD_DIM = 64
N_SB_HEADS = 8
N_MOBA_HEADS = 8
ATT_W = (N_SB_HEADS + N_MOBA_HEADS) * HEAD_DIM
MOBA_BLOCK = 256
MOBA_TOPK = 3
N_RET_HEADS = 4
RET_DK = D_MODEL // N_RET_HEADS
RET_DV = 2 * RET_DK
RET_QK_W = N_RET_HEADS * RET_DK
RET_V_W = N_RET_HEADS * RET_DV
D_FF = -(-8 * D_MODEL // 768) * 256
ROPE_THETA = 10000.0
NORM_EPS = 1e-6
GN_EPS = 1e-5
NEG = -1e30
Q_SCALE = HEAD_DIM ** -0.5 * math.log2(math.e)

LANES = 128
ROW_TILE = 512
COL_CHUNK = 512
ATT_TILE = 256
PAIRS_PER_STEP = 2
RET_CHUNK = 256
RET_SEQ_TILE = 1024
RET_HEADS_PER_STEP = 2
VMEM_LIMIT = 56 * 1024 * 1024


def _params(n_axes):
    return pltpu.CompilerParams(
        dimension_semantics=("arbitrary",) * n_axes, vmem_limit_bytes=VMEM_LIMIT)


def _resident(shape, index_map):
    return pl.BlockSpec(shape, index_map, pipeline_mode=pl.Buffered(1))


def _silu(v):
    return v * (1.0 / (1.0 + jnp.exp(-v)))


def _ada_kernel(c_ref, w_ref, b_ref, o_ref):
    o_ref[0] = jnp.dot(_silu(c_ref[...]), w_ref[0], preferred_element_type=F32) + b_ref[0]


def _ada_mod(c, ada_w, ada_b):
    depth, d, n = ada_w.shape
    cp = jnp.pad(c, ((0, 8 - c.shape[0]), (0, 0)))
    tn = 1536
    return pl.pallas_call(
        _ada_kernel,
        grid=(depth, n // tn),
        in_specs=[
            pl.BlockSpec((8, d), lambda l, j: (0, 0)),
            pl.BlockSpec((1, d, tn), lambda l, j: (l, 0, j)),
            pl.BlockSpec((1, 1, tn), lambda l, j: (l, 0, j)),
        ],
        out_specs=pl.BlockSpec((1, 8, tn), lambda l, j: (l, 0, j)),
        out_shape=jax.ShapeDtypeStruct((depth, 8, n), F32),
        compiler_params=_params(2),
        name="ada_mod",
    )(cp, ada_w, ada_b.reshape(depth, 1, n))


def _norm_mod(x, g, shift, scale):
    y = x * lax.rsqrt(jnp.mean(x * x, axis=-1, keepdims=True) + NORM_EPS)
    return (y * g) * (1.0 + scale) + shift


def _att_proj_kernel(x_ref, g_ref, sh_ref, sc_ref, cos_ref, sina_ref, sinb_ref, w_ref, o_ref):
    h = _norm_mod(x_ref[...], g_ref[...], sh_ref[0], sc_ref[0]).astype(BF16)
    cos, sina, sinb = cos_ref[...], sina_ref[...], sinb_ref[...]
    sb_w = N_SB_HEADS * HEAD_DIM
    for c in range(3 * ATT_W // COL_CHUNK):
        lo = c * COL_CHUNK
        y = jnp.dot(h, w_ref[:, lo:lo + COL_CHUNK], preferred_element_type=F32)
        is_q = lo < ATT_W
        rotated = (lo % ATT_W) >= sb_w and lo < 2 * ATT_W
        for s in range(COL_CHUNK // LANES):
            ys = y[:, s * LANES:(s + 1) * LANES]
            if rotated:
                ys = (ys * cos + pltpu.roll(ys, LANES - HEAD_DIM // 2, 1) * sina
                      + pltpu.roll(ys, HEAD_DIM // 2, 1) * sinb)
            if is_q:
                ys = ys * Q_SCALE
            o_ref[:, lo + s * LANES:lo + (s + 1) * LANES] = ys.astype(BF16)


def _ret_proj_kernel(x_ref, g_ref, sh_ref, sc_ref, cos_ref, sin_ref, w_ref, o_ref):
    h = _norm_mod(x_ref[...], g_ref[...], sh_ref[0], sc_ref[0]).astype(BF16)
    cos, sin = cos_ref[...], sin_ref[...]
    n = 2 * RET_QK_W + 2 * RET_V_W
    for c in range(n // COL_CHUNK):
        lo = c * COL_CHUNK
        y = jnp.dot(h, w_ref[:, lo:lo + COL_CHUNK], preferred_element_type=F32)
        if lo < 2 * RET_QK_W:
            mul = 1.0 if lo < RET_QK_W else RET_DK ** -0.5
            half = RET_DK // 2
            for hd in range(COL_CHUNK // RET_DK):
                x1 = y[:, hd * RET_DK:hd * RET_DK + half]
                x2 = y[:, hd * RET_DK + half:(hd + 1) * RET_DK]
                o1 = (x1 * cos - x2 * sin) * mul
                o2 = (x2 * cos + x1 * sin) * mul
                o_ref[:, lo + hd * RET_DK:lo + hd * RET_DK + half] = o1.astype(BF16)
                o_ref[:, lo + hd * RET_DK + half:lo + (hd + 1) * RET_DK] = o2.astype(BF16)
        else:
            o_ref[:, lo:lo + COL_CHUNK] = y.astype(BF16)


def _proj(kernel, x, gain, shift, scale, tables, w, seq, name):
    t, d = x.shape
    n = w.shape[1]
    tm = ROW_TILE
    per_seq = seq // tm
    row = lambda i: (i, 0)
    batch = lambda i: (i // per_seq, 0, 0)
    pos = lambda i: (i % per_seq, 0)
    return pl.pallas_call(
        kernel,
        grid=(t // tm,),
        in_specs=[
            pl.BlockSpec((tm, d), row),
            pl.BlockSpec((1, d), lambda i: (0, 0)),
            pl.BlockSpec((1, 1, d), batch),
            pl.BlockSpec((1, 1, d), batch),
            *[pl.BlockSpec((tm, LANES), pos) for _ in tables],
            _resident((d, n), lambda i: (0, 0)),
        ],
        out_specs=pl.BlockSpec((tm, n), row),
        out_shape=jax.ShapeDtypeStruct((t, n), BF16),
        compiler_params=_params(1),
        name=name,
    )(x, gain, shift, scale, *tables, w)


def _pair_lanes(p):
    return slice(p * LANES, (p + 1) * LANES)


def _fill_vt(v_ref, vt_ref):
    t = ATT_TILE
    for p in range(v_ref.shape[1] // LANES):
        for n in range(v_ref.shape[0] // t):
            vt_ref[p, n] = v_ref[n * t:(n + 1) * t, _pair_lanes(p)].T


def _tile_masks(strict):
    t = ATT_TILE
    key = lax.broadcasted_iota(jnp.int32, (t, 2 * t), 0)
    qry = lax.broadcasted_iota(jnp.int32, (t, 2 * t), 1) % t
    chan0 = lax.broadcasted_iota(jnp.int32, (LANES, 1), 0) < HEAD_DIM
    return (key < qry) if strict else (key <= qry), chan0


def _load_super_tile(q_ref, p, i, chan0):
    t = ATT_TILE
    base = pl.multiple_of(i * 2 * t, 2 * t)
    cells = []
    for c in range(2):
        q_t = q_ref[pl.ds(base + c * t, t), _pair_lanes(p)].T
        zero = jnp.zeros_like(q_t)
        cells += [jnp.where(chan0, q_t, zero), jnp.where(chan0, zero, q_t)]
    return jnp.concatenate(cells, axis=1), jnp.concatenate(cells[2:], axis=1)


def _store_super_tile(o_ref, p, i, acc_t, chan0):
    t = ATT_TILE
    base = pl.multiple_of(i * 2 * t, 2 * t)
    for cell in range(2):
        lo = cell * 2 * t
        o_t = jnp.where(chan0, acc_t[:, lo:lo + t], acc_t[:, lo + t:lo + 2 * t])
        o_ref[pl.ds(base + cell * t, t), _pair_lanes(p)] = o_t.T.astype(BF16)


def _scores(k_ref, p, j, q_cols):
    t = ATT_TILE
    return jnp.dot(k_ref[pl.ds(pl.multiple_of(j * t, t), t), _pair_lanes(p)], q_cols,
                   preferred_element_type=F32)


def _sb_kernel(q_ref, k_ref, v_ref, o_ref, vt_ref, acc_ref, z_ref):
    t = ATT_TILE
    w = 2 * t
    seq = q_ref.shape[0]
    pairs = range(q_ref.shape[1] // LANES)
    per_half = 2 * len(pairs)
    strict, chan0 = _tile_masks(True)
    from_key = -(lax.broadcasted_iota(jnp.int32, (t, t), 1)
                 >= lax.broadcasted_iota(jnp.int32, (t, t), 0)).astype(BF16)
    _fill_vt(v_ref, vt_ref)

    def block(p, j, slot, lanes, run, mask):
        z = z_ref[slot, :, lanes]
        neg_l = jnp.maximum(z, 0.0) + jnp.log2(1.0 + jnp.exp2(-jnp.abs(z)))
        if mask is not None:
            neg_l = jnp.where(mask, neg_l, 0.0)
        since = jnp.dot(from_key, neg_l.astype(BF16), preferred_element_type=F32)
        a = jnp.exp2(z_ref[slot, :, lanes] + since)
        if mask is not None:
            a = jnp.where(mask, a, 0.0)
        part = jnp.dot(vt_ref[p, j], a.astype(BF16), preferred_element_type=F32)
        return run + since[0:1, :], part * jnp.exp2(run)

    def stage_pair_of_blocks(half, j, tiles):
        for p in pairs:
            for b in range(2):
                z_ref[half + 2 * p + b] = _scores(k_ref, p, j + b, tiles[p][0])

    def q_super(i, _):
        tiles = [_load_super_tile(q_ref, p, i, chan0) for p in pairs]
        mask = jnp.concatenate([strict, jnp.ones((t, w), jnp.bool_)], axis=1)
        cell1 = slice(w, 2 * w)
        diag = lax.rem(i, 2) * per_half
        for p in pairs:
            q2, qb = tiles[p]
            z_ref[diag + 2 * p + 1, :, cell1] = _scores(k_ref, p, 2 * i + 1, qb)
            z_ref[diag + 2 * p] = _scores(k_ref, p, 2 * i, q2)
        runs = []
        for p in pairs:
            run_b, part = block(p, 2 * i + 1, diag + 2 * p + 1, cell1, jnp.zeros((1, w), F32),
                                strict)
            acc_ref[p, :, w:] = part
            run = jnp.concatenate([jnp.zeros((1, w), F32), run_b], axis=1)
            run, part = block(p, 2 * i, diag + 2 * p, slice(None), run, mask)
            acc_ref[p, :, :w] = part[:, :w]
            acc_ref[p, :, w:] += part[:, w:]
            runs.append(run)

        def past(s, runs):
            j = 2 * (i - 1 - s)
            half = lax.rem(s, 2) * per_half
            stage_pair_of_blocks(half, j, tiles)
            out = []
            for p in pairs:
                run, part_b = block(p, j + 1, half + 2 * p + 1, slice(None), runs[p], None)
                run, part_a = block(p, j, half + 2 * p, slice(None), run, None)
                acc_ref[p] += part_b + part_a
                out.append(run)
            return tuple(out)

        lax.fori_loop(0, i, past, tuple(runs))
        for p in pairs:
            _store_super_tile(o_ref, p, i, acc_ref[p], chan0)
        return 0

    lax.fori_loop(0, seq // w, q_super, 0)


def _pair_specs(seq, q_col, k_col, v_col):
    width = PAIRS_PER_STEP * LANES
    return [
        pl.BlockSpec((seq, width), lambda b, p: (b, q_col + p)),
        pl.BlockSpec((seq, width), lambda b, p: (b, k_col + p)),
        pl.BlockSpec((seq, width), lambda b, p: (b, v_col + p)),
    ]


def _attention_call(kernel, qkv, batch, seq, n_heads, first_head, extra_scratch, name):
    width = PAIRS_PER_STEP * LANES
    groups = n_heads * HEAD_DIM // width
    third = ATT_W // width
    first = first_head * HEAD_DIM // width
    return pl.pallas_call(
        kernel,
        grid=(batch, groups),
        in_specs=_pair_specs(seq, first, third + first, 2 * third + first),
        out_specs=pl.BlockSpec((seq, width), lambda b, p: (b, p)),
        out_shape=jax.ShapeDtypeStruct((batch * seq, groups * width), BF16),
        scratch_shapes=[pltpu.VMEM((PAIRS_PER_STEP, seq // ATT_TILE, LANES, ATT_TILE), BF16),
                        pltpu.VMEM((PAIRS_PER_STEP, LANES, 4 * ATT_TILE), F32),
                        *extra_scratch],
        compiler_params=_params(2),
        name=name,
    )(qkv, qkv, qkv)


def _sb_attention(qkv, batch, seq):
    staged = pltpu.VMEM((4 * PAIRS_PER_STEP, ATT_TILE, 4 * ATT_TILE), F32)
    return _attention_call(_sb_kernel, qkv, batch, seq, N_SB_HEADS, 0, [staged], "sb_attention")


def _moba_kernel(q_ref, k_ref, v_ref, o_ref, vt_ref, acc_ref, sel_ref, s_ref):
    t = ATT_TILE
    w = 2 * t
    seq = q_ref.shape[0]
    nb = seq // MOBA_BLOCK
    pairs = range(q_ref.shape[1] // LANES)
    per_half = 2 * len(pairs)
    causal, chan0 = _tile_masks(False)
    blk = lax.broadcasted_iota(jnp.int32, (nb, 2 * w), 0)
    cell = lax.broadcasted_iota(jnp.int32, (nb, 2 * w), 1) // w
    blk_f = blk.astype(F32)
    kmean = [(jnp.sum(k_ref[:, _pair_lanes(p)].astype(F32).reshape(nb, MOBA_BLOCK, LANES), axis=1)
              * (1.0 / MOBA_BLOCK)).astype(BF16) for p in pairs]
    _fill_vt(v_ref, vt_ref)


    def select(p, i, q2):
        gate = jnp.dot(kmean[p], q2, preferred_element_type=F32)
        past_blk = blk < 2 * i + cell
        gate = jnp.where(past_blk, gate, NEG)
        sel = jnp.zeros((nb, 2 * w), F32)
        for _r in range(MOBA_TOPK):
            top = jnp.max(gate, axis=0, keepdims=True)
            idx = jnp.min(jnp.where(gate == top, blk_f, float(nb)), axis=0, keepdims=True)
            hit = blk_f == idx
            sel = jnp.where(hit & past_blk, 1.0, sel)
            gate = jnp.where(hit, -jnp.inf, gate)
        sel_ref[p] = sel

    def own_blocks(p, i, raw_a, raw_b):
        s = jnp.where(causal, raw_b, NEG)
        m_b = jnp.max(s, axis=0, keepdims=True)
        pr = jnp.exp2(s - m_b)
        acc_ref[p, :, :w] = jnp.zeros((LANES, w), F32)
        acc_ref[p, :, w:] = jnp.dot(vt_ref[p, 2 * i + 1], pr.astype(BF16),
                                    preferred_element_type=F32)
        m = jnp.concatenate([jnp.full((1, w), -jnp.inf, F32), m_b], axis=1)
        l = jnp.concatenate([jnp.zeros((1, w), F32), jnp.sum(pr, axis=0, keepdims=True)], axis=1)
        keep = jnp.concatenate(
            [causal, jnp.broadcast_to(sel_ref[p, pl.ds(2 * i, 1), w:] > 0.0, (t, w))], axis=1)
        s = jnp.where(keep, raw_a, NEG)
        m_new = jnp.maximum(m, jnp.max(s, axis=0, keepdims=True))
        alpha = jnp.exp2(m - m_new)
        pr = jnp.exp2(s - m_new)
        acc_ref[p] = alpha * acc_ref[p] + jnp.dot(
            vt_ref[p, 2 * i], pr.astype(BF16), preferred_element_type=F32)
        return m_new, alpha * l + jnp.sum(pr, axis=0, keepdims=True)

    def past_pair(p, j, raw_a, raw_b, m_old, l_old):
        s_a = jnp.where(sel_ref[p, pl.ds(j, 1), :] > 0.0, raw_a, NEG)
        s_b = jnp.where(sel_ref[p, pl.ds(j + 1, 1), :] > 0.0, raw_b, NEG)
        m_new = jnp.maximum(m_old, jnp.maximum(jnp.max(s_a, axis=0, keepdims=True),
                                               jnp.max(s_b, axis=0, keepdims=True)))
        alpha = jnp.exp2(m_old - m_new)
        p_a = jnp.exp2(s_a - m_new)
        p_b = jnp.exp2(s_b - m_new)
        part = (jnp.dot(vt_ref[p, j], p_a.astype(BF16), preferred_element_type=F32)
                + jnp.dot(vt_ref[p, j + 1], p_b.astype(BF16), preferred_element_type=F32))
        acc_ref[p] = alpha * acc_ref[p] + part
        l_new = (alpha * l_old + jnp.sum(p_a, axis=0, keepdims=True)
                 + jnp.sum(p_b, axis=0, keepdims=True))
        return m_new, l_new

    def stage_pair_of_blocks(half, j, tiles):
        for p in pairs:
            for b in range(2):
                s_ref[half + 2 * p + b] = _scores(k_ref, p, j + b, tiles[p][0])

    def q_super(i, _):
        tiles = [_load_super_tile(q_ref, p, i, chan0) for p in pairs]
        own = lax.rem(i, 2) * per_half
        for p in pairs:
            q2, qb = tiles[p]
            s_ref[own + 2 * p + 1, :, w:] = _scores(k_ref, p, 2 * i + 1, qb)
            s_ref[own + 2 * p] = _scores(k_ref, p, 2 * i, q2)
            select(p, i, q2)
        stats = tuple(own_blocks(p, i, s_ref[own + 2 * p], s_ref[own + 2 * p + 1, :, w:])
                      for p in pairs)

        def past(jj, stats):
            half = lax.rem(jj, 2) * per_half
            stage_pair_of_blocks(half, 2 * jj, tiles)
            return tuple(past_pair(p, 2 * jj, s_ref[half + 2 * p], s_ref[half + 2 * p + 1],
                                   *stats[p]) for p in pairs)

        stats = lax.fori_loop(0, i, past, stats)
        for p in pairs:
            _store_super_tile(o_ref, p, i, acc_ref[p] / stats[p][1], chan0)
        return 0

    lax.fori_loop(0, seq // w, q_super, 0)


def _moba_attention(qkv, batch, seq):
    sel = pltpu.VMEM((PAIRS_PER_STEP, seq // MOBA_BLOCK, 4 * ATT_TILE), F32)
    staged = pltpu.VMEM((4 * PAIRS_PER_STEP, ATT_TILE, 4 * ATT_TILE), F32)
    return _attention_call(_moba_kernel, qkv, batch, seq, N_MOBA_HEADS, N_SB_HEADS,
                           [sel, staged], "moba_attention")


def _ret_kernel(q_ref, k_ref, v_ref, g_ref, gn_ref, o_ref, state_ref, decay_ref):
    c = RET_CHUNK
    heads = range(RET_HEADS_PER_STEP)
    n_row = lax.broadcasted_iota(jnp.int32, (c, 1), 0).astype(F32)
    log_g, xi, zeta, chunk_decay = [], [], [], []
    for hh in heads:
        head = (pl.program_id(1) * RET_HEADS_PER_STEP + hh + 5).astype(F32)
        lg = jnp.log(1.0 - jnp.exp2(-jnp.full((1, 1), head, F32)))
        log_g.append(lg)
        xi.append(jnp.exp(lg * (n_row + 1.0)))
        zeta.append(jnp.exp(lg * (c - 1.0 - n_row)))
        chunk_decay.append(jnp.exp(lg * c))

    @pl.when(pl.program_id(2) == 0)
    def _():
        diff = (lax.broadcasted_iota(jnp.int32, (c, c), 0)
                - lax.broadcasted_iota(jnp.int32, (c, c), 1)).astype(F32)
        for hh in heads:
            decay_ref[hh] = jnp.where(diff >= 0, jnp.exp(log_g[hh] * jnp.maximum(diff, 0.0)), 0.0)
        state_ref[...] = jnp.zeros_like(state_ref)

    def chunk(i, _):
        rows = pl.ds(pl.multiple_of(i * c, c), c)
        for hh in heads:
            qk_cols = slice(hh * RET_DK, (hh + 1) * RET_DK)
            v_cols = slice(hh * RET_DV, (hh + 1) * RET_DV)
            q, k, v = q_ref[rows, qk_cols], k_ref[rows, qk_cols], v_ref[rows, v_cols]
            inner = lax.dot_general(q, k, (((1,), (1,)), ((), ())),
                                    preferred_element_type=F32) * decay_ref[hh]
            state = state_ref[hh]
            o = (jnp.dot(inner.astype(BF16), v, preferred_element_type=F32)
                 + jnp.dot(q, state.astype(BF16), preferred_element_type=F32) * xi[hh])
            kz = (k.astype(F32) * zeta[hh]).astype(BF16)
            state_ref[hh] = state * chunk_decay[hh] + lax.dot_general(
                kz, v, (((0,), (0,)), ((), ())), preferred_element_type=F32)
            mu = jnp.mean(o, axis=-1, keepdims=True)
            var = jnp.mean(jnp.square(o - mu), axis=-1, keepdims=True)
            on = (o - mu) * lax.rsqrt(var + GN_EPS) * gn_ref[:, v_cols]
            o_ref[rows, v_cols] = (_silu(g_ref[rows, v_cols].astype(F32)) * on).astype(BF16)
        return 0

    lax.fori_loop(0, q_ref.shape[0] // c, chunk, 0)


def _retention(proj, gn_gain, batch, seq):
    hps = RET_HEADS_PER_STEP
    tile = RET_SEQ_TILE
    tiles = seq // tile
    qk_w, v_w = hps * RET_DK, hps * RET_DV
    k0 = RET_QK_W // qk_w
    v0 = 2 * RET_QK_W // v_w
    g0 = v0 + RET_V_W // v_w
    row = lambda b, h, s: b * tiles + s
    return pl.pallas_call(
        _ret_kernel,
        grid=(batch, N_RET_HEADS // hps, tiles),
        in_specs=[
            pl.BlockSpec((tile, qk_w), lambda b, h, s: (row(b, h, s), h)),
            pl.BlockSpec((tile, qk_w), lambda b, h, s: (row(b, h, s), k0 + h)),
            pl.BlockSpec((tile, v_w), lambda b, h, s: (row(b, h, s), v0 + h)),
            pl.BlockSpec((tile, v_w), lambda b, h, s: (row(b, h, s), g0 + h)),
            pl.BlockSpec((1, v_w), lambda b, h, s: (0, h)),
        ],
        out_specs=pl.BlockSpec((tile, v_w), lambda b, h, s: (row(b, h, s), h)),
        out_shape=jax.ShapeDtypeStruct((batch * seq, RET_V_W), BF16),
        scratch_shapes=[pltpu.VMEM((hps, RET_DK, RET_DV), F32),
                        pltpu.VMEM((hps, RET_CHUNK, RET_CHUNK), F32)],
        compiler_params=_params(3),
        name="retention",
    )(proj, proj, proj, proj, gn_gain)


def _mix_ffn_kernel(*refs, n_in, final_norm):
    x_ref, gate_m_ref, g_ref, sh_ref, sc_ref, gate_f_ref = refs[:6]
    a_refs, w_refs = refs[6:6 + n_in], refs[6 + n_in:6 + 2 * n_in]
    wgu_ref, wd_ref, fin_ref, o_ref = refs[6 + 2 * n_in:]
    y = jnp.dot(a_refs[0][...], w_refs[0][...], preferred_element_type=F32)
    for a_ref, w_ref in zip(a_refs[1:], w_refs[1:]):
        y += jnp.dot(a_ref[...], w_ref[...], preferred_element_type=F32)
    x = x_ref[...] + gate_m_ref[0] * y
    h = _norm_mod(x, g_ref[...], sh_ref[0], sc_ref[0]).astype(BF16)
    half = D_FF // 2
    y = None
    for c in range(2):
        cols = slice(c * half, (c + 1) * half)
        gt = jnp.dot(h, wgu_ref[:, cols], preferred_element_type=F32)
        up = jnp.dot(h, wgu_ref[:, D_FF + c * half:D_FF + (c + 1) * half],
                     preferred_element_type=F32)
        part = jnp.dot((_silu(gt) * up).astype(BF16), wd_ref[cols, :], preferred_element_type=F32)
        y = part if y is None else y + part
    out = x + gate_f_ref[0] * y
    if final_norm:
        out = out * lax.rsqrt(jnp.mean(out * out, axis=-1, keepdims=True) + NORM_EPS) * fin_ref[...]
    o_ref[...] = out


def _mix_ffn(x, gate_m, acts, w_outs, gain, shift, scale, gate_f, w_gate_up, w_down, layer,
             fin_gain, seq, final_norm, name):
    t, d = x.shape
    tm = ROW_TILE
    per_seq = seq // tm
    row = lambda i: (i, 0)
    batch = lambda i: (i // per_seq, 0, 0)
    const = lambda i: (0, 0)
    this_layer = lambda i: (layer, 0, 0)
    return pl.pallas_call(
        functools.partial(_mix_ffn_kernel, n_in=len(acts), final_norm=final_norm),
        grid=(t // tm,),
        in_specs=[
            pl.BlockSpec((tm, d), row),
            pl.BlockSpec((1, 1, d), batch),
            pl.BlockSpec((1, d), const),
            pl.BlockSpec((1, 1, d), batch),
            pl.BlockSpec((1, 1, d), batch),
            pl.BlockSpec((1, 1, d), batch),
            *[pl.BlockSpec((tm, a.shape[1]), row) for a in acts],
            *[_resident(w.shape, const) for w in w_outs],
            _resident((None,) + w_gate_up.shape[1:], this_layer),
            _resident((None,) + w_down.shape[1:], this_layer),
            pl.BlockSpec((1, d), const),
        ],
        out_specs=pl.BlockSpec((tm, d), row),
        out_shape=jax.ShapeDtypeStruct((t, d), F32),
        compiler_params=_params(1),
        name=name,
    )(x, gate_m, gain, shift, scale, gate_f, *acts, *w_outs, w_gate_up, w_down, fin_gain)


def _rope_tables(seq, dim, tile):
    inv = ROPE_THETA ** (-jnp.arange(0, dim, 2, dtype=F32) / dim)
    ang = jnp.arange(seq, dtype=jnp.int32).astype(F32)[:, None] * inv[None, :]
    return jnp.tile(jnp.cos(ang), (1, tile)), jnp.tile(jnp.sin(ang), (1, tile))


def kernel(x, c, ada_w, ada_b, norm_gains, att_w_qkv, att_w_o, ret_w_in, ret_gn, ret_w_o,
           ffn_w_gate_up, ffn_w_down, final_norm):
    batch, seq, d = x.shape
    depth = ada_w.shape[0]
    assert d == D_MODEL and batch <= 8 and seq % ROW_TILE == 0
    assert seq % (2 * ATT_TILE) == 0 and seq % RET_SEQ_TILE == 0

    mod = _ada_mod(c, ada_w, ada_b)[:, :batch].reshape(depth, batch, 1, 6, d)
    xt = x.reshape(batch * seq, d)

    half = HEAD_DIM // 2
    cos_a, sin_a = _rope_tables(seq, HEAD_DIM, LANES // half)
    first_half = (jnp.arange(LANES) % HEAD_DIM) < half
    att_tables = (cos_a, jnp.where(first_half, -sin_a, 0.0), jnp.where(first_half, 0.0, sin_a))
    ret_tables = _rope_tables(seq, RET_DK, 1)
    w_gate_up, w_down = ffn_w_gate_up.astype(BF16), ffn_w_down.astype(BF16)

    for layer in range(depth):
        shift_m, scale_m, gate_m, shift_f, scale_f, gate_f = (mod[layer, :, :, i] for i in range(6))
        gains = norm_gains[layer]
        j = layer // 2
        if layer % 2 == 0:
            qkv = _proj(_att_proj_kernel, xt, gains[0:1], shift_m, scale_m, att_tables,
                        att_w_qkv[j].astype(BF16), seq, "att_proj")
            sb = _sb_attention(qkv, batch, seq)
            mb = _moba_attention(qkv, batch, seq)
            w_o = att_w_o[j].astype(BF16)
            split = N_SB_HEADS * HEAD_DIM
            acts, w_outs = (sb, mb), (w_o[:split], w_o[split:])
        else:
            proj = _proj(_ret_proj_kernel, xt, gains[0:1], shift_m, scale_m, ret_tables,
                         ret_w_in[j].astype(BF16), seq, "ret_proj")
            acts = (_retention(proj, ret_gn[j].reshape(1, RET_V_W), batch, seq),)
            w_outs = (ret_w_o[j].astype(BF16),)
        xt = _mix_ffn(xt, gate_m, acts, w_outs, gains[1:2], shift_f, scale_f, gate_f, w_gate_up,
                      w_down, layer, final_norm.reshape(1, d), seq, layer == depth - 1,
                      "mix_ffn%d" % layer)
    return xt.reshape(batch, seq, d)
```

```python
import functools
import math

import jax
import jax.numpy as jnp
from jax import lax
from jax.experimental import pallas as pl
from jax.experimental.pallas import tpu as pltpu

F32 = jnp.float32
BF16 = jnp.bfloat16

D_MODEL = 1024
HEAD_DIM = 64
N_SB_HEADS = 8
N_MOBA_HEADS = 8
ATT_W = (N_SB_HEADS + N_MOBA_HEADS) * HEAD_DIM
MOBA_BLOCK = 256
MOBA_TOPK = 3
N_RET_HEADS = 4
RET_DK = D_MODEL // N_RET_HEADS
RET_DV = 2 * RET_DK
RET_QK_W = N_RET_HEADS * RET_DK
RET_V_W = N_RET_HEADS * RET_DV
D_FF = -(-8 * D_MODEL // 768) * 256
ROPE_THETA = 10000.0
NORM_EPS = 1e-6
GN_EPS = 1e-5
NEG = -1e30
Q_SCALE = HEAD_DIM ** -0.5 * math.log2(math.e)

LANES = 128
ROW_TILE = 512
COL_CHUNK = 512
MXU_WIDTH = 256
_FF_SPLIT = (D_FF // MXU_WIDTH + 1) // 2 * MXU_WIDTH
FF_CHUNKS = ((0, _FF_SPLIT), (_FF_SPLIT, D_FF))
ATT_TILE = 256
PAIRS_PER_STEP = 2
RET_CHUNK = 256
RET_SEQ_TILE = 1024
RET_HEADS_PER_STEP = 4
VMEM_LIMIT = 56 * 1024 * 1024


def _params(n_axes):
    return pltpu.CompilerParams(
        dimension_semantics=("arbitrary",) * n_axes, vmem_limit_bytes=VMEM_LIMIT)


def _resident(shape, index_map):
    return pl.BlockSpec(shape, index_map, pipeline_mode=pl.Buffered(1))


def _silu(v):
    return v * (1.0 / (1.0 + jnp.exp(-v)))


def _ada_kernel(c_ref, w_ref, b_ref, o_ref):
    o_ref[0] = jnp.dot(_silu(c_ref[...]), w_ref[0], preferred_element_type=F32) + b_ref[0]


def _ada_mod(c, ada_w, ada_b):
    depth, d, n = ada_w.shape
    cp = jnp.pad(c, ((0, 8 - c.shape[0]), (0, 0)))
    tn = 1536
    return pl.pallas_call(
        _ada_kernel,
        grid=(depth, n // tn),
        in_specs=[
            pl.BlockSpec((8, d), lambda l, j: (0, 0)),
            pl.BlockSpec((1, d, tn), lambda l, j: (l, 0, j)),
            pl.BlockSpec((1, 1, tn), lambda l, j: (l, 0, j)),
        ],
        out_specs=pl.BlockSpec((1, 8, tn), lambda l, j: (l, 0, j)),
        out_shape=jax.ShapeDtypeStruct((depth, 8, n), F32),
        compiler_params=_params(2),
        name="ada_mod",
    )(cp, ada_w, ada_b.reshape(depth, 1, n))


def _norm_mod(x, g, shift, scale):
    y = x * lax.rsqrt(jnp.mean(x * x, axis=-1, keepdims=True) + NORM_EPS)
    return (y * g) * (1.0 + scale) + shift


def _att_proj_kernel(x_ref, g_ref, sh_ref, sc_ref, cos_ref, sina_ref, sinb_ref, w_ref, o_ref):
    h = _norm_mod(x_ref[...], g_ref[...], sh_ref[0], sc_ref[0]).astype(BF16)
    cos, sina, sinb = cos_ref[...], sina_ref[...], sinb_ref[...]
    sb_w = N_SB_HEADS * HEAD_DIM
    for c in range(3 * ATT_W // COL_CHUNK):
        lo = c * COL_CHUNK
        y = jnp.dot(h, w_ref[:, lo:lo + COL_CHUNK], preferred_element_type=F32)
        is_q = lo < ATT_W
        rotated = (lo % ATT_W) >= sb_w and lo < 2 * ATT_W
        for s in range(COL_CHUNK // LANES):
            ys = y[:, s * LANES:(s + 1) * LANES]
            if rotated:
                ys = (ys * cos + pltpu.roll(ys, LANES - HEAD_DIM // 2, 1) * sina
                      + pltpu.roll(ys, HEAD_DIM // 2, 1) * sinb)
            if is_q:
                ys = ys * Q_SCALE
            o_ref[:, lo + s * LANES:lo + (s + 1) * LANES] = ys.astype(BF16)


def _ret_proj_kernel(x_ref, g_ref, sh_ref, sc_ref, cos_ref, sin_ref, w_ref, o_ref):
    h = _norm_mod(x_ref[...], g_ref[...], sh_ref[0], sc_ref[0]).astype(BF16)
    cos, sin = cos_ref[...], sin_ref[...]
    n = 2 * RET_QK_W + 2 * RET_V_W
    for c in range(n // COL_CHUNK):
        lo = c * COL_CHUNK
        y = jnp.dot(h, w_ref[:, lo:lo + COL_CHUNK], preferred_element_type=F32)
        if lo < 2 * RET_QK_W:
            mul = 1.0 if lo < RET_QK_W else RET_DK ** -0.5
            half = RET_DK // 2
            for hd in range(COL_CHUNK // RET_DK):
                x1 = y[:, hd * RET_DK:hd * RET_DK + half]
                x2 = y[:, hd * RET_DK + half:(hd + 1) * RET_DK]
                o1 = (x1 * cos - x2 * sin) * mul
                o2 = (x2 * cos + x1 * sin) * mul
                o_ref[:, lo + hd * RET_DK:lo + hd * RET_DK + half] = o1.astype(BF16)
                o_ref[:, lo + hd * RET_DK + half:lo + (hd + 1) * RET_DK] = o2.astype(BF16)
        else:
            o_ref[:, lo:lo + COL_CHUNK] = y.astype(BF16)


def _proj(kernel, x, gain, shift, scale, tables, w, seq, name):
    t, d = x.shape
    n = w.shape[1]
    tm = ROW_TILE
    per_seq = seq // tm
    row = lambda i: (i, 0)
    batch = lambda i: (i // per_seq, 0, 0)
    pos = lambda i: (i % per_seq, 0)
    return pl.pallas_call(
        kernel,
        grid=(t // tm,),
        in_specs=[
            pl.BlockSpec((tm, d), row),
            pl.BlockSpec((1, d), lambda i: (0, 0)),
            pl.BlockSpec((1, 1, d), batch),
            pl.BlockSpec((1, 1, d), batch),
            *[pl.BlockSpec((tm, LANES), pos) for _ in tables],
            _resident((d, n), lambda i: (0, 0)),
        ],
        out_specs=pl.BlockSpec((tm, n), row),
        out_shape=jax.ShapeDtypeStruct((t, n), BF16),
        compiler_params=_params(1),
        name=name,
    )(x, gain, shift, scale, *tables, w)


def _pair_lanes(p):
    return slice(p * LANES, (p + 1) * LANES)


def _fill_vt(v_ref, vt_ref):
    t = ATT_TILE
    for p in range(v_ref.shape[1] // LANES):
        for n in range(v_ref.shape[0] // t):
            vt_ref[p, n] = v_ref[n * t:(n + 1) * t, _pair_lanes(p)].T


def _tile_masks(strict):
    t = ATT_TILE
    key = lax.broadcasted_iota(jnp.int32, (t, 2 * t), 0)
    qry = lax.broadcasted_iota(jnp.int32, (t, 2 * t), 1) % t
    chan0 = lax.broadcasted_iota(jnp.int32, (LANES, 1), 0) < HEAD_DIM
    return (key < qry) if strict else (key <= qry), chan0


def _load_super_tile(q_ref, p, i, chan0):
    t = ATT_TILE
    base = pl.multiple_of(i * 2 * t, 2 * t)
    cells = []
    for c in range(2):
        q_t = q_ref[pl.ds(base + c * t, t), _pair_lanes(p)].T
        zero = jnp.zeros_like(q_t)
        cells += [jnp.where(chan0, q_t, zero), jnp.where(chan0, zero, q_t)]
    return jnp.concatenate(cells, axis=1), jnp.concatenate(cells[2:], axis=1)


def _store_super_tile(o_ref, p, i, acc_t, chan0):
    t = ATT_TILE
    base = pl.multiple_of(i * 2 * t, 2 * t)
    for cell in range(2):
        lo = cell * 2 * t
        o_t = jnp.where(chan0, acc_t[:, lo:lo + t], acc_t[:, lo + t:lo + 2 * t])
        o_ref[pl.ds(base + cell * t, t), _pair_lanes(p)] = o_t.T.astype(BF16)


def _scores(k_ref, p, j, q_cols):
    t = ATT_TILE
    return jnp.dot(k_ref[pl.ds(pl.multiple_of(j * t, t), t), _pair_lanes(p)], q_cols,
                   preferred_element_type=F32)


def _sb_kernel(q_ref, k_ref, v_ref, o_ref, vt_ref, acc_ref, z_ref):
    t = ATT_TILE
    w = 2 * t
    seq = q_ref.shape[0]
    pairs = range(q_ref.shape[1] // LANES)
    per_half = 2 * len(pairs)
    strict, chan0 = _tile_masks(True)
    from_key = -(lax.broadcasted_iota(jnp.int32, (t, t), 1)
                 >= lax.broadcasted_iota(jnp.int32, (t, t), 0)).astype(BF16)
    _fill_vt(v_ref, vt_ref)

    def block(p, j, slot, lanes, run, mask):
        z = z_ref[slot, :, lanes]
        neg_l = jnp.maximum(z, 0.0) + jnp.log2(1.0 + jnp.exp2(-jnp.abs(z)))
        if mask is not None:
            neg_l = jnp.where(mask, neg_l, 0.0)
        since = jnp.dot(from_key, neg_l.astype(BF16), preferred_element_type=F32)
        a = jnp.exp2(z_ref[slot, :, lanes] + since)
        if mask is not None:
            a = jnp.where(mask, a, 0.0)
        part = jnp.dot(vt_ref[p, j], a.astype(BF16), preferred_element_type=F32)
        return run + since[0:1, :], part * jnp.exp2(run)

    def stage_pair_of_blocks(half, j, tiles):
        for p in pairs:
            for b in range(2):
                z_ref[half + 2 * p + b] = _scores(k_ref, p, j + b, tiles[p][0])

    def q_super(i, _):
        tiles = [_load_super_tile(q_ref, p, i, chan0) for p in pairs]
        mask = jnp.concatenate([strict, jnp.ones((t, w), jnp.bool_)], axis=1)
        cell1 = slice(w, 2 * w)
        diag = lax.rem(i, 2) * per_half
        for p in pairs:
            q2, qb = tiles[p]
            z_ref[diag + 2 * p + 1, :, cell1] = _scores(k_ref, p, 2 * i + 1, qb)
            z_ref[diag + 2 * p] = _scores(k_ref, p, 2 * i, q2)
        runs = []
        for p in pairs:
            run_b, part = block(p, 2 * i + 1, diag + 2 * p + 1, cell1, jnp.zeros((1, w), F32),
                                strict)
            acc_ref[p, :, w:] = part
            run = jnp.concatenate([jnp.zeros((1, w), F32), run_b], axis=1)
            run, part = block(p, 2 * i, diag + 2 * p, slice(None), run, mask)
            acc_ref[p, :, :w] = part[:, :w]
            acc_ref[p, :, w:] += part[:, w:]
            runs.append(run)

        def past(s, runs):
            j = 2 * (i - 1 - s)
            half = lax.rem(s, 2) * per_half
            stage_pair_of_blocks(half, j, tiles)
            out = []
            for p in pairs:
                run, part_b = block(p, j + 1, half + 2 * p + 1, slice(None), runs[p], None)
                run, part_a = block(p, j, half + 2 * p, slice(None), run, None)
                acc_ref[p] += part_b + part_a
                out.append(run)
            return tuple(out)

        lax.fori_loop(0, i, past, tuple(runs))
        for p in pairs:
            _store_super_tile(o_ref, p, i, acc_ref[p], chan0)
        return 0

    lax.fori_loop(0, seq // w, q_super, 0)


def _pair_specs(seq, q_col, k_col, v_col):
    width = PAIRS_PER_STEP * LANES
    return [
        pl.BlockSpec((seq, width), lambda b, p: (b, q_col + p)),
        pl.BlockSpec((seq, width), lambda b, p: (b, k_col + p)),
        pl.BlockSpec((seq, width), lambda b, p: (b, v_col + p)),
    ]


def _attention_call(kernel, qkv, batch, seq, n_heads, first_head, extra_scratch, name):
    width = PAIRS_PER_STEP * LANES
    groups = n_heads * HEAD_DIM // width
    third = ATT_W // width
    first = first_head * HEAD_DIM // width
    return pl.pallas_call(
        kernel,
        grid=(batch, groups),
        in_specs=_pair_specs(seq, first, third + first, 2 * third + first),
        out_specs=pl.BlockSpec((seq, width), lambda b, p: (b, p)),
        out_shape=jax.ShapeDtypeStruct((batch * seq, groups * width), BF16),
        scratch_shapes=[pltpu.VMEM((PAIRS_PER_STEP, seq // ATT_TILE, LANES, ATT_TILE), BF16),
                        pltpu.VMEM((PAIRS_PER_STEP, LANES, 4 * ATT_TILE), F32),
                        *extra_scratch],
        compiler_params=_params(2),
        name=name,
    )(qkv, qkv, qkv)


def _sb_attention(qkv, batch, seq):
    staged = pltpu.VMEM((4 * PAIRS_PER_STEP, ATT_TILE, 4 * ATT_TILE), F32)
    return _attention_call(_sb_kernel, qkv, batch, seq, N_SB_HEADS, 0, [staged], "sb_attention")


def _moba_kernel(q_ref, k_ref, v_ref, o_ref, vt_ref, acc_ref, sel_ref, s_ref):
    t = ATT_TILE
    w = 2 * t
    seq = q_ref.shape[0]
    nb = seq // MOBA_BLOCK
    pairs = range(q_ref.shape[1] // LANES)
    per_half = 2 * len(pairs)
    causal, chan0 = _tile_masks(False)
    blk = lax.broadcasted_iota(jnp.int32, (nb, 2 * w), 0)
    cell = lax.broadcasted_iota(jnp.int32, (nb, 2 * w), 1) // w
    blk_f = blk.astype(F32)
    kmean = [(jnp.sum(k_ref[:, _pair_lanes(p)].astype(F32).reshape(nb, MOBA_BLOCK, LANES), axis=1)
              * (1.0 / MOBA_BLOCK)).astype(BF16) for p in pairs]
    _fill_vt(v_ref, vt_ref)


    def select(p, i, q2):
        gate = jnp.dot(kmean[p], q2, preferred_element_type=F32)
        past_blk = blk < 2 * i + cell
        gate = jnp.where(past_blk, gate, NEG)
        sel = jnp.zeros((nb, 2 * w), F32)
        for _r in range(MOBA_TOPK):
            top = jnp.max(gate, axis=0, keepdims=True)
            idx = jnp.min(jnp.where(gate == top, blk_f, float(nb)), axis=0, keepdims=True)
            hit = blk_f == idx
            sel = jnp.where(hit & past_blk, 1.0, sel)
            gate = jnp.where(hit, -jnp.inf, gate)
        sel_ref[p] = sel

    def own_blocks(p, i, raw_a, raw_b):
        s = jnp.where(causal, raw_b, NEG)
        m_b = jnp.max(s, axis=0, keepdims=True)
        pr = jnp.exp2(s - m_b)
        acc_ref[p, :, :w] = jnp.zeros((LANES, w), F32)
        acc_ref[p, :, w:] = jnp.dot(vt_ref[p, 2 * i + 1], pr.astype(BF16),
                                    preferred_element_type=F32)
        m = jnp.concatenate([jnp.full((1, w), -jnp.inf, F32), m_b], axis=1)
        l = jnp.concatenate([jnp.zeros((1, w), F32), jnp.sum(pr, axis=0, keepdims=True)], axis=1)
        keep = jnp.concatenate(
            [causal, jnp.broadcast_to(sel_ref[p, pl.ds(2 * i, 1), w:] > 0.0, (t, w))], axis=1)
        s = jnp.where(keep, raw_a, NEG)
        m_new = jnp.maximum(m, jnp.max(s, axis=0, keepdims=True))
        alpha = jnp.exp2(m - m_new)
        pr = jnp.exp2(s - m_new)
        acc_ref[p] = alpha * acc_ref[p] + jnp.dot(
            vt_ref[p, 2 * i], pr.astype(BF16), preferred_element_type=F32)
        return m_new, alpha * l + jnp.sum(pr, axis=0, keepdims=True)

    def past_pair(p, j, raw_a, raw_b, m_old, l_old):
        s_a = jnp.where(sel_ref[p, pl.ds(j, 1), :] > 0.0, raw_a, NEG)
        s_b = jnp.where(sel_ref[p, pl.ds(j + 1, 1), :] > 0.0, raw_b, NEG)
        m_new = jnp.maximum(m_old, jnp.maximum(jnp.max(s_a, axis=0, keepdims=True),
                                               jnp.max(s_b, axis=0, keepdims=True)))
        alpha = jnp.exp2(m_old - m_new)
        p_a = jnp.exp2(s_a - m_new)
        p_b = jnp.exp2(s_b - m_new)
        part = (jnp.dot(vt_ref[p, j], p_a.astype(BF16), preferred_element_type=F32)
                + jnp.dot(vt_ref[p, j + 1], p_b.astype(BF16), preferred_element_type=F32))
        acc_ref[p] = alpha * acc_ref[p] + part
        l_new = (alpha * l_old + jnp.sum(p_a, axis=0, keepdims=True)
                 + jnp.sum(p_b, axis=0, keepdims=True))
        return m_new, l_new

    def stage_pair_of_blocks(half, j, tiles):
        for p in pairs:
            for b in range(2):
                s_ref[half + 2 * p + b] = _scores(k_ref, p, j + b, tiles[p][0])

    def q_super(i, _):
        tiles = [_load_super_tile(q_ref, p, i, chan0) for p in pairs]
        own = lax.rem(i, 2) * per_half
        for p in pairs:
            q2, qb = tiles[p]
            s_ref[own + 2 * p + 1, :, w:] = _scores(k_ref, p, 2 * i + 1, qb)
            s_ref[own + 2 * p] = _scores(k_ref, p, 2 * i, q2)
            select(p, i, q2)
        stats = tuple(own_blocks(p, i, s_ref[own + 2 * p], s_ref[own + 2 * p + 1, :, w:])
                      for p in pairs)

        def past(jj, stats):
            half = lax.rem(jj, 2) * per_half
            stage_pair_of_blocks(half, 2 * jj, tiles)
            return tuple(past_pair(p, 2 * jj, s_ref[half + 2 * p], s_ref[half + 2 * p + 1],
                                   *stats[p]) for p in pairs)

        stats = lax.fori_loop(0, i, past, stats)
        for p in pairs:
            _store_super_tile(o_ref, p, i, acc_ref[p] / stats[p][1], chan0)
        return 0

    lax.fori_loop(0, seq // w, q_super, 0)


def _moba_attention(qkv, batch, seq):
    sel = pltpu.VMEM((PAIRS_PER_STEP, seq // MOBA_BLOCK, 4 * ATT_TILE), F32)
    staged = pltpu.VMEM((4 * PAIRS_PER_STEP, ATT_TILE, 4 * ATT_TILE), F32)
    return _attention_call(_moba_kernel, qkv, batch, seq, N_MOBA_HEADS, N_SB_HEADS,
                           [sel, staged], "moba_attention")


def _ret_kernel(q_ref, k_ref, v_ref, g_ref, gn_ref, o_ref, state_ref, decay_ref):
    c = RET_CHUNK
    heads = range(RET_HEADS_PER_STEP)
    n_row = lax.broadcasted_iota(jnp.int32, (c, 1), 0).astype(F32)
    log_g, xi, zeta, chunk_decay = [], [], [], []
    for hh in heads:
        head = (pl.program_id(1) * RET_HEADS_PER_STEP + hh + 5).astype(F32)
        lg = jnp.log(1.0 - jnp.exp2(-jnp.full((1, 1), head, F32)))
        log_g.append(lg)
        xi.append(jnp.exp(lg * (n_row + 1.0)))
        zeta.append(jnp.exp(lg * (c - 1.0 - n_row)))
        chunk_decay.append(jnp.exp(lg * c))

    @pl.when(pl.program_id(2) == 0)
    def _():
        diff = (lax.broadcasted_iota(jnp.int32, (c, c), 0)
                - lax.broadcasted_iota(jnp.int32, (c, c), 1)).astype(F32)
        for hh in heads:
            decay_ref[hh] = jnp.where(diff >= 0, jnp.exp(log_g[hh] * jnp.maximum(diff, 0.0)), 0.0)
        state_ref[...] = jnp.zeros_like(state_ref)

    def chunk(i, _):
        rows = pl.ds(pl.multiple_of(i * c, c), c)
        for hh in heads:
            qk_cols = slice(hh * RET_DK, (hh + 1) * RET_DK)
            v_cols = slice(hh * RET_DV, (hh + 1) * RET_DV)
            q, k, v = q_ref[rows, qk_cols], k_ref[rows, qk_cols], v_ref[rows, v_cols]
            inner = lax.dot_general(q, k, (((1,), (1,)), ((), ())),
                                    preferred_element_type=F32) * decay_ref[hh]
            state = state_ref[hh]
            o = (jnp.dot(inner.astype(BF16), v, preferred_element_type=F32)
                 + jnp.dot(q, state.astype(BF16), preferred_element_type=F32) * xi[hh])
            kz = (k.astype(F32) * zeta[hh]).astype(BF16)
            state_ref[hh] = state * chunk_decay[hh] + lax.dot_general(
                kz, v, (((0,), (0,)), ((), ())), preferred_element_type=F32)
            mu = jnp.mean(o, axis=-1, keepdims=True)
            var = jnp.mean(jnp.square(o - mu), axis=-1, keepdims=True)
            on = (o - mu) * lax.rsqrt(var + GN_EPS) * gn_ref[:, v_cols]
            o_ref[rows, v_cols] = (_silu(g_ref[rows, v_cols].astype(F32)) * on).astype(BF16)
        return 0

    lax.fori_loop(0, q_ref.shape[0] // c, chunk, 0)


def _retention(proj, gn_gain, batch, seq):
    hps = RET_HEADS_PER_STEP
    tile = RET_SEQ_TILE
    tiles = seq // tile
    qk_w, v_w = hps * RET_DK, hps * RET_DV
    k0 = RET_QK_W // qk_w
    v0 = 2 * RET_QK_W // v_w
    g0 = v0 + RET_V_W // v_w
    row = lambda b, h, s: b * tiles + s
    return pl.pallas_call(
        _ret_kernel,
        grid=(batch, N_RET_HEADS // hps, tiles),
        in_specs=[
            pl.BlockSpec((tile, qk_w), lambda b, h, s: (row(b, h, s), h)),
            pl.BlockSpec((tile, qk_w), lambda b, h, s: (row(b, h, s), k0 + h)),
            pl.BlockSpec((tile, v_w), lambda b, h, s: (row(b, h, s), v0 + h)),
            pl.BlockSpec((tile, v_w), lambda b, h, s: (row(b, h, s), g0 + h)),
            pl.BlockSpec((1, v_w), lambda b, h, s: (0, h)),
        ],
        out_specs=pl.BlockSpec((tile, v_w), lambda b, h, s: (row(b, h, s), h)),
        out_shape=jax.ShapeDtypeStruct((batch * seq, RET_V_W), BF16),
        scratch_shapes=[pltpu.VMEM((hps, RET_DK, RET_DV), F32),
                        pltpu.VMEM((hps, RET_CHUNK, RET_CHUNK), F32)],
        compiler_params=_params(3),
        name="retention",
    )(proj, proj, proj, proj, gn_gain)


def _mix_ffn_kernel(*refs, n_in, final_norm):
    x_ref, gate_m_ref, g_ref, sh_ref, sc_ref, gate_f_ref = refs[:6]
    a_refs, w_refs = refs[6:6 + n_in], refs[6 + n_in:6 + 2 * n_in]
    wgu_ref, wd_ref, fin_ref, o_ref = refs[6 + 2 * n_in:]
    y = jnp.dot(a_refs[0][...], w_refs[0][...], preferred_element_type=F32)
    for a_ref, w_ref in zip(a_refs[1:], w_refs[1:]):
        y += jnp.dot(a_ref[...], w_ref[...], preferred_element_type=F32)
    x = x_ref[...] + gate_m_ref[0] * y
    h = _norm_mod(x, g_ref[...], sh_ref[0], sc_ref[0]).astype(BF16)
    y = None
    for lo, hi in FF_CHUNKS:
        gt = jnp.dot(h, wgu_ref[:, lo:hi], preferred_element_type=F32)
        up = jnp.dot(h, wgu_ref[:, D_FF + lo:D_FF + hi], preferred_element_type=F32)
        part = jnp.dot((_silu(gt) * up).astype(BF16), wd_ref[lo:hi, :],
                       preferred_element_type=F32)
        y = part if y is None else y + part
    out = x + gate_f_ref[0] * y
    if final_norm:
        out = out * lax.rsqrt(jnp.mean(out * out, axis=-1, keepdims=True) + NORM_EPS) * fin_ref[...]
    o_ref[...] = out


def _mix_ffn(x, gate_m, acts, w_outs, gain, shift, scale, gate_f, w_gate_up, w_down, layer,
             fin_gain, seq, final_norm, name):
    t, d = x.shape
    tm = ROW_TILE
    per_seq = seq // tm
    row = lambda i: (i, 0)
    batch = lambda i: (i // per_seq, 0, 0)
    const = lambda i: (0, 0)
    this_layer = lambda i: (layer, 0, 0)
    return pl.pallas_call(
        functools.partial(_mix_ffn_kernel, n_in=len(acts), final_norm=final_norm),
        grid=(t // tm,),
        in_specs=[
            pl.BlockSpec((tm, d), row),
            pl.BlockSpec((1, 1, d), batch),
            pl.BlockSpec((1, d), const),
            pl.BlockSpec((1, 1, d), batch),
            pl.BlockSpec((1, 1, d), batch),
            pl.BlockSpec((1, 1, d), batch),
            *[pl.BlockSpec((tm, a.shape[1]), row) for a in acts],
            *[_resident(w.shape, const) for w in w_outs],
            _resident((None,) + w_gate_up.shape[1:], this_layer),
            _resident((None,) + w_down.shape[1:], this_layer),
            pl.BlockSpec((1, d), const),
        ],
        out_specs=pl.BlockSpec((tm, d), row),
        out_shape=jax.ShapeDtypeStruct((t, d), F32),
        compiler_params=_params(1),
        name=name,
    )(x, gate_m, gain, shift, scale, gate_f, *acts, *w_outs, w_gate_up, w_down, fin_gain)


def _rope_tables(seq, dim, tile):
    inv = ROPE_THETA ** (-jnp.arange(0, dim, 2, dtype=F32) / dim)
    ang = jnp.arange(seq, dtype=jnp.int32).astype(F32)[:, None] * inv[None, :]
    return jnp.tile(jnp.cos(ang), (1, tile)), jnp.tile(jnp.sin(ang), (1, tile))


def kernel(x, c, ada_w, ada_b, norm_gains, att_w_qkv, att_w_o, ret_w_in, ret_gn, ret_w_o,
           ffn_w_gate_up, ffn_w_down, final_norm):
    batch, seq, d = x.shape
    depth = ada_w.shape[0]
    assert d == D_MODEL and batch <= 8 and seq % ROW_TILE == 0
    assert seq % (2 * ATT_TILE) == 0 and seq % RET_SEQ_TILE == 0

    mod = _ada_mod(c, ada_w, ada_b)[:, :batch].reshape(depth, batch, 1, 6, d)
    xt = x.reshape(batch * seq, d)

    half = HEAD_DIM // 2
    cos_a, sin_a = _rope_tables(seq, HEAD_DIM, LANES // half)
    first_half = (jnp.arange(LANES) % HEAD_DIM) < half
    att_tables = (cos_a, jnp.where(first_half, -sin_a, 0.0), jnp.where(first_half, 0.0, sin_a))
    ret_tables = _rope_tables(seq, RET_DK, 1)
    w_gate_up, w_down = ffn_w_gate_up.astype(BF16), ffn_w_down.astype(BF16)

    for layer in range(depth):
        shift_m, scale_m, gate_m, shift_f, scale_f, gate_f = (mod[layer, :, :, i] for i in range(6))
        gains = norm_gains[layer]
        j = layer // 2
        if layer % 2 == 0:
            qkv = _proj(_att_proj_kernel, xt, gains[0:1], shift_m, scale_m, att_tables,
                        att_w_qkv[j].astype(BF16), seq, "att_proj")
            sb = _sb_attention(qkv, batch, seq)
            mb = _moba_attention(qkv, batch, seq)
            w_o = att_w_o[j].astype(BF16)
            split = N_SB_HEADS * HEAD_DIM
            acts, w_outs = (sb, mb), (w_o[:split], w_o[split:])
        else:
            proj = _proj(_ret_proj_kernel, xt, gains[0:1], shift_m, scale_m, ret_tables,
                         ret_w_in[j].astype(BF16), seq, "ret_proj")
            acts = (_retention(proj, ret_gn[j].reshape(1, RET_V_W), batch, seq),)
            w_outs = (ret_w_o[j].astype(BF16),)
        xt = _mix_ffn(xt, gate_m, acts, w_outs, gains[1:2], shift_f, scale_f, gate_f, w_gate_up,
                      w_down, layer, final_norm.reshape(1, d), seq, layer == depth - 1,
                      "mix_ffn%d" % layer)
    return xt.reshape(batch, seq, d)
```

```python
import functools
import math

import jax
import jax.numpy as jnp
from jax import lax
from jax.experimental import pallas as pl
from jax.experimental.pallas import tpu as pltpu

F32 = jnp.float32
BF16 = jnp.bfloat16

D_MODEL = 1024
HEAD_DIM = 64
N_SB_HEADS = 8
N_MOBA_HEADS = 8
ATT_W = (N_SB_HEADS + N_MOBA_HEADS) * HEAD_DIM
MOBA_BLOCK = 256
MOBA_TOPK = 3
N_RET_HEADS = 4
RET_DK = D_MODEL // N_RET_HEADS
RET_DV = 2 * RET_DK
RET_QK_W = N_RET_HEADS * RET_DK
RET_V_W = N_RET_HEADS * RET_DV
D_FF = -(-8 * D_MODEL // 768) * 256
ROPE_THETA = 10000.0
NORM_EPS = 1e-6
GN_EPS = 1e-5
NEG = -1e30
Q_SCALE = HEAD_DIM ** -0.5 * math.log2(math.e)

LANES = 128
ROW_TILE = 512
COL_CHUNK = 512
MXU_WIDTH = 256
_FF_SPLIT = (D_FF // MXU_WIDTH + 1) // 2 * MXU_WIDTH
FF_CHUNKS = ((0, _FF_SPLIT), (_FF_SPLIT, D_FF))
ATT_TILE = 256
PAIRS_PER_STEP = 2
RET_CHUNK = 256
RET_SEQ_TILE = 1024
RET_HEADS_PER_STEP = 4
VMEM_LIMIT = 56 * 1024 * 1024


def _params(n_axes):
    return pltpu.CompilerParams(
        dimension_semantics=("arbitrary",) * n_axes, vmem_limit_bytes=VMEM_LIMIT)


def _resident(shape, index_map):
    return pl.BlockSpec(shape, index_map, pipeline_mode=pl.Buffered(1))


def _silu(v):
    return v * (1.0 / (1.0 + jnp.exp(-v)))


def _ada_kernel(c_ref, w_ref, b_ref, o_ref):
    o_ref[0] = jnp.dot(_silu(c_ref[...]), w_ref[0], preferred_element_type=F32) + b_ref[0]


def _ada_mod(c, ada_w, ada_b):
    depth, d, n = ada_w.shape
    cp = jnp.pad(c, ((0, 8 - c.shape[0]), (0, 0)))
    tn = 1536
    return pl.pallas_call(
        _ada_kernel,
        grid=(depth, n // tn),
        in_specs=[
            pl.BlockSpec((8, d), lambda l, j: (0, 0)),
            pl.BlockSpec((1, d, tn), lambda l, j: (l, 0, j)),
            pl.BlockSpec((1, 1, tn), lambda l, j: (l, 0, j)),
        ],
        out_specs=pl.BlockSpec((1, 8, tn), lambda l, j: (l, 0, j)),
        out_shape=jax.ShapeDtypeStruct((depth, 8, n), F32),
        compiler_params=_params(2),
        name="ada_mod",
    )(cp, ada_w, ada_b.reshape(depth, 1, n))


def _norm_mod(x, g, shift, scale):
    y = x * lax.rsqrt(jnp.mean(x * x, axis=-1, keepdims=True) + NORM_EPS)
    return (y * g) * (1.0 + scale) + shift


def _att_proj_kernel(x_ref, g_ref, sh_ref, sc_ref, cos_ref, sina_ref, sinb_ref, w_ref, o_ref):
    h = _norm_mod(x_ref[...], g_ref[...], sh_ref[0], sc_ref[0]).astype(BF16)
    cos, sina, sinb = cos_ref[...], sina_ref[...], sinb_ref[...]
    sb_w = N_SB_HEADS * HEAD_DIM
    for c in range(3 * ATT_W // COL_CHUNK):
        lo = c * COL_CHUNK
        y = jnp.dot(h, w_ref[:, lo:lo + COL_CHUNK], preferred_element_type=F32)
        is_q = lo < ATT_W
        rotated = (lo % ATT_W) >= sb_w and lo < 2 * ATT_W
        for s in range(COL_CHUNK // LANES):
            ys = y[:, s * LANES:(s + 1) * LANES]
            if rotated:
                ys = (ys * cos + pltpu.roll(ys, LANES - HEAD_DIM // 2, 1) * sina
                      + pltpu.roll(ys, HEAD_DIM // 2, 1) * sinb)
            if is_q:
                ys = ys * Q_SCALE
            o_ref[:, lo + s * LANES:lo + (s + 1) * LANES] = ys.astype(BF16)


def _ret_proj_kernel(x_ref, g_ref, sh_ref, sc_ref, cos_ref, sin_ref, w_ref, o_ref):
    h = _norm_mod(x_ref[...], g_ref[...], sh_ref[0], sc_ref[0]).astype(BF16)
    cos, sin = cos_ref[...], sin_ref[...]
    n = 2 * RET_QK_W + 2 * RET_V_W
    for c in range(n // COL_CHUNK):
        lo = c * COL_CHUNK
        y = jnp.dot(h, w_ref[:, lo:lo + COL_CHUNK], preferred_element_type=F32)
        if lo < 2 * RET_QK_W:
            mul = 1.0 if lo < RET_QK_W else RET_DK ** -0.5
            half = RET_DK // 2
            for hd in range(COL_CHUNK // RET_DK):
                x1 = y[:, hd * RET_DK:hd * RET_DK + half]
                x2 = y[:, hd * RET_DK + half:(hd + 1) * RET_DK]
                o1 = (x1 * cos - x2 * sin) * mul
                o2 = (x2 * cos + x1 * sin) * mul
                o_ref[:, lo + hd * RET_DK:lo + hd * RET_DK + half] = o1.astype(BF16)
                o_ref[:, lo + hd * RET_DK + half:lo + (hd + 1) * RET_DK] = o2.astype(BF16)
        else:
            o_ref[:, lo:lo + COL_CHUNK] = y.astype(BF16)


def _proj(kernel, x, gain, shift, scale, tables, w, seq, name):
    t, d = x.shape
    n = w.shape[1]
    tm = ROW_TILE
    per_seq = seq // tm
    row = lambda i: (i, 0)
    batch = lambda i: (i // per_seq, 0, 0)
    pos = lambda i: (i % per_seq, 0)
    return pl.pallas_call(
        kernel,
        grid=(t // tm,),
        in_specs=[
            pl.BlockSpec((tm, d), row),
            pl.BlockSpec((1, d), lambda i: (0, 0)),
            pl.BlockSpec((1, 1, d), batch),
            pl.BlockSpec((1, 1, d), batch),
            *[pl.BlockSpec((tm, LANES), pos) for _ in tables],
            _resident((d, n), lambda i: (0, 0)),
        ],
        out_specs=pl.BlockSpec((tm, n), row),
        out_shape=jax.ShapeDtypeStruct((t, n), BF16),
        compiler_params=_params(1),
        name=name,
    )(x, gain, shift, scale, *tables, w)


def _pair_lanes(p):
    return slice(p * LANES, (p + 1) * LANES)


def _fill_vt(v_ref, vt_ref):
    t = ATT_TILE
    for p in range(v_ref.shape[1] // LANES):
        for n in range(v_ref.shape[0] // t):
            vt_ref[p, n] = v_ref[n * t:(n + 1) * t, _pair_lanes(p)].T


def _tile_masks(strict):
    t = ATT_TILE
    key = lax.broadcasted_iota(jnp.int32, (t, 2 * t), 0)
    qry = lax.broadcasted_iota(jnp.int32, (t, 2 * t), 1) % t
    chan0 = lax.broadcasted_iota(jnp.int32, (LANES, 1), 0) < HEAD_DIM
    return (key < qry) if strict else (key <= qry), chan0


def _load_super_tile(q_ref, p, i, chan0):
    t = ATT_TILE
    base = pl.multiple_of(i * 2 * t, 2 * t)
    cells = []
    for c in range(2):
        q_t = q_ref[pl.ds(base + c * t, t), _pair_lanes(p)].T
        zero = jnp.zeros_like(q_t)
        cells += [jnp.where(chan0, q_t, zero), jnp.where(chan0, zero, q_t)]
    return jnp.concatenate(cells, axis=1), jnp.concatenate(cells[2:], axis=1)


def _store_super_tile(o_ref, p, i, acc_t, chan0):
    t = ATT_TILE
    base = pl.multiple_of(i * 2 * t, 2 * t)
    for cell in range(2):
        lo = cell * 2 * t
        o_t = jnp.where(chan0, acc_t[:, lo:lo + t], acc_t[:, lo + t:lo + 2 * t])
        o_ref[pl.ds(base + cell * t, t), _pair_lanes(p)] = o_t.T.astype(BF16)


def _scores(k_ref, p, j, q_cols):
    t = ATT_TILE
    return jnp.dot(k_ref[pl.ds(pl.multiple_of(j * t, t), t), _pair_lanes(p)], q_cols,
                   preferred_element_type=F32)


def _sb_kernel(q_ref, k_ref, v_ref, o_ref, vt_ref, acc_ref, z_ref):
    t = ATT_TILE
    w = 2 * t
    seq = q_ref.shape[0]
    pairs = range(q_ref.shape[1] // LANES)
    per_half = 2 * len(pairs)
    strict, chan0 = _tile_masks(True)
    from_key = -(lax.broadcasted_iota(jnp.int32, (t, t), 1)
                 >= lax.broadcasted_iota(jnp.int32, (t, t), 0)).astype(BF16)
    _fill_vt(v_ref, vt_ref)

    def block(p, j, slot, lanes, run, mask):
        z = z_ref[slot, :, lanes]
        neg_l = jnp.maximum(z, 0.0) + jnp.log2(1.0 + jnp.exp2(-jnp.abs(z)))
        if mask is not None:
            neg_l = jnp.where(mask, neg_l, 0.0)
        since = jnp.dot(from_key, neg_l.astype(BF16), preferred_element_type=F32)
        a = jnp.exp2(z_ref[slot, :, lanes] + since)
        if mask is not None:
            a = jnp.where(mask, a, 0.0)
        part = jnp.dot(vt_ref[p, j], a.astype(BF16), preferred_element_type=F32)
        return run + since[0:1, :], part * jnp.exp2(run)

    def stage_pair_of_blocks(half, j, tiles):
        for p in pairs:
            for b in range(2):
                z_ref[half + 2 * p + b] = _scores(k_ref, p, j + b, tiles[p][0])

    def q_super(i, _):
        tiles = [_load_super_tile(q_ref, p, i, chan0) for p in pairs]
        mask = jnp.concatenate([strict, jnp.ones((t, w), jnp.bool_)], axis=1)
        cell1 = slice(w, 2 * w)
        diag = lax.rem(i, 2) * per_half
        for p in pairs:
            q2, qb = tiles[p]
            z_ref[diag + 2 * p + 1, :, cell1] = _scores(k_ref, p, 2 * i + 1, qb)
            z_ref[diag + 2 * p] = _scores(k_ref, p, 2 * i, q2)
        runs = []
        for p in pairs:
            run_b, part = block(p, 2 * i + 1, diag + 2 * p + 1, cell1, jnp.zeros((1, w), F32),
                                strict)
            acc_ref[p, :, w:] = part
            run = jnp.concatenate([jnp.zeros((1, w), F32), run_b], axis=1)
            run, part = block(p, 2 * i, diag + 2 * p, slice(None), run, mask)
            acc_ref[p, :, :w] = part[:, :w]
            acc_ref[p, :, w:] += part[:, w:]
            runs.append(run)

        def past(s, runs):
            j = 2 * (i - 1 - s)
            half = lax.rem(s, 2) * per_half
            stage_pair_of_blocks(half, j, tiles)
            out = []
            for p in pairs:
                run, part_b = block(p, j + 1, half + 2 * p + 1, slice(None), runs[p], None)
                run, part_a = block(p, j, half + 2 * p, slice(None), run, None)
                acc_ref[p] += part_b + part_a
                out.append(run)
            return tuple(out)

        lax.fori_loop(0, i, past, tuple(runs))
        for p in pairs:
            _store_super_tile(o_ref, p, i, acc_ref[p], chan0)
        return 0

    lax.fori_loop(0, seq // w, q_super, 0)


def _pair_specs(seq, q_col, k_col, v_col):
    width = PAIRS_PER_STEP * LANES
    return [
        pl.BlockSpec((seq, width), lambda b, p: (b, q_col + p)),
        pl.BlockSpec((seq, width), lambda b, p: (b, k_col + p)),
        pl.BlockSpec((seq, width), lambda b, p: (b, v_col + p)),
    ]


def _with_weight_casts(kernel, n_cast):
    def wrapped(q_ref, k_ref, v_ref, *refs):
        srcs, o_ref, dsts = refs[:n_cast], refs[n_cast], refs[n_cast + 1:2 * n_cast + 1]
        for src, dst in zip(srcs, dsts):
            dst[...] = src[...].astype(BF16)
        kernel(q_ref, k_ref, v_ref, o_ref, *refs[2 * n_cast + 1:])
    return wrapped


def _attention_call(kernel, qkv, weights, batch, seq, n_heads, first_head, extra_scratch, name):
    width = PAIRS_PER_STEP * LANES
    groups = n_heads * HEAD_DIM // width
    third = ATT_W // width
    first = first_head * HEAD_DIM // width
    steps = batch * groups
    w_specs = [pl.BlockSpec((w.shape[0] // steps, w.shape[1]), lambda b, p: (b * groups + p, 0))
               for w in weights]
    out, *cast = pl.pallas_call(
        _with_weight_casts(kernel, len(weights)),
        grid=(batch, groups),
        in_specs=_pair_specs(seq, first, third + first, 2 * third + first) + w_specs,
        out_specs=[pl.BlockSpec((seq, width), lambda b, p: (b, p))] + w_specs,
        out_shape=[jax.ShapeDtypeStruct((batch * seq, groups * width), BF16)]
        + [jax.ShapeDtypeStruct(w.shape, BF16) for w in weights],
        scratch_shapes=[pltpu.VMEM((PAIRS_PER_STEP, seq // ATT_TILE, LANES, ATT_TILE), BF16),
                        pltpu.VMEM((PAIRS_PER_STEP, LANES, 4 * ATT_TILE), F32),
                        *extra_scratch],
        compiler_params=_params(2),
        name=name,
    )(qkv, qkv, qkv, *weights)
    return out, cast


def _sb_attention(qkv, weights, batch, seq):
    staged = pltpu.VMEM((4 * PAIRS_PER_STEP, ATT_TILE, 4 * ATT_TILE), F32)
    return _attention_call(_sb_kernel, qkv, weights, batch, seq, N_SB_HEADS, 0, [staged],
                           "sb_attention")


def _moba_kernel(q_ref, k_ref, v_ref, o_ref, vt_ref, acc_ref, sel_ref, s_ref):
    t = ATT_TILE
    w = 2 * t
    seq = q_ref.shape[0]
    nb = seq // MOBA_BLOCK
    pairs = range(q_ref.shape[1] // LANES)
    per_half = 2 * len(pairs)
    causal, chan0 = _tile_masks(False)
    blk = lax.broadcasted_iota(jnp.int32, (nb, 2 * w), 0)
    cell = lax.broadcasted_iota(jnp.int32, (nb, 2 * w), 1) // w
    blk_f = blk.astype(F32)
    kmean = [(jnp.sum(k_ref[:, _pair_lanes(p)].astype(F32).reshape(nb, MOBA_BLOCK, LANES), axis=1)
              * (1.0 / MOBA_BLOCK)).astype(BF16) for p in pairs]
    _fill_vt(v_ref, vt_ref)


    def select(p, i, q2):
        gate = jnp.dot(kmean[p], q2, preferred_element_type=F32)
        past_blk = blk < 2 * i + cell
        gate = jnp.where(past_blk, gate, NEG)
        sel = jnp.zeros((nb, 2 * w), F32)
        for _r in range(MOBA_TOPK):
            top = jnp.max(gate, axis=0, keepdims=True)
            idx = jnp.min(jnp.where(gate == top, blk_f, float(nb)), axis=0, keepdims=True)
            hit = blk_f == idx
            sel = jnp.where(hit & past_blk, 1.0, sel)
            gate = jnp.where(hit, -jnp.inf, gate)
        sel_ref[p] = sel

    def own_blocks(p, i, raw_a, raw_b):
        s = jnp.where(causal, raw_b, NEG)
        m_b = jnp.max(s, axis=0, keepdims=True)
        pr = jnp.exp2(s - m_b)
        acc_ref[p, :, :w] = jnp.zeros((LANES, w), F32)
        acc_ref[p, :, w:] = jnp.dot(vt_ref[p, 2 * i + 1], pr.astype(BF16),
                                    preferred_element_type=F32)
        m = jnp.concatenate([jnp.full((1, w), -jnp.inf, F32), m_b], axis=1)
        l = jnp.concatenate([jnp.zeros((1, w), F32), jnp.sum(pr, axis=0, keepdims=True)], axis=1)
        keep = jnp.concatenate(
            [causal, jnp.broadcast_to(sel_ref[p, pl.ds(2 * i, 1), w:] > 0.0, (t, w))], axis=1)
        s = jnp.where(keep, raw_a, NEG)
        m_new = jnp.maximum(m, jnp.max(s, axis=0, keepdims=True))
        alpha = jnp.exp2(m - m_new)
        pr = jnp.exp2(s - m_new)
        acc_ref[p] = alpha * acc_ref[p] + jnp.dot(
            vt_ref[p, 2 * i], pr.astype(BF16), preferred_element_type=F32)
        return m_new, alpha * l + jnp.sum(pr, axis=0, keepdims=True)

    def past_pair(p, j, raw_a, raw_b, m_old, l_old):
        s_a = jnp.where(sel_ref[p, pl.ds(j, 1), :] > 0.0, raw_a, NEG)
        s_b = jnp.where(sel_ref[p, pl.ds(j + 1, 1), :] > 0.0, raw_b, NEG)
        m_new = jnp.maximum(m_old, jnp.maximum(jnp.max(s_a, axis=0, keepdims=True),
                                               jnp.max(s_b, axis=0, keepdims=True)))
        alpha = jnp.exp2(m_old - m_new)
        p_a = jnp.exp2(s_a - m_new)
        p_b = jnp.exp2(s_b - m_new)
        part = (jnp.dot(vt_ref[p, j], p_a.astype(BF16), preferred_element_type=F32)
                + jnp.dot(vt_ref[p, j + 1], p_b.astype(BF16), preferred_element_type=F32))
        acc_ref[p] = alpha * acc_ref[p] + part
        l_new = (alpha * l_old + jnp.sum(p_a, axis=0, keepdims=True)
                 + jnp.sum(p_b, axis=0, keepdims=True))
        return m_new, l_new

    def stage_pair_of_blocks(half, j, tiles):
        for p in pairs:
            for b in range(2):
                s_ref[half + 2 * p + b] = _scores(k_ref, p, j + b, tiles[p][0])

    def q_super(i, _):
        tiles = [_load_super_tile(q_ref, p, i, chan0) for p in pairs]
        own = lax.rem(i, 2) * per_half
        for p in pairs:
            q2, qb = tiles[p]
            s_ref[own + 2 * p + 1, :, w:] = _scores(k_ref, p, 2 * i + 1, qb)
            s_ref[own + 2 * p] = _scores(k_ref, p, 2 * i, q2)
            select(p, i, q2)
        stats = tuple(own_blocks(p, i, s_ref[own + 2 * p], s_ref[own + 2 * p + 1, :, w:])
                      for p in pairs)

        def past(jj, stats):
            half = lax.rem(jj, 2) * per_half
            stage_pair_of_blocks(half, 2 * jj, tiles)
            return tuple(past_pair(p, 2 * jj, s_ref[half + 2 * p], s_ref[half + 2 * p + 1],
                                   *stats[p]) for p in pairs)

        stats = lax.fori_loop(0, i, past, stats)
        for p in pairs:
            _store_super_tile(o_ref, p, i, acc_ref[p] / stats[p][1], chan0)
        return 0

    lax.fori_loop(0, seq // w, q_super, 0)


def _moba_attention(qkv, weights, batch, seq):
    sel = pltpu.VMEM((PAIRS_PER_STEP, seq // MOBA_BLOCK, 4 * ATT_TILE), F32)
    staged = pltpu.VMEM((4 * PAIRS_PER_STEP, ATT_TILE, 4 * ATT_TILE), F32)
    return _attention_call(_moba_kernel, qkv, weights, batch, seq, N_MOBA_HEADS, N_SB_HEADS,
                           [sel, staged], "moba_attention")


def _ret_kernel(q_ref, k_ref, v_ref, g_ref, gn_ref, o_ref, state_ref, decay_ref):
    c = RET_CHUNK
    heads = range(RET_HEADS_PER_STEP)
    n_row = lax.broadcasted_iota(jnp.int32, (c, 1), 0).astype(F32)
    log_g, xi, zeta, chunk_decay = [], [], [], []
    for hh in heads:
        head = (pl.program_id(1) * RET_HEADS_PER_STEP + hh + 5).astype(F32)
        lg = jnp.log(1.0 - jnp.exp2(-jnp.full((1, 1), head, F32)))
        log_g.append(lg)
        xi.append(jnp.exp(lg * (n_row + 1.0)))
        zeta.append(jnp.exp(lg * (c - 1.0 - n_row)))
        chunk_decay.append(jnp.exp(lg * c))

    @pl.when(pl.program_id(2) == 0)
    def _():
        diff = (lax.broadcasted_iota(jnp.int32, (c, c), 0)
                - lax.broadcasted_iota(jnp.int32, (c, c), 1)).astype(F32)
        for hh in heads:
            decay_ref[hh] = jnp.where(diff >= 0, jnp.exp(log_g[hh] * jnp.maximum(diff, 0.0)), 0.0)
        state_ref[...] = jnp.zeros_like(state_ref)

    def chunk(i, _):
        rows = pl.ds(pl.multiple_of(i * c, c), c)
        for hh in heads:
            qk_cols = slice(hh * RET_DK, (hh + 1) * RET_DK)
            v_cols = slice(hh * RET_DV, (hh + 1) * RET_DV)
            q, k, v = q_ref[rows, qk_cols], k_ref[rows, qk_cols], v_ref[rows, v_cols]
            inner = lax.dot_general(q, k, (((1,), (1,)), ((), ())),
                                    preferred_element_type=F32) * decay_ref[hh]
            state = state_ref[hh]
            o = (jnp.dot(inner.astype(BF16), v, preferred_element_type=F32)
                 + jnp.dot(q, state.astype(BF16), preferred_element_type=F32) * xi[hh])
            kz = (k.astype(F32) * zeta[hh]).astype(BF16)
            state_ref[hh] = state * chunk_decay[hh] + lax.dot_general(
                kz, v, (((0,), (0,)), ((), ())), preferred_element_type=F32)
            mu = jnp.mean(o, axis=-1, keepdims=True)
            var = jnp.mean(jnp.square(o - mu), axis=-1, keepdims=True)
            on = (o - mu) * lax.rsqrt(var + GN_EPS) * gn_ref[:, v_cols]
            o_ref[rows, v_cols] = (_silu(g_ref[rows, v_cols].astype(F32)) * on).astype(BF16)
        return 0

    lax.fori_loop(0, q_ref.shape[0] // c, chunk, 0)


def _retention(proj, gn_gain, batch, seq):
    hps = RET_HEADS_PER_STEP
    tile = RET_SEQ_TILE
    tiles = seq // tile
    qk_w, v_w = hps * RET_DK, hps * RET_DV
    k0 = RET_QK_W // qk_w
    v0 = 2 * RET_QK_W // v_w
    g0 = v0 + RET_V_W // v_w
    row = lambda b, h, s: b * tiles + s
    return pl.pallas_call(
        _ret_kernel,
        grid=(batch, N_RET_HEADS // hps, tiles),
        in_specs=[
            pl.BlockSpec((tile, qk_w), lambda b, h, s: (row(b, h, s), h)),
            pl.BlockSpec((tile, qk_w), lambda b, h, s: (row(b, h, s), k0 + h)),
            pl.BlockSpec((tile, v_w), lambda b, h, s: (row(b, h, s), v0 + h)),
            pl.BlockSpec((tile, v_w), lambda b, h, s: (row(b, h, s), g0 + h)),
            pl.BlockSpec((1, v_w), lambda b, h, s: (0, h)),
        ],
        out_specs=pl.BlockSpec((tile, v_w), lambda b, h, s: (row(b, h, s), h)),
        out_shape=jax.ShapeDtypeStruct((batch * seq, RET_V_W), BF16),
        scratch_shapes=[pltpu.VMEM((hps, RET_DK, RET_DV), F32),
                        pltpu.VMEM((hps, RET_CHUNK, RET_CHUNK), F32)],
        compiler_params=_params(3),
        name="retention",
    )(proj, proj, proj, proj, gn_gain)


def _mix_ffn_kernel(*refs, n_in, final_norm):
    x_ref, gate_m_ref, g_ref, sh_ref, sc_ref, gate_f_ref = refs[:6]
    a_refs, w_refs = refs[6:6 + n_in], refs[6 + n_in:6 + 2 * n_in]
    wgu_ref, wd_ref, fin_ref, o_ref = refs[6 + 2 * n_in:]
    y = jnp.dot(a_refs[0][...], w_refs[0][...], preferred_element_type=F32)
    for a_ref, w_ref in zip(a_refs[1:], w_refs[1:]):
        y += jnp.dot(a_ref[...], w_ref[...], preferred_element_type=F32)
    x = x_ref[...] + gate_m_ref[0] * y
    h = _norm_mod(x, g_ref[...], sh_ref[0], sc_ref[0]).astype(BF16)
    y = None
    for lo, hi in FF_CHUNKS:
        gt = jnp.dot(h, wgu_ref[:, lo:hi], preferred_element_type=F32)
        up = jnp.dot(h, wgu_ref[:, D_FF + lo:D_FF + hi], preferred_element_type=F32)
        part = jnp.dot((_silu(gt) * up).astype(BF16), wd_ref[lo:hi, :],
                       preferred_element_type=F32)
        y = part if y is None else y + part
    out = x + gate_f_ref[0] * y
    if final_norm:
        out = out * lax.rsqrt(jnp.mean(out * out, axis=-1, keepdims=True) + NORM_EPS) * fin_ref[...]
    o_ref[...] = out


def _mix_ffn(x, gate_m, acts, w_outs, gain, shift, scale, gate_f, w_gate_up, w_down, layer,
             fin_gain, seq, final_norm, name):
    t, d = x.shape
    tm = ROW_TILE
    per_seq = seq // tm
    row = lambda i: (i, 0)
    batch = lambda i: (i // per_seq, 0, 0)
    const = lambda i: (0, 0)
    this_layer = lambda i: (layer, 0, 0)
    return pl.pallas_call(
        functools.partial(_mix_ffn_kernel, n_in=len(acts), final_norm=final_norm),
        grid=(t // tm,),
        in_specs=[
            pl.BlockSpec((tm, d), row),
            pl.BlockSpec((1, 1, d), batch),
            pl.BlockSpec((1, d), const),
            pl.BlockSpec((1, 1, d), batch),
            pl.BlockSpec((1, 1, d), batch),
            pl.BlockSpec((1, 1, d), batch),
            *[pl.BlockSpec((tm, a.shape[1]), row) for a in acts],
            *[_resident(w.shape, const) for w in w_outs],
            _resident((None,) + w_gate_up.shape[1:], this_layer),
            _resident((None,) + w_down.shape[1:], this_layer),
            pl.BlockSpec((1, d), const),
        ],
        out_specs=pl.BlockSpec((tm, d), row),
        out_shape=jax.ShapeDtypeStruct((t, d), F32),
        compiler_params=_params(1),
        name=name,
    )(x, gate_m, gain, shift, scale, gate_f, *acts, *w_outs, w_gate_up, w_down, fin_gain)


def _rope_tables(seq, dim, tile):
    inv = ROPE_THETA ** (-jnp.arange(0, dim, 2, dtype=F32) / dim)
    ang = jnp.arange(seq, dtype=jnp.int32).astype(F32)[:, None] * inv[None, :]
    return jnp.tile(jnp.cos(ang), (1, tile)), jnp.tile(jnp.sin(ang), (1, tile))


def kernel(x, c, ada_w, ada_b, norm_gains, att_w_qkv, att_w_o, ret_w_in, ret_gn, ret_w_o,
           ffn_w_gate_up, ffn_w_down, final_norm):
    batch, seq, d = x.shape
    depth = ada_w.shape[0]
    assert d == D_MODEL and batch <= 8 and seq % ROW_TILE == 0
    assert seq % (2 * ATT_TILE) == 0 and seq % RET_SEQ_TILE == 0

    mod = _ada_mod(c, ada_w, ada_b)[:, :batch].reshape(depth, batch, 1, 6, d)
    xt = x.reshape(batch * seq, d)

    half = HEAD_DIM // 2
    cos_a, sin_a = _rope_tables(seq, HEAD_DIM, LANES // half)
    first_half = (jnp.arange(LANES) % HEAD_DIM) < half
    att_tables = (cos_a, jnp.where(first_half, -sin_a, 0.0), jnp.where(first_half, 0.0, sin_a))
    ret_tables = _rope_tables(seq, RET_DK, 1)
    assert depth == 2 and att_w_qkv.shape[0] == 1 and ret_w_in.shape[0] == 1
    rows = lambda w: w.reshape(-1, w.shape[-1])

    for layer in range(depth):
        shift_m, scale_m, gate_m, shift_f, scale_f, gate_f = (mod[layer, :, :, i] for i in range(6))
        gains = norm_gains[layer]
        if layer % 2 == 0:
            qkv = _proj(_att_proj_kernel, xt, gains[0:1], shift_m, scale_m, att_tables,
                        att_w_qkv[0].astype(BF16), seq, "att_proj")
            sb, (w_gate_up, w_ret_o, w_att_o) = _sb_attention(
                qkv, [rows(ffn_w_gate_up), rows(ret_w_o), rows(att_w_o)], batch, seq)
            mb, (w_down, w_ret_in) = _moba_attention(
                qkv, [rows(ffn_w_down), rows(ret_w_in)], batch, seq)
            w_gate_up = w_gate_up.reshape(ffn_w_gate_up.shape)
            w_down = w_down.reshape(ffn_w_down.shape)
            split = N_SB_HEADS * HEAD_DIM
            acts, w_outs = (sb, mb), (w_att_o[:split], w_att_o[split:])
        else:
            proj = _proj(_ret_proj_kernel, xt, gains[0:1], shift_m, scale_m, ret_tables,
                         w_ret_in, seq, "ret_proj")
            acts = (_retention(proj, ret_gn[0].reshape(1, RET_V_W), batch, seq),)
            w_outs = (w_ret_o,)
        xt = _mix_ffn(xt, gate_m, acts, w_outs, gains[1:2], shift_f, scale_f, gate_f, w_gate_up,
                      w_down, layer, final_norm.reshape(1, d), seq, layer == depth - 1,
                      "mix_ffn%d" % layer)
    return xt.reshape(batch, seq, d)
```

```python
import functools
import math

import jax
import jax.numpy as jnp
from jax import lax
from jax.experimental import pallas as pl
from jax.experimental.pallas import tpu as pltpu

F32 = jnp.float32
BF16 = jnp.bfloat16

D_MODEL = 1024
HEAD_DIM = 64
N_SB_HEADS = 8
N_MOBA_HEADS = 8
ATT_W = (N_SB_HEADS + N_MOBA_HEADS) * HEAD_DIM
MOBA_BLOCK = 256
MOBA_TOPK = 3
N_RET_HEADS = 4
RET_DK = D_MODEL // N_RET_HEADS
RET_DV = 2 * RET_DK
RET_QK_W = N_RET_HEADS * RET_DK
RET_V_W = N_RET_HEADS * RET_DV
D_FF = -(-8 * D_MODEL // 768) * 256
ROPE_THETA = 10000.0
ROPE_SIDE = 64
NORM_EPS = 1e-6
GN_EPS = 1e-5
NEG = -1e30
Q_SCALE = HEAD_DIM ** -0.5 * math.log2(math.e)

LANES = 128
ROW_TILE = 512
PROJ_SLABS = 2
COL_CHUNK = 512
MXU_WIDTH = 256
_FF_SPLIT = (D_FF // MXU_WIDTH + 1) // 2 * MXU_WIDTH
FF_CHUNKS = ((0, _FF_SPLIT), (_FF_SPLIT, D_FF))
ATT_TILE = 256
PAIRS_PER_STEP = 2
RET_CHUNK = 256
RET_SEQ_TILE = 1024
RET_HEADS_PER_STEP = 4
VMEM_LIMIT = 56 * 1024 * 1024


def _params(n_axes):
    return pltpu.CompilerParams(
        dimension_semantics=("arbitrary",) * n_axes, vmem_limit_bytes=VMEM_LIMIT)


def _resident(shape, index_map):
    return pl.BlockSpec(shape, index_map, pipeline_mode=pl.Buffered(1))


def _silu(v):
    return v * (1.0 / (1.0 + jnp.exp(-v)))


def _ada_kernel(c_ref, w_ref, b_ref, o_ref):
    o_ref[0] = jnp.dot(_silu(c_ref[...]), w_ref[0], preferred_element_type=F32) + b_ref[0]


def _ada_mod(c, ada_w, ada_b):
    depth, d, n = ada_w.shape
    cp = jnp.pad(c, ((0, 8 - c.shape[0]), (0, 0)))
    tn = 1536
    return pl.pallas_call(
        _ada_kernel,
        grid=(depth, n // tn),
        in_specs=[
            pl.BlockSpec((8, d), lambda l, j: (0, 0)),
            pl.BlockSpec((1, d, tn), lambda l, j: (l, 0, j)),
            pl.BlockSpec((1, 1, tn), lambda l, j: (l, 0, j)),
        ],
        out_specs=pl.BlockSpec((1, 8, tn), lambda l, j: (l, 0, j)),
        out_shape=jax.ShapeDtypeStruct((depth, 8, n), F32),
        compiler_params=_params(2),
        name="ada_mod",
    )(cp, ada_w, ada_b.reshape(depth, 1, n))


def _norm_mod(x, g, shift, scale):
    y = x * lax.rsqrt(jnp.mean(x * x, axis=-1, keepdims=True) + NORM_EPS)
    return (y * g) * (1.0 + scale) + shift


def _att_proj_kernel(x_ref, g_ref, sh_ref, sc_ref, cos_ref, sina_ref, sinb_ref, w_ref, o_ref):
    sb_w = N_SB_HEADS * HEAD_DIM
    for r in range(x_ref.shape[0] // ROW_TILE):
        rows = slice(r * ROW_TILE, (r + 1) * ROW_TILE)
        h = _norm_mod(x_ref[rows, :], g_ref[...], sh_ref[0], sc_ref[0]).astype(BF16)
        cos, sina, sinb = cos_ref[rows, :], sina_ref[rows, :], sinb_ref[rows, :]
        for c in range(3 * ATT_W // COL_CHUNK):
            lo = c * COL_CHUNK
            y = jnp.dot(h, w_ref[:, lo:lo + COL_CHUNK], preferred_element_type=F32)
            is_q = lo < ATT_W
            rotated = (lo % ATT_W) >= sb_w and lo < 2 * ATT_W
            for s in range(COL_CHUNK // LANES):
                ys = y[:, s * LANES:(s + 1) * LANES]
                if rotated:
                    ys = (ys * cos + pltpu.roll(ys, LANES - HEAD_DIM // 2, 1) * sina
                          + pltpu.roll(ys, HEAD_DIM // 2, 1) * sinb)
                if is_q:
                    ys = ys * Q_SCALE
                o_ref[rows, lo + s * LANES:lo + (s + 1) * LANES] = ys.astype(BF16)


def _ret_proj_kernel(x_ref, g_ref, sh_ref, sc_ref, cos_ref, sin_ref, w_ref, o_ref):
    n = 2 * RET_QK_W + 2 * RET_V_W
    half = RET_DK // 2
    for r in range(x_ref.shape[0] // ROW_TILE):
        rows = slice(r * ROW_TILE, (r + 1) * ROW_TILE)
        h = _norm_mod(x_ref[rows, :], g_ref[...], sh_ref[0], sc_ref[0]).astype(BF16)
        cos, sin = cos_ref[rows, :], sin_ref[rows, :]
        for c in range(n // COL_CHUNK):
            lo = c * COL_CHUNK
            y = jnp.dot(h, w_ref[:, lo:lo + COL_CHUNK], preferred_element_type=F32)
            if lo < 2 * RET_QK_W:
                mul = 1.0 if lo < RET_QK_W else RET_DK ** -0.5
                for hd in range(COL_CHUNK // RET_DK):
                    x1 = y[:, hd * RET_DK:hd * RET_DK + half]
                    x2 = y[:, hd * RET_DK + half:(hd + 1) * RET_DK]
                    o1 = (x1 * cos - x2 * sin) * mul
                    o2 = (x2 * cos + x1 * sin) * mul
                    o_ref[rows, lo + hd * RET_DK:lo + hd * RET_DK + half] = o1.astype(BF16)
                    o_ref[rows, lo + hd * RET_DK + half:lo + (hd + 1) * RET_DK] = o2.astype(BF16)
            else:
                o_ref[rows, lo:lo + COL_CHUNK] = y.astype(BF16)


def _proj(kernel, x, gain, shift, scale, tables, w, seq, name):
    t, d = x.shape
    n = w.shape[1]
    tm = PROJ_SLABS * ROW_TILE
    per_seq = seq // tm
    row = lambda i: (i, 0)
    batch = lambda i: (i // per_seq, 0, 0)
    pos = lambda i: (i % per_seq, 0)
    return pl.pallas_call(
        kernel,
        grid=(t // tm,),
        in_specs=[
            pl.BlockSpec((tm, d), row),
            pl.BlockSpec((1, d), lambda i: (0, 0)),
            pl.BlockSpec((1, 1, d), batch),
            pl.BlockSpec((1, 1, d), batch),
            *[pl.BlockSpec((tm, LANES), pos) for _ in tables],
            _resident((d, n), lambda i: (0, 0)),
        ],
        out_specs=pl.BlockSpec((tm, n), row),
        out_shape=jax.ShapeDtypeStruct((t, n), BF16),
        compiler_params=_params(1),
        name=name,
    )(x, gain, shift, scale, *tables, w)


def _pair_lanes(p):
    return slice(p * LANES, (p + 1) * LANES)


def _fill_vt(v_ref, vt_ref):
    t = ATT_TILE
    for p in range(v_ref.shape[1] // LANES):
        for n in range(v_ref.shape[0] // t):
            vt_ref[p, n] = v_ref[n * t:(n + 1) * t, _pair_lanes(p)].T


def _tile_masks(strict):
    t = ATT_TILE
    key = lax.broadcasted_iota(jnp.int32, (t, 2 * t), 0)
    qry = lax.broadcasted_iota(jnp.int32, (t, 2 * t), 1) % t
    chan0 = lax.broadcasted_iota(jnp.int32, (LANES, 1), 0) < HEAD_DIM
    return (key < qry) if strict else (key <= qry), chan0


def _load_super_tile(q_ref, p, i, chan0):
    t = ATT_TILE
    base = pl.multiple_of(i * 2 * t, 2 * t)
    cells = []
    for c in range(2):
        q_t = q_ref[pl.ds(base + c * t, t), _pair_lanes(p)].T
        zero = jnp.zeros_like(q_t)
        cells += [jnp.where(chan0, q_t, zero), jnp.where(chan0, zero, q_t)]
    return jnp.concatenate(cells, axis=1), jnp.concatenate(cells[2:], axis=1)


def _store_super_tile(o_ref, p, i, acc_t, chan0):
    t = ATT_TILE
    base = pl.multiple_of(i * 2 * t, 2 * t)
    for cell in range(2):
        lo = cell * 2 * t
        o_t = jnp.where(chan0, acc_t[:, lo:lo + t], acc_t[:, lo + t:lo + 2 * t])
        o_ref[pl.ds(base + cell * t, t), _pair_lanes(p)] = o_t.T.astype(BF16)


def _scores(k_ref, p, j, q_cols):
    t = ATT_TILE
    return jnp.dot(k_ref[pl.ds(pl.multiple_of(j * t, t), t), _pair_lanes(p)], q_cols,
                   preferred_element_type=F32)


def _sb_kernel(q_ref, k_ref, v_ref, o_ref, vt_ref, acc_ref, z_ref):
    t = ATT_TILE
    w = 2 * t
    seq = q_ref.shape[0]
    pairs = range(q_ref.shape[1] // LANES)
    per_half = 2 * len(pairs)
    strict, chan0 = _tile_masks(True)
    from_key = -(lax.broadcasted_iota(jnp.int32, (t, t), 1)
                 >= lax.broadcasted_iota(jnp.int32, (t, t), 0)).astype(BF16)
    _fill_vt(v_ref, vt_ref)

    def block(p, j, slot, lanes, run, mask):
        z = z_ref[slot, :, lanes]
        neg_l = jnp.maximum(z, 0.0) + jnp.log2(1.0 + jnp.exp2(-jnp.abs(z)))
        if mask is not None:
            neg_l = jnp.where(mask, neg_l, 0.0)
        since = jnp.dot(from_key, neg_l.astype(BF16), preferred_element_type=F32)
        a = jnp.exp2(z_ref[slot, :, lanes] + since)
        if mask is not None:
            a = jnp.where(mask, a, 0.0)
        part = jnp.dot(vt_ref[p, j], a.astype(BF16), preferred_element_type=F32)
        return run + since[0:1, :], part * jnp.exp2(run)

    def stage_pair_of_blocks(half, j, tiles):
        for p in pairs:
            for b in range(2):
                z_ref[half + 2 * p + b] = _scores(k_ref, p, j + b, tiles[p][0])

    def q_super(i, _):
        tiles = [_load_super_tile(q_ref, p, i, chan0) for p in pairs]
        mask = jnp.concatenate([strict, jnp.ones((t, w), jnp.bool_)], axis=1)
        cell1 = slice(w, 2 * w)
        diag = lax.rem(i, 2) * per_half
        for p in pairs:
            q2, qb = tiles[p]
            z_ref[diag + 2 * p + 1, :, cell1] = _scores(k_ref, p, 2 * i + 1, qb)
            z_ref[diag + 2 * p] = _scores(k_ref, p, 2 * i, q2)
        runs = []
        for p in pairs:
            run_b, part = block(p, 2 * i + 1, diag + 2 * p + 1, cell1, jnp.zeros((1, w), F32),
                                strict)
            acc_ref[p, :, w:] = part
            run = jnp.concatenate([jnp.zeros((1, w), F32), run_b], axis=1)
            run, part = block(p, 2 * i, diag + 2 * p, slice(None), run, mask)
            acc_ref[p, :, :w] = part[:, :w]
            acc_ref[p, :, w:] += part[:, w:]
            runs.append(run)

        def past(s, runs):
            j = 2 * (i - 1 - s)
            half = lax.rem(s, 2) * per_half
            stage_pair_of_blocks(half, j, tiles)
            out = []
            for p in pairs:
                run, part_b = block(p, j + 1, half + 2 * p + 1, slice(None), runs[p], None)
                run, part_a = block(p, j, half + 2 * p, slice(None), run, None)
                acc_ref[p] += part_b + part_a
                out.append(run)
            return tuple(out)

        lax.fori_loop(0, i, past, tuple(runs))
        for p in pairs:
            _store_super_tile(o_ref, p, i, acc_ref[p], chan0)
        return 0

    lax.fori_loop(0, seq // w, q_super, 0)


def _pair_specs(seq, q_col, k_col, v_col):
    width = PAIRS_PER_STEP * LANES
    return [
        pl.BlockSpec((seq, width), lambda b, p: (b, q_col + p)),
        pl.BlockSpec((seq, width), lambda b, p: (b, k_col + p)),
        pl.BlockSpec((seq, width), lambda b, p: (b, v_col + p)),
    ]


def _with_weight_casts(kernel, n_cast):
    def wrapped(q_ref, k_ref, v_ref, *refs):
        srcs, o_ref, dsts = refs[:n_cast], refs[n_cast], refs[n_cast + 1:2 * n_cast + 1]
        for src, dst in zip(srcs, dsts):
            dst[...] = src[...].astype(BF16)
        kernel(q_ref, k_ref, v_ref, o_ref, *refs[2 * n_cast + 1:])
    return wrapped


def _attention_call(kernel, qkv, weights, batch, seq, n_heads, first_head, extra_scratch, name):
    width = PAIRS_PER_STEP * LANES
    groups = n_heads * HEAD_DIM // width
    third = ATT_W // width
    first = first_head * HEAD_DIM // width
    steps = batch * groups
    w_specs = [pl.BlockSpec((w.shape[0] // steps, w.shape[1]), lambda b, p: (b * groups + p, 0))
               for w in weights]
    out, *cast = pl.pallas_call(
        _with_weight_casts(kernel, len(weights)),
        grid=(batch, groups),
        in_specs=_pair_specs(seq, first, third + first, 2 * third + first) + w_specs,
        out_specs=[pl.BlockSpec((seq, width), lambda b, p: (b, p))] + w_specs,
        out_shape=[jax.ShapeDtypeStruct((batch * seq, groups * width), BF16)]
        + [jax.ShapeDtypeStruct(w.shape, BF16) for w in weights],
        scratch_shapes=[pltpu.VMEM((PAIRS_PER_STEP, seq // ATT_TILE, LANES, ATT_TILE), BF16),
                        pltpu.VMEM((PAIRS_PER_STEP, LANES, 4 * ATT_TILE), F32),
                        *extra_scratch],
        compiler_params=_params(2),
        name=name,
    )(qkv, qkv, qkv, *weights)
    return out, cast


def _sb_attention(qkv, weights, batch, seq):
    staged = pltpu.VMEM((4 * PAIRS_PER_STEP, ATT_TILE, 4 * ATT_TILE), F32)
    return _attention_call(_sb_kernel, qkv, weights, batch, seq, N_SB_HEADS, 0, [staged],
                           "sb_attention")


def _moba_kernel(q_ref, k_ref, v_ref, o_ref, vt_ref, acc_ref, sel_ref, s_ref):
    t = ATT_TILE
    w = 2 * t
    seq = q_ref.shape[0]
    nb = seq // MOBA_BLOCK
    pairs = range(q_ref.shape[1] // LANES)
    per_half = 2 * len(pairs)
    causal, chan0 = _tile_masks(False)
    blk = lax.broadcasted_iota(jnp.int32, (nb, 2 * w), 0)
    cell = lax.broadcasted_iota(jnp.int32, (nb, 2 * w), 1) // w
    blk_f = blk.astype(F32)
    kmean = [(jnp.sum(k_ref[:, _pair_lanes(p)].astype(F32).reshape(nb, MOBA_BLOCK, LANES), axis=1)
              * (1.0 / MOBA_BLOCK)).astype(BF16) for p in pairs]
    _fill_vt(v_ref, vt_ref)


    def select(p, i, q2):
        gate = jnp.dot(kmean[p], q2, preferred_element_type=F32)
        past_blk = blk < 2 * i + cell
        gate = jnp.where(past_blk, gate, NEG)
        sel = jnp.zeros((nb, 2 * w), F32)
        for _r in range(MOBA_TOPK):
            top = jnp.max(gate, axis=0, keepdims=True)
            idx = jnp.min(jnp.where(gate == top, blk_f, float(nb)), axis=0, keepdims=True)
            hit = blk_f == idx
            sel = jnp.where(hit & past_blk, 1.0, sel)
            gate = jnp.where(hit, -jnp.inf, gate)
        sel_ref[p] = sel

    def own_blocks(p, i, raw_a, raw_b):
        s = jnp.where(causal, raw_b, NEG)
        m_b = jnp.max(s, axis=0, keepdims=True)
        pr = jnp.exp2(s - m_b)
        acc_ref[p, :, :w] = jnp.zeros((LANES, w), F32)
        acc_ref[p, :, w:] = jnp.dot(vt_ref[p, 2 * i + 1], pr.astype(BF16),
                                    preferred_element_type=F32)
        m = jnp.concatenate([jnp.full((1, w), -jnp.inf, F32), m_b], axis=1)
        l = jnp.concatenate([jnp.zeros((1, w), F32), jnp.sum(pr, axis=0, keepdims=True)], axis=1)
        keep = jnp.concatenate(
            [causal, jnp.broadcast_to(sel_ref[p, pl.ds(2 * i, 1), w:] > 0.0, (t, w))], axis=1)
        s = jnp.where(keep, raw_a, NEG)
        m_new = jnp.maximum(m, jnp.max(s, axis=0, keepdims=True))
        alpha = jnp.exp2(m - m_new)
        pr = jnp.exp2(s - m_new)
        acc_ref[p] = alpha * acc_ref[p] + jnp.dot(
            vt_ref[p, 2 * i], pr.astype(BF16), preferred_element_type=F32)
        return m_new, alpha * l + jnp.sum(pr, axis=0, keepdims=True)

    def past_pair(p, j, raw_a, raw_b, m_old, l_old):
        s_a = jnp.where(sel_ref[p, pl.ds(j, 1), :] > 0.0, raw_a, NEG)
        s_b = jnp.where(sel_ref[p, pl.ds(j + 1, 1), :] > 0.0, raw_b, NEG)
        m_new = jnp.maximum(m_old, jnp.maximum(jnp.max(s_a, axis=0, keepdims=True),
                                               jnp.max(s_b, axis=0, keepdims=True)))
        alpha = jnp.exp2(m_old - m_new)
        p_a = jnp.exp2(s_a - m_new)
        p_b = jnp.exp2(s_b - m_new)
        part = (jnp.dot(vt_ref[p, j], p_a.astype(BF16), preferred_element_type=F32)
                + jnp.dot(vt_ref[p, j + 1], p_b.astype(BF16), preferred_element_type=F32))
        acc_ref[p] = alpha * acc_ref[p] + part
        l_new = (alpha * l_old + jnp.sum(p_a, axis=0, keepdims=True)
                 + jnp.sum(p_b, axis=0, keepdims=True))
        return m_new, l_new

    def stage_pair_of_blocks(half, j, tiles):
        for p in pairs:
            for b in range(2):
                s_ref[half + 2 * p + b] = _scores(k_ref, p, j + b, tiles[p][0])

    def q_super(i, _):
        tiles = [_load_super_tile(q_ref, p, i, chan0) for p in pairs]
        own = lax.rem(i, 2) * per_half
        for p in pairs:
            q2, qb = tiles[p]
            s_ref[own + 2 * p + 1, :, w:] = _scores(k_ref, p, 2 * i + 1, qb)
            s_ref[own + 2 * p] = _scores(k_ref, p, 2 * i, q2)
            select(p, i, q2)
        stats = tuple(own_blocks(p, i, s_ref[own + 2 * p], s_ref[own + 2 * p + 1, :, w:])
                      for p in pairs)

        def past(jj, stats):
            half = lax.rem(jj, 2) * per_half
            stage_pair_of_blocks(half, 2 * jj, tiles)
            return tuple(past_pair(p, 2 * jj, s_ref[half + 2 * p], s_ref[half + 2 * p + 1],
                                   *stats[p]) for p in pairs)

        stats = lax.fori_loop(0, i, past, stats)
        for p in pairs:
            _store_super_tile(o_ref, p, i, acc_ref[p] / stats[p][1], chan0)
        return 0

    lax.fori_loop(0, seq // w, q_super, 0)


def _moba_attention(qkv, weights, batch, seq):
    sel = pltpu.VMEM((PAIRS_PER_STEP, seq // MOBA_BLOCK, 4 * ATT_TILE), F32)
    staged = pltpu.VMEM((4 * PAIRS_PER_STEP, ATT_TILE, 4 * ATT_TILE), F32)
    return _attention_call(_moba_kernel, qkv, weights, batch, seq, N_MOBA_HEADS, N_SB_HEADS,
                           [sel, staged], "moba_attention")


def _ret_kernel(q_ref, k_ref, v_ref, g_ref, gn_ref, o_ref, state_ref, decay_ref):
    c = RET_CHUNK
    heads = range(RET_HEADS_PER_STEP)
    n_row = lax.broadcasted_iota(jnp.int32, (c, 1), 0).astype(F32)
    log_g, xi, zeta, chunk_decay = [], [], [], []
    for hh in heads:
        head = (pl.program_id(1) * RET_HEADS_PER_STEP + hh + 5).astype(F32)
        lg = jnp.log(1.0 - jnp.exp2(-jnp.full((1, 1), head, F32)))
        log_g.append(lg)
        xi.append(jnp.exp(lg * (n_row + 1.0)))
        zeta.append(jnp.exp(lg * (c - 1.0 - n_row)))
        chunk_decay.append(jnp.exp(lg * c))

    @pl.when(pl.program_id(2) == 0)
    def _():
        diff = (lax.broadcasted_iota(jnp.int32, (c, c), 0)
                - lax.broadcasted_iota(jnp.int32, (c, c), 1)).astype(F32)
        for hh in heads:
            decay_ref[hh] = jnp.where(diff >= 0, jnp.exp(log_g[hh] * jnp.maximum(diff, 0.0)), 0.0)
        state_ref[...] = jnp.zeros_like(state_ref)

    def chunk(i, _):
        rows = pl.ds(pl.multiple_of(i * c, c), c)
        for hh in heads:
            qk_cols = slice(hh * RET_DK, (hh + 1) * RET_DK)
            v_cols = slice(hh * RET_DV, (hh + 1) * RET_DV)
            q, k, v = q_ref[rows, qk_cols], k_ref[rows, qk_cols], v_ref[rows, v_cols]
            inner = lax.dot_general(q, k, (((1,), (1,)), ((), ())),
                                    preferred_element_type=F32) * decay_ref[hh]
            state = state_ref[hh]
            o = (jnp.dot(inner.astype(BF16), v, preferred_element_type=F32)
                 + jnp.dot(q, state.astype(BF16), preferred_element_type=F32) * xi[hh])
            kz = (k.astype(F32) * zeta[hh]).astype(BF16)
            state_ref[hh] = state * chunk_decay[hh] + lax.dot_general(
                kz, v, (((0,), (0,)), ((), ())), preferred_element_type=F32)
            mu = jnp.mean(o, axis=-1, keepdims=True)
            var = jnp.mean(jnp.square(o - mu), axis=-1, keepdims=True)
            on = (o - mu) * lax.rsqrt(var + GN_EPS) * gn_ref[:, v_cols]
            o_ref[rows, v_cols] = (_silu(g_ref[rows, v_cols].astype(F32)) * on).astype(BF16)
        return 0

    lax.fori_loop(0, q_ref.shape[0] // c, chunk, 0)


def _retention(proj, gn_gain, batch, seq):
    hps = RET_HEADS_PER_STEP
    tile = RET_SEQ_TILE
    tiles = seq // tile
    qk_w, v_w = hps * RET_DK, hps * RET_DV
    k0 = RET_QK_W // qk_w
    v0 = 2 * RET_QK_W // v_w
    g0 = v0 + RET_V_W // v_w
    row = lambda b, h, s: b * tiles + s
    return pl.pallas_call(
        _ret_kernel,
        grid=(batch, N_RET_HEADS // hps, tiles),
        in_specs=[
            pl.BlockSpec((tile, qk_w), lambda b, h, s: (row(b, h, s), h)),
            pl.BlockSpec((tile, qk_w), lambda b, h, s: (row(b, h, s), k0 + h)),
            pl.BlockSpec((tile, v_w), lambda b, h, s: (row(b, h, s), v0 + h)),
            pl.BlockSpec((tile, v_w), lambda b, h, s: (row(b, h, s), g0 + h)),
            pl.BlockSpec((1, v_w), lambda b, h, s: (0, h)),
        ],
        out_specs=pl.BlockSpec((tile, v_w), lambda b, h, s: (row(b, h, s), h)),
        out_shape=jax.ShapeDtypeStruct((batch * seq, RET_V_W), BF16),
        scratch_shapes=[pltpu.VMEM((hps, RET_DK, RET_DV), F32),
                        pltpu.VMEM((hps, RET_CHUNK, RET_CHUNK), F32)],
        compiler_params=_params(3),
        name="retention",
    )(proj, proj, proj, proj, gn_gain)


def _mix_ffn_kernel(*refs, n_in, final_norm):
    x_ref, gate_m_ref, g_ref, sh_ref, sc_ref, gate_f_ref = refs[:6]
    a_refs, w_refs = refs[6:6 + n_in], refs[6 + n_in:6 + 2 * n_in]
    wgu_ref, wd_ref, fin_ref, o_ref = refs[6 + 2 * n_in:]
    y = jnp.dot(a_refs[0][...], w_refs[0][...], preferred_element_type=F32)
    for a_ref, w_ref in zip(a_refs[1:], w_refs[1:]):
        y += jnp.dot(a_ref[...], w_ref[...], preferred_element_type=F32)
    x = x_ref[...] + gate_m_ref[0] * y
    h = _norm_mod(x, g_ref[...], sh_ref[0], sc_ref[0]).astype(BF16)
    y = None
    for lo, hi in FF_CHUNKS:
        gt = jnp.dot(h, wgu_ref[:, lo:hi], preferred_element_type=F32)
        up = jnp.dot(h, wgu_ref[:, D_FF + lo:D_FF + hi], preferred_element_type=F32)
        part = jnp.dot((_silu(gt) * up).astype(BF16), wd_ref[lo:hi, :],
                       preferred_element_type=F32)
        y = part if y is None else y + part
    out = x + gate_f_ref[0] * y
    if final_norm:
        out = out * lax.rsqrt(jnp.mean(out * out, axis=-1, keepdims=True) + NORM_EPS) * fin_ref[...]
    o_ref[...] = out


def _mix_ffn(x, gate_m, acts, w_outs, gain, shift, scale, gate_f, w_gate_up, w_down, layer,
             fin_gain, seq, final_norm, name):
    t, d = x.shape
    tm = ROW_TILE
    per_seq = seq // tm
    row = lambda i: (i, 0)
    batch = lambda i: (i // per_seq, 0, 0)
    const = lambda i: (0, 0)
    this_layer = lambda i: (layer, 0, 0)
    return pl.pallas_call(
        functools.partial(_mix_ffn_kernel, n_in=len(acts), final_norm=final_norm),
        grid=(t // tm,),
        in_specs=[
            pl.BlockSpec((tm, d), row),
            pl.BlockSpec((1, 1, d), batch),
            pl.BlockSpec((1, d), const),
            pl.BlockSpec((1, 1, d), batch),
            pl.BlockSpec((1, 1, d), batch),
            pl.BlockSpec((1, 1, d), batch),
            *[pl.BlockSpec((tm, a.shape[1]), row) for a in acts],
            *[_resident(w.shape, const) for w in w_outs],
            _resident((None,) + w_gate_up.shape[1:], this_layer),
            _resident((None,) + w_down.shape[1:], this_layer),
            pl.BlockSpec((1, d), const),
        ],
        out_specs=pl.BlockSpec((tm, d), row),
        out_shape=jax.ShapeDtypeStruct((t, d), F32),
        compiler_params=_params(1),
        name=name,
    )(x, gate_m, gain, shift, scale, gate_f, *acts, *w_outs, w_gate_up, w_down, fin_gain)


def _rope_tables(seq, dim, tile):
    inv = ROPE_THETA ** (-jnp.arange(0, dim, 2, dtype=F32) / dim)
    hi = (jnp.arange(seq // ROPE_SIDE, dtype=jnp.int32) * ROPE_SIDE).astype(F32)[:, None] * inv
    lo = jnp.arange(ROPE_SIDE, dtype=jnp.int32).astype(F32)[:, None] * inv
    ch, sh, cl, sl = jnp.cos(hi)[:, None], jnp.sin(hi)[:, None], jnp.cos(lo)[None], jnp.sin(lo)[None]
    cos = (ch * cl - sh * sl).reshape(seq, dim // 2)
    sin = (sh * cl + ch * sl).reshape(seq, dim // 2)
    return jnp.tile(cos, (1, tile)), jnp.tile(sin, (1, tile))


def kernel(x, c, ada_w, ada_b, norm_gains, att_w_qkv, att_w_o, ret_w_in, ret_gn, ret_w_o,
           ffn_w_gate_up, ffn_w_down, final_norm):
    batch, seq, d = x.shape
    depth = ada_w.shape[0]
    assert d == D_MODEL and batch <= 8 and seq % ROW_TILE == 0
    assert seq % (2 * ATT_TILE) == 0 and seq % RET_SEQ_TILE == 0

    mod = _ada_mod(c, ada_w, ada_b)[:, :batch].reshape(depth, batch, 1, 6, d)
    xt = x.reshape(batch * seq, d)

    half = HEAD_DIM // 2
    cos_a, sin_a = _rope_tables(seq, HEAD_DIM, LANES // half)
    first_half = (jnp.arange(LANES) % HEAD_DIM) < half
    att_tables = (cos_a, jnp.where(first_half, -sin_a, 0.0), jnp.where(first_half, 0.0, sin_a))
    ret_tables = _rope_tables(seq, RET_DK, 1)
    assert depth == 2 and att_w_qkv.shape[0] == 1 and ret_w_in.shape[0] == 1
    rows = lambda w: w.reshape(-1, w.shape[-1])

    for layer in range(depth):
        shift_m, scale_m, gate_m, shift_f, scale_f, gate_f = (mod[layer, :, :, i] for i in range(6))
        gains = norm_gains[layer]
        if layer % 2 == 0:
            qkv = _proj(_att_proj_kernel, xt, gains[0:1], shift_m, scale_m, att_tables,
                        att_w_qkv[0].astype(BF16), seq, "att_proj")
            sb, (w_gate_up, w_ret_o, w_att_o) = _sb_attention(
                qkv, [rows(ffn_w_gate_up), rows(ret_w_o), rows(att_w_o)], batch, seq)
            mb, (w_down, w_ret_in) = _moba_attention(
                qkv, [rows(ffn_w_down), rows(ret_w_in)], batch, seq)
            w_gate_up = w_gate_up.reshape(ffn_w_gate_up.shape)
            w_down = w_down.reshape(ffn_w_down.shape)
            split = N_SB_HEADS * HEAD_DIM
            acts, w_outs = (sb, mb), (w_att_o[:split], w_att_o[split:])
        else:
            proj = _proj(_ret_proj_kernel, xt, gains[0:1], shift_m, scale_m, ret_tables,
                         w_ret_in, seq, "ret_proj")
            acts = (_retention(proj, ret_gn[0].reshape(1, RET_V_W), batch, seq),)
            w_outs = (w_ret_o,)
        xt = _mix_ffn(xt, gate_m, acts, w_outs, gains[1:2], shift_f, scale_f, gate_f, w_gate_up,
                      w_down, layer, final_norm.reshape(1, d), seq, layer == depth - 1,
                      "mix_ffn%d" % layer)
    return xt.reshape(batch, seq, d)
```

```python
import functools
import math

import jax
import jax.numpy as jnp
from jax import lax
from jax.experimental import pallas as pl
from jax.experimental.pallas import tpu as pltpu

F32 = jnp.float32
BF16 = jnp.bfloat16

D_MODEL = 1024
HEAD_DIM = 64
N_SB_HEADS = 8
N_MOBA_HEADS = 8
ATT_W = (N_SB_HEADS + N_MOBA_HEADS) * HEAD_DIM
MOBA_BLOCK = 256
MOBA_TOPK = 3
N_RET_HEADS = 4
RET_DK = D_MODEL // N_RET_HEADS
RET_DV = 2 * RET_DK
RET_QK_W = N_RET_HEADS * RET_DK
RET_V_W = N_RET_HEADS * RET_DV
D_FF = -(-8 * D_MODEL // 768) * 256
ROPE_THETA = 10000.0
ROPE_SIDE = 64
NORM_EPS = 1e-6
GN_EPS = 1e-5
NEG = -1e30
Q_SCALE = HEAD_DIM ** -0.5 * math.log2(math.e)

LANES = 128
ROW_TILE = 512
PROJ_SLABS = 2
COL_CHUNK = 512
MXU_WIDTH = 256
_FF_SPLIT = (D_FF // MXU_WIDTH + 1) // 2 * MXU_WIDTH
FF_CHUNKS = ((0, _FF_SPLIT), (_FF_SPLIT, D_FF))
ATT_TILE = 256
PAIRS_PER_STEP = 2
RET_CHUNK = 256
RET_SEQ_TILE = 1024
RET_HEADS_PER_STEP = 4
VMEM_LIMIT = 56 * 1024 * 1024


def _params(n_axes):
    return pltpu.CompilerParams(
        dimension_semantics=("arbitrary",) * n_axes, vmem_limit_bytes=VMEM_LIMIT)


def _resident(shape, index_map):
    return pl.BlockSpec(shape, index_map, pipeline_mode=pl.Buffered(1))


def _silu(v):
    return v * (1.0 / (1.0 + jnp.exp(-v)))


def _ada_kernel(c_ref, w_ref, b_ref, o_ref):
    o_ref[0] = jnp.dot(_silu(c_ref[...]), w_ref[0], preferred_element_type=F32) + b_ref[0]


def _ada_mod(c, ada_w, ada_b):
    depth, d, n = ada_w.shape
    cp = jnp.pad(c, ((0, 8 - c.shape[0]), (0, 0)))
    tn = 1536
    return pl.pallas_call(
        _ada_kernel,
        grid=(depth, n // tn),
        in_specs=[
            pl.BlockSpec((8, d), lambda l, j: (0, 0)),
            pl.BlockSpec((1, d, tn), lambda l, j: (l, 0, j)),
            pl.BlockSpec((1, 1, tn), lambda l, j: (l, 0, j)),
        ],
        out_specs=pl.BlockSpec((1, 8, tn), lambda l, j: (l, 0, j)),
        out_shape=jax.ShapeDtypeStruct((depth, 8, n), F32),
        compiler_params=_params(2),
        name="ada_mod",
    )(cp, ada_w, ada_b.reshape(depth, 1, n))


def _norm_mod(x, g, shift, scale):
    y = x * lax.rsqrt(jnp.mean(x * x, axis=-1, keepdims=True) + NORM_EPS)
    return (y * g) * (1.0 + scale) + shift


def _att_proj_kernel(x_ref, g_ref, sh_ref, sc_ref, cos_ref, sina_ref, sinb_ref, w_ref, o_ref):
    sb_w = N_SB_HEADS * HEAD_DIM
    for r in range(x_ref.shape[0] // ROW_TILE):
        rows = slice(r * ROW_TILE, (r + 1) * ROW_TILE)
        h = _norm_mod(x_ref[rows, :], g_ref[...], sh_ref[0], sc_ref[0]).astype(BF16)
        cos, sina, sinb = cos_ref[rows, :], sina_ref[rows, :], sinb_ref[rows, :]
        for c in range(3 * ATT_W // COL_CHUNK):
            lo = c * COL_CHUNK
            y = jnp.dot(h, w_ref[:, lo:lo + COL_CHUNK], preferred_element_type=F32)
            is_q = lo < ATT_W
            rotated = (lo % ATT_W) >= sb_w and lo < 2 * ATT_W
            for s in range(COL_CHUNK // LANES):
                ys = y[:, s * LANES:(s + 1) * LANES]
                if rotated:
                    ys = (ys * cos + pltpu.roll(ys, LANES - HEAD_DIM // 2, 1) * sina
                          + pltpu.roll(ys, HEAD_DIM // 2, 1) * sinb)
                if is_q:
                    ys = ys * Q_SCALE
                o_ref[rows, lo + s * LANES:lo + (s + 1) * LANES] = ys.astype(BF16)


def _ret_proj_kernel(x_ref, g_ref, sh_ref, sc_ref, cos_ref, sin_ref, w_ref, o_ref):
    n = 2 * RET_QK_W + 2 * RET_V_W
    half = RET_DK // 2
    for r in range(x_ref.shape[0] // ROW_TILE):
        rows = slice(r * ROW_TILE, (r + 1) * ROW_TILE)
        h = _norm_mod(x_ref[rows, :], g_ref[...], sh_ref[0], sc_ref[0]).astype(BF16)
        cos, sin = cos_ref[rows, :], sin_ref[rows, :]
        for c in range(n // COL_CHUNK):
            lo = c * COL_CHUNK
            y = jnp.dot(h, w_ref[:, lo:lo + COL_CHUNK], preferred_element_type=F32)
            if lo < 2 * RET_QK_W:
                mul = 1.0 if lo < RET_QK_W else RET_DK ** -0.5
                for hd in range(COL_CHUNK // RET_DK):
                    x1 = y[:, hd * RET_DK:hd * RET_DK + half]
                    x2 = y[:, hd * RET_DK + half:(hd + 1) * RET_DK]
                    o1 = (x1 * cos - x2 * sin) * mul
                    o2 = (x2 * cos + x1 * sin) * mul
                    o_ref[rows, lo + hd * RET_DK:lo + hd * RET_DK + half] = o1.astype(BF16)
                    o_ref[rows, lo + hd * RET_DK + half:lo + (hd + 1) * RET_DK] = o2.astype(BF16)
            else:
                o_ref[rows, lo:lo + COL_CHUNK] = y.astype(BF16)


def _proj(kernel, x, gain, shift, scale, tables, w, seq, name):
    t, d = x.shape
    n = w.shape[1]
    tm = PROJ_SLABS * ROW_TILE
    per_seq = seq // tm
    row = lambda i: (i, 0)
    batch = lambda i: (i // per_seq, 0, 0)
    pos = lambda i: (i % per_seq, 0)
    return pl.pallas_call(
        kernel,
        grid=(t // tm,),
        in_specs=[
            pl.BlockSpec((tm, d), row),
            pl.BlockSpec((1, d), lambda i: (0, 0)),
            pl.BlockSpec((1, 1, d), batch),
            pl.BlockSpec((1, 1, d), batch),
            *[pl.BlockSpec((tm, LANES), pos) for _ in tables],
            _resident((d, n), lambda i: (0, 0)),
        ],
        out_specs=pl.BlockSpec((tm, n), row),
        out_shape=jax.ShapeDtypeStruct((t, n), BF16),
        compiler_params=_params(1),
        name=name,
    )(x, gain, shift, scale, *tables, w)


def _pair_lanes(p):
    return slice(p * LANES, (p + 1) * LANES)


def _fill_vt(v_ref, vt_ref):
    t = ATT_TILE
    for p in range(v_ref.shape[1] // LANES):
        for n in range(v_ref.shape[0] // t):
            vt_ref[p, n] = v_ref[n * t:(n + 1) * t, _pair_lanes(p)].T


def _tile_masks(strict):
    t = ATT_TILE
    key = lax.broadcasted_iota(jnp.int32, (t, 2 * t), 0)
    qry = lax.broadcasted_iota(jnp.int32, (t, 2 * t), 1) % t
    chan0 = lax.broadcasted_iota(jnp.int32, (LANES, 1), 0) < HEAD_DIM
    return (key < qry) if strict else (key <= qry), chan0


def _load_super_tile(q_ref, p, i, chan0):
    t = ATT_TILE
    base = pl.multiple_of(i * 2 * t, 2 * t)
    cells = []
    for c in range(2):
        q_t = q_ref[pl.ds(base + c * t, t), _pair_lanes(p)].T
        zero = jnp.zeros_like(q_t)
        cells += [jnp.where(chan0, q_t, zero), jnp.where(chan0, zero, q_t)]
    return jnp.concatenate(cells, axis=1), jnp.concatenate(cells[2:], axis=1)


def _store_super_tile(o_ref, p, i, acc_t, chan0):
    t = ATT_TILE
    base = pl.multiple_of(i * 2 * t, 2 * t)
    for cell in range(2):
        lo = cell * 2 * t
        o_t = jnp.where(chan0, acc_t[:, lo:lo + t], acc_t[:, lo + t:lo + 2 * t])
        o_ref[pl.ds(base + cell * t, t), _pair_lanes(p)] = o_t.T.astype(BF16)


def _scores(k_ref, p, j, q_cols, blocks=1):
    t = ATT_TILE
    return jnp.dot(k_ref[pl.ds(pl.multiple_of(j * t, t), blocks * t), _pair_lanes(p)], q_cols,
                   preferred_element_type=F32)


def _stage_pair_of_blocks(stage_ref, k_ref, half, j, tiles):
    t = ATT_TILE
    for p, (q_cols, _) in enumerate(tiles):
        z = _scores(k_ref, p, j, q_cols, blocks=2)
        stage_ref[half + 2 * p] = z[:t]
        stage_ref[half + 2 * p + 1] = z[t:]


def _sb_kernel(q_ref, k_ref, v_ref, o_ref, vt_ref, acc_ref, z_ref):
    t = ATT_TILE
    w = 2 * t
    seq = q_ref.shape[0]
    pairs = range(q_ref.shape[1] // LANES)
    per_half = 2 * len(pairs)
    strict, chan0 = _tile_masks(True)
    from_key = -(lax.broadcasted_iota(jnp.int32, (t, t), 1)
                 >= lax.broadcasted_iota(jnp.int32, (t, t), 0)).astype(BF16)
    _fill_vt(v_ref, vt_ref)

    def block(p, j, slot, lanes, run, mask):
        z = z_ref[slot, :, lanes]
        neg_l = jnp.maximum(z, 0.0) + jnp.log2(1.0 + jnp.exp2(-jnp.abs(z)))
        if mask is not None:
            neg_l = jnp.where(mask, neg_l, 0.0)
        since = jnp.dot(from_key, neg_l.astype(BF16), preferred_element_type=F32)
        a = jnp.exp2(z_ref[slot, :, lanes] + since)
        if mask is not None:
            a = jnp.where(mask, a, 0.0)
        part = jnp.dot(vt_ref[p, j], a.astype(BF16), preferred_element_type=F32)
        return run + since[0:1, :], part * jnp.exp2(run)

    def q_super(i, _):
        tiles = [_load_super_tile(q_ref, p, i, chan0) for p in pairs]
        mask = jnp.concatenate([strict, jnp.ones((t, w), jnp.bool_)], axis=1)
        cell1 = slice(w, 2 * w)
        diag = lax.rem(i, 2) * per_half
        for p in pairs:
            q2, qb = tiles[p]
            z_ref[diag + 2 * p + 1, :, cell1] = _scores(k_ref, p, 2 * i + 1, qb)
            z_ref[diag + 2 * p] = _scores(k_ref, p, 2 * i, q2)
        runs = []
        for p in pairs:
            run_b, part = block(p, 2 * i + 1, diag + 2 * p + 1, cell1, jnp.zeros((1, w), F32),
                                strict)
            acc_ref[p, :, w:] = part
            run = jnp.concatenate([jnp.zeros((1, w), F32), run_b], axis=1)
            run, part = block(p, 2 * i, diag + 2 * p, slice(None), run, mask)
            acc_ref[p, :, :w] = part[:, :w]
            acc_ref[p, :, w:] += part[:, w:]
            runs.append(run)

        def past(s, runs):
            j = 2 * (i - 1 - s)
            half = lax.rem(s, 2) * per_half
            _stage_pair_of_blocks(z_ref, k_ref, half, j, tiles)
            out = []
            for p in pairs:
                run, part_b = block(p, j + 1, half + 2 * p + 1, slice(None), runs[p], None)
                run, part_a = block(p, j, half + 2 * p, slice(None), run, None)
                acc_ref[p] += part_b + part_a
                out.append(run)
            return tuple(out)

        lax.fori_loop(0, i, past, tuple(runs))
        for p in pairs:
            _store_super_tile(o_ref, p, i, acc_ref[p], chan0)
        return 0

    lax.fori_loop(0, seq // w, q_super, 0)


def _pair_specs(seq, q_col, k_col, v_col):
    width = PAIRS_PER_STEP * LANES
    return [
        pl.BlockSpec((seq, width), lambda b, p: (b, q_col + p)),
        pl.BlockSpec((seq, width), lambda b, p: (b, k_col + p)),
        pl.BlockSpec((seq, width), lambda b, p: (b, v_col + p)),
    ]


def _with_weight_casts(kernel, n_cast):
    def wrapped(q_ref, k_ref, v_ref, *refs):
        srcs, o_ref, dsts = refs[:n_cast], refs[n_cast], refs[n_cast + 1:2 * n_cast + 1]
        for src, dst in zip(srcs, dsts):
            dst[...] = src[...].astype(BF16)
        kernel(q_ref, k_ref, v_ref, o_ref, *refs[2 * n_cast + 1:])
    return wrapped


def _attention_call(kernel, qkv, weights, batch, seq, n_heads, first_head, extra_scratch, name):
    width = PAIRS_PER_STEP * LANES
    groups = n_heads * HEAD_DIM // width
    third = ATT_W // width
    first = first_head * HEAD_DIM // width
    steps = batch * groups
    w_specs = [pl.BlockSpec((w.shape[0] // steps, w.shape[1]), lambda b, p: (b * groups + p, 0))
               for w in weights]
    out, *cast = pl.pallas_call(
        _with_weight_casts(kernel, len(weights)),
        grid=(batch, groups),
        in_specs=_pair_specs(seq, first, third + first, 2 * third + first) + w_specs,
        out_specs=[pl.BlockSpec((seq, width), lambda b, p: (b, p))] + w_specs,
        out_shape=[jax.ShapeDtypeStruct((batch * seq, groups * width), BF16)]
        + [jax.ShapeDtypeStruct(w.shape, BF16) for w in weights],
        scratch_shapes=[pltpu.VMEM((PAIRS_PER_STEP, seq // ATT_TILE, LANES, ATT_TILE), BF16),
                        pltpu.VMEM((PAIRS_PER_STEP, LANES, 4 * ATT_TILE), F32),
                        *extra_scratch],
        compiler_params=_params(2),
        name=name,
    )(qkv, qkv, qkv, *weights)
    return out, cast


def _sb_attention(qkv, weights, batch, seq):
    staged = pltpu.VMEM((4 * PAIRS_PER_STEP, ATT_TILE, 4 * ATT_TILE), F32)
    return _attention_call(_sb_kernel, qkv, weights, batch, seq, N_SB_HEADS, 0, [staged],
                           "sb_attention")


def _moba_kernel(q_ref, k_ref, v_ref, o_ref, vt_ref, acc_ref, sel_ref, s_ref):
    t = ATT_TILE
    w = 2 * t
    seq = q_ref.shape[0]
    nb = seq // MOBA_BLOCK
    pairs = range(q_ref.shape[1] // LANES)
    per_half = 2 * len(pairs)
    causal, chan0 = _tile_masks(False)
    blk = lax.broadcasted_iota(jnp.int32, (nb, 2 * w), 0)
    cell = lax.broadcasted_iota(jnp.int32, (nb, 2 * w), 1) // w
    blk_f = blk.astype(F32)
    kmean = [(jnp.sum(k_ref[:, _pair_lanes(p)].astype(F32).reshape(nb, MOBA_BLOCK, LANES), axis=1)
              * (1.0 / MOBA_BLOCK)).astype(BF16) for p in pairs]
    _fill_vt(v_ref, vt_ref)


    def select(p, i, q2):
        gate = jnp.dot(kmean[p], q2, preferred_element_type=F32)
        past_blk = blk < 2 * i + cell
        gate = jnp.where(past_blk, gate, NEG)
        sel = jnp.zeros((nb, 2 * w), F32)
        for _r in range(MOBA_TOPK):
            top = jnp.max(gate, axis=0, keepdims=True)
            idx = jnp.min(jnp.where(gate == top, blk_f, float(nb)), axis=0, keepdims=True)
            hit = blk_f == idx
            sel = jnp.where(hit & past_blk, 1.0, sel)
            gate = jnp.where(hit, -jnp.inf, gate)
        sel_ref[p] = sel

    def own_blocks(p, i, raw_a, raw_b):
        s = jnp.where(causal, raw_b, NEG)
        m_b = jnp.max(s, axis=0, keepdims=True)
        pr = jnp.exp2(s - m_b)
        acc_ref[p, :, :w] = jnp.zeros((LANES, w), F32)
        acc_ref[p, :, w:] = jnp.dot(vt_ref[p, 2 * i + 1], pr.astype(BF16),
                                    preferred_element_type=F32)
        m = jnp.concatenate([jnp.full((1, w), -jnp.inf, F32), m_b], axis=1)
        l = jnp.concatenate([jnp.zeros((1, w), F32), jnp.sum(pr, axis=0, keepdims=True)], axis=1)
        keep = jnp.concatenate(
            [causal, jnp.broadcast_to(sel_ref[p, pl.ds(2 * i, 1), w:] > 0.0, (t, w))], axis=1)
        s = jnp.where(keep, raw_a, NEG)
        m_new = jnp.maximum(m, jnp.max(s, axis=0, keepdims=True))
        alpha = jnp.exp2(m - m_new)
        pr = jnp.exp2(s - m_new)
        acc_ref[p] = alpha * acc_ref[p] + jnp.dot(
            vt_ref[p, 2 * i], pr.astype(BF16), preferred_element_type=F32)
        return m_new, alpha * l + jnp.sum(pr, axis=0, keepdims=True)

    def past_pair(p, j, raw_a, raw_b, m_old, l_old):
        s_a = jnp.where(sel_ref[p, pl.ds(j, 1), :] > 0.0, raw_a, NEG)
        s_b = jnp.where(sel_ref[p, pl.ds(j + 1, 1), :] > 0.0, raw_b, NEG)
        m_new = jnp.maximum(m_old, jnp.maximum(jnp.max(s_a, axis=0, keepdims=True),
                                               jnp.max(s_b, axis=0, keepdims=True)))
        alpha = jnp.exp2(m_old - m_new)
        p_a = jnp.exp2(s_a - m_new)
        p_b = jnp.exp2(s_b - m_new)
        part = (jnp.dot(vt_ref[p, j], p_a.astype(BF16), preferred_element_type=F32)
                + jnp.dot(vt_ref[p, j + 1], p_b.astype(BF16), preferred_element_type=F32))
        acc_ref[p] = alpha * acc_ref[p] + part
        l_new = (alpha * l_old + jnp.sum(p_a, axis=0, keepdims=True)
                 + jnp.sum(p_b, axis=0, keepdims=True))
        return m_new, l_new

    def q_super(i, _):
        tiles = [_load_super_tile(q_ref, p, i, chan0) for p in pairs]
        own = lax.rem(i, 2) * per_half
        for p in pairs:
            q2, qb = tiles[p]
            s_ref[own + 2 * p + 1, :, w:] = _scores(k_ref, p, 2 * i + 1, qb)
            s_ref[own + 2 * p] = _scores(k_ref, p, 2 * i, q2)
            select(p, i, q2)
        stats = tuple(own_blocks(p, i, s_ref[own + 2 * p], s_ref[own + 2 * p + 1, :, w:])
                      for p in pairs)

        def past(jj, stats):
            half = lax.rem(jj, 2) * per_half
            _stage_pair_of_blocks(s_ref, k_ref, half, 2 * jj, tiles)
            return tuple(past_pair(p, 2 * jj, s_ref[half + 2 * p], s_ref[half + 2 * p + 1],
                                   *stats[p]) for p in pairs)

        stats = lax.fori_loop(0, i, past, stats)
        for p in pairs:
            _store_super_tile(o_ref, p, i, acc_ref[p] / stats[p][1], chan0)
        return 0

    lax.fori_loop(0, seq // w, q_super, 0)


def _moba_attention(qkv, weights, batch, seq):
    sel = pltpu.VMEM((PAIRS_PER_STEP, seq // MOBA_BLOCK, 4 * ATT_TILE), F32)
    staged = pltpu.VMEM((4 * PAIRS_PER_STEP, ATT_TILE, 4 * ATT_TILE), F32)
    return _attention_call(_moba_kernel, qkv, weights, batch, seq, N_MOBA_HEADS, N_SB_HEADS,
                           [sel, staged], "moba_attention")


def _ret_kernel(q_ref, k_ref, v_ref, g_ref, gn_ref, o_ref, state_ref, decay_ref):
    c = RET_CHUNK
    heads = range(RET_HEADS_PER_STEP)
    n_row = lax.broadcasted_iota(jnp.int32, (c, 1), 0).astype(F32)
    log_g, xi, zeta, chunk_decay = [], [], [], []
    for hh in heads:
        head = (pl.program_id(1) * RET_HEADS_PER_STEP + hh + 5).astype(F32)
        lg = jnp.log(1.0 - jnp.exp2(-jnp.full((1, 1), head, F32)))
        log_g.append(lg)
        xi.append(jnp.exp(lg * (n_row + 1.0)))
        zeta.append(jnp.exp(lg * (c - 1.0 - n_row)))
        chunk_decay.append(jnp.exp(lg * c))

    @pl.when(pl.program_id(2) == 0)
    def _():
        diff = (lax.broadcasted_iota(jnp.int32, (c, c), 0)
                - lax.broadcasted_iota(jnp.int32, (c, c), 1)).astype(F32)
        for hh in heads:
            decay_ref[hh] = jnp.where(diff >= 0, jnp.exp(log_g[hh] * jnp.maximum(diff, 0.0)), 0.0)
        state_ref[...] = jnp.zeros_like(state_ref)

    def chunk(i, _):
        rows = pl.ds(pl.multiple_of(i * c, c), c)
        for hh in heads:
            qk_cols = slice(hh * RET_DK, (hh + 1) * RET_DK)
            v_cols = slice(hh * RET_DV, (hh + 1) * RET_DV)
            q, k, v = q_ref[rows, qk_cols], k_ref[rows, qk_cols], v_ref[rows, v_cols]
            inner = lax.dot_general(q, k, (((1,), (1,)), ((), ())),
                                    preferred_element_type=F32) * decay_ref[hh]
            state = state_ref[hh]
            o = (jnp.dot(inner.astype(BF16), v, preferred_element_type=F32)
                 + jnp.dot(q, state.astype(BF16), preferred_element_type=F32) * xi[hh])
            kz = (k.astype(F32) * zeta[hh]).astype(BF16)
            state_ref[hh] = state * chunk_decay[hh] + lax.dot_general(
                kz, v, (((0,), (0,)), ((), ())), preferred_element_type=F32)
            mu = jnp.mean(o, axis=-1, keepdims=True)
            var = jnp.mean(jnp.square(o - mu), axis=-1, keepdims=True)
            on = (o - mu) * lax.rsqrt(var + GN_EPS) * gn_ref[:, v_cols]
            o_ref[rows, v_cols] = (_silu(g_ref[rows, v_cols].astype(F32)) * on).astype(BF16)
        return 0

    lax.fori_loop(0, q_ref.shape[0] // c, chunk, 0)


def _retention(proj, gn_gain, batch, seq):
    hps = RET_HEADS_PER_STEP
    tile = RET_SEQ_TILE
    tiles = seq // tile
    qk_w, v_w = hps * RET_DK, hps * RET_DV
    k0 = RET_QK_W // qk_w
    v0 = 2 * RET_QK_W // v_w
    g0 = v0 + RET_V_W // v_w
    row = lambda b, h, s: b * tiles + s
    return pl.pallas_call(
        _ret_kernel,
        grid=(batch, N_RET_HEADS // hps, tiles),
        in_specs=[
            pl.BlockSpec((tile, qk_w), lambda b, h, s: (row(b, h, s), h)),
            pl.BlockSpec((tile, qk_w), lambda b, h, s: (row(b, h, s), k0 + h)),
            pl.BlockSpec((tile, v_w), lambda b, h, s: (row(b, h, s), v0 + h)),
            pl.BlockSpec((tile, v_w), lambda b, h, s: (row(b, h, s), g0 + h)),
            pl.BlockSpec((1, v_w), lambda b, h, s: (0, h)),
        ],
        out_specs=pl.BlockSpec((tile, v_w), lambda b, h, s: (row(b, h, s), h)),
        out_shape=jax.ShapeDtypeStruct((batch * seq, RET_V_W), BF16),
        scratch_shapes=[pltpu.VMEM((hps, RET_DK, RET_DV), F32),
                        pltpu.VMEM((hps, RET_CHUNK, RET_CHUNK), F32)],
        compiler_params=_params(3),
        name="retention",
    )(proj, proj, proj, proj, gn_gain)


def _mix_ffn_kernel(*refs, n_in, final_norm):
    x_ref, gate_m_ref, g_ref, sh_ref, sc_ref, gate_f_ref = refs[:6]
    a_refs, w_refs = refs[6:6 + n_in], refs[6 + n_in:6 + 2 * n_in]
    wgu_ref, wd_ref, fin_ref, o_ref = refs[6 + 2 * n_in:]
    y = jnp.dot(a_refs[0][...], w_refs[0][...], preferred_element_type=F32)
    for a_ref, w_ref in zip(a_refs[1:], w_refs[1:]):
        y += jnp.dot(a_ref[...], w_ref[...], preferred_element_type=F32)
    x = x_ref[...] + gate_m_ref[0] * y
    h = _norm_mod(x, g_ref[...], sh_ref[0], sc_ref[0]).astype(BF16)
    y = None
    for lo, hi in FF_CHUNKS:
        gt = jnp.dot(h, wgu_ref[:, lo:hi], preferred_element_type=F32)
        up = jnp.dot(h, wgu_ref[:, D_FF + lo:D_FF + hi], preferred_element_type=F32)
        part = jnp.dot((_silu(gt) * up).astype(BF16), wd_ref[lo:hi, :],
                       preferred_element_type=F32)
        y = part if y is None else y + part
    out = x + gate_f_ref[0] * y
    if final_norm:
        out = out * lax.rsqrt(jnp.mean(out * out, axis=-1, keepdims=True) + NORM_EPS) * fin_ref[...]
    o_ref[...] = out


def _mix_ffn(x, gate_m, acts, w_outs, gain, shift, scale, gate_f, w_gate_up, w_down, layer,
             fin_gain, seq, final_norm, name):
    t, d = x.shape
    tm = ROW_TILE
    per_seq = seq // tm
    row = lambda i: (i, 0)
    batch = lambda i: (i // per_seq, 0, 0)
    const = lambda i: (0, 0)
    this_layer = lambda i: (layer, 0, 0)
    return pl.pallas_call(
        functools.partial(_mix_ffn_kernel, n_in=len(acts), final_norm=final_norm),
        grid=(t // tm,),
        in_specs=[
            pl.BlockSpec((tm, d), row),
            pl.BlockSpec((1, 1, d), batch),
            pl.BlockSpec((1, d), const),
            pl.BlockSpec((1, 1, d), batch),
            pl.BlockSpec((1, 1, d), batch),
            pl.BlockSpec((1, 1, d), batch),
            *[pl.BlockSpec((tm, a.shape[1]), row) for a in acts],
            *[_resident(w.shape, const) for w in w_outs],
            _resident((None,) + w_gate_up.shape[1:], this_layer),
            _resident((None,) + w_down.shape[1:], this_layer),
            pl.BlockSpec((1, d), const),
        ],
        out_specs=pl.BlockSpec((tm, d), row),
        out_shape=jax.ShapeDtypeStruct((t, d), F32),
        compiler_params=_params(1),
        name=name,
    )(x, gate_m, gain, shift, scale, gate_f, *acts, *w_outs, w_gate_up, w_down, fin_gain)


def _rope_tables(seq, dim, tile):
    inv = ROPE_THETA ** (-jnp.arange(0, dim, 2, dtype=F32) / dim)
    hi = (jnp.arange(seq // ROPE_SIDE, dtype=jnp.int32) * ROPE_SIDE).astype(F32)[:, None] * inv
    lo = jnp.arange(ROPE_SIDE, dtype=jnp.int32).astype(F32)[:, None] * inv
    ch, sh, cl, sl = jnp.cos(hi)[:, None], jnp.sin(hi)[:, None], jnp.cos(lo)[None], jnp.sin(lo)[None]
    cos = (ch * cl - sh * sl).reshape(seq, dim // 2)
    sin = (sh * cl + ch * sl).reshape(seq, dim // 2)
    return jnp.tile(cos, (1, tile)), jnp.tile(sin, (1, tile))


def kernel(x, c, ada_w, ada_b, norm_gains, att_w_qkv, att_w_o, ret_w_in, ret_gn, ret_w_o,
           ffn_w_gate_up, ffn_w_down, final_norm):
    batch, seq, d = x.shape
    depth = ada_w.shape[0]
    assert d == D_MODEL and batch <= 8 and seq % ROW_TILE == 0
    assert seq % (2 * ATT_TILE) == 0 and seq % RET_SEQ_TILE == 0

    mod = _ada_mod(c, ada_w, ada_b)[:, :batch].reshape(depth, batch, 1, 6, d)
    xt = x.reshape(batch * seq, d)

    half = HEAD_DIM // 2
    cos_a, sin_a = _rope_tables(seq, HEAD_DIM, LANES // half)
    first_half = (jnp.arange(LANES) % HEAD_DIM) < half
    att_tables = (cos_a, jnp.where(first_half, -sin_a, 0.0), jnp.where(first_half, 0.0, sin_a))
    ret_tables = _rope_tables(seq, RET_DK, 1)
    assert depth == 2 and att_w_qkv.shape[0] == 1 and ret_w_in.shape[0] == 1
    rows = lambda w: w.reshape(-1, w.shape[-1])

    for layer in range(depth):
        shift_m, scale_m, gate_m, shift_f, scale_f, gate_f = (mod[layer, :, :, i] for i in range(6))
        gains = norm_gains[layer]
        if layer % 2 == 0:
            qkv = _proj(_att_proj_kernel, xt, gains[0:1], shift_m, scale_m, att_tables,
                        att_w_qkv[0].astype(BF16), seq, "att_proj")
            sb, (w_gate_up, w_ret_o, w_att_o) = _sb_attention(
                qkv, [rows(ffn_w_gate_up), rows(ret_w_o), rows(att_w_o)], batch, seq)
            mb, (w_down, w_ret_in) = _moba_attention(
                qkv, [rows(ffn_w_down), rows(ret_w_in)], batch, seq)
            w_gate_up = w_gate_up.reshape(ffn_w_gate_up.shape)
            w_down = w_down.reshape(ffn_w_down.shape)
            split = N_SB_HEADS * HEAD_DIM
            acts, w_outs = (sb, mb), (w_att_o[:split], w_att_o[split:])
        else:
            proj = _proj(_ret_proj_kernel, xt, gains[0:1], shift_m, scale_m, ret_tables,
                         w_ret_in, seq, "ret_proj")
            acts = (_retention(proj, ret_gn[0].reshape(1, RET_V_W), batch, seq),)
            w_outs = (w_ret_o,)
        xt = _mix_ffn(xt, gate_m, acts, w_outs, gains[1:2], shift_f, scale_f, gate_f, w_gate_up,
                      w_down, layer, final_norm.reshape(1, d), seq, layer == depth - 1,
                      "mix_ffn%d" % layer)
    return xt.reshape(batch, seq, d)
```

```python
import functools
import math

import jax
import jax.numpy as jnp
from jax import lax
from jax.experimental import pallas as pl
from jax.experimental.pallas import tpu as pltpu

F32 = jnp.float32
BF16 = jnp.bfloat16

D_MODEL = 1024
HEAD_DIM = 64
N_SB_HEADS = 8
N_MOBA_HEADS = 8
ATT_W = (N_SB_HEADS + N_MOBA_HEADS) * HEAD_DIM
MOBA_BLOCK = 256
MOBA_TOPK = 3
N_RET_HEADS = 4
RET_DK = D_MODEL // N_RET_HEADS
RET_DV = 2 * RET_DK
RET_QK_W = N_RET_HEADS * RET_DK
RET_V_W = N_RET_HEADS * RET_DV
D_FF = -(-8 * D_MODEL // 768) * 256
ROPE_THETA = 10000.0
ROPE_SIDE = 64
NORM_EPS = 1e-6
GN_EPS = 1e-5
NEG = -1e30
Q_SCALE = HEAD_DIM ** -0.5 * math.log2(math.e)

LANES = 128
SUBLANES = 8
ADA_COL_TILE = 1536
ROW_TILE = 512
PROJ_SLABS = 2
MIX_SLABS = 2
COL_CHUNK = 512
MXU_WIDTH = 256
_FF_SPLIT = (D_FF // MXU_WIDTH + 1) // 2 * MXU_WIDTH
FF_CHUNKS = ((0, _FF_SPLIT), (_FF_SPLIT, D_FF))
ATT_TILE = 256
PAIRS_PER_STEP = 2
RET_CHUNK = 256
RET_SEQ_TILE = 1024
RET_HEADS_PER_STEP = 4
VMEM_LIMIT = 56 * 1024 * 1024


def _params(n_axes):
    return pltpu.CompilerParams(
        dimension_semantics=("arbitrary",) * n_axes, vmem_limit_bytes=VMEM_LIMIT)


def _resident(shape, index_map):
    return pl.BlockSpec(shape, index_map, pipeline_mode=pl.Buffered(1))


def _silu(v):
    return v * (1.0 / (1.0 + jnp.exp(-v)))


def _ada_kernel(c_ref, w_ref, b_ref, o_ref):
    o_ref[0] = jnp.dot(_silu(c_ref[...]), w_ref[0], preferred_element_type=F32) + b_ref[0]


def _ada_mod(c, ada_w, ada_b):
    depth, d, n = ada_w.shape
    cp = jnp.pad(c, ((0, SUBLANES - c.shape[0]), (0, 0)))
    tn = ADA_COL_TILE
    return pl.pallas_call(
        _ada_kernel,
        grid=(depth, n // tn),
        in_specs=[
            pl.BlockSpec((SUBLANES, d), lambda l, j: (0, 0)),
            pl.BlockSpec((1, d, tn), lambda l, j: (l, 0, j)),
            pl.BlockSpec((1, 1, tn), lambda l, j: (l, 0, j)),
        ],
        out_specs=pl.BlockSpec((1, SUBLANES, tn), lambda l, j: (l, 0, j)),
        out_shape=jax.ShapeDtypeStruct((depth, SUBLANES, n), F32),
        compiler_params=_params(2),
        name="ada_mod",
    )(cp, ada_w, ada_b.reshape(depth, 1, n))


def _norm_mod(x, g, shift, scale):
    y = x * lax.rsqrt(jnp.mean(x * x, axis=-1, keepdims=True) + NORM_EPS)
    return (y * g) * (1.0 + scale) + shift


def _att_proj_kernel(x_ref, g_ref, sh_ref, sc_ref, cos_ref, sina_ref, sinb_ref, w_ref, o_ref):
    sb_w = N_SB_HEADS * HEAD_DIM
    for r in range(x_ref.shape[0] // ROW_TILE):
        rows = slice(r * ROW_TILE, (r + 1) * ROW_TILE)
        h = _norm_mod(x_ref[rows, :], g_ref[...], sh_ref[0], sc_ref[0]).astype(BF16)
        cos, sina, sinb = cos_ref[rows, :], sina_ref[rows, :], sinb_ref[rows, :]
        for c in range(3 * ATT_W // COL_CHUNK):
            lo = c * COL_CHUNK
            y = jnp.dot(h, w_ref[:, lo:lo + COL_CHUNK], preferred_element_type=F32)
            is_q = lo < ATT_W
            rotated = (lo % ATT_W) >= sb_w and lo < 2 * ATT_W
            for s in range(COL_CHUNK // LANES):
                ys = y[:, s * LANES:(s + 1) * LANES]
                if rotated:
                    ys = (ys * cos + pltpu.roll(ys, LANES - HEAD_DIM // 2, 1) * sina
                          + pltpu.roll(ys, HEAD_DIM // 2, 1) * sinb)
                if is_q:
                    ys = ys * Q_SCALE
                o_ref[rows, lo + s * LANES:lo + (s + 1) * LANES] = ys.astype(BF16)


def _ret_proj_kernel(x_ref, g_ref, sh_ref, sc_ref, cos_ref, sin_ref, w_ref, o_ref):
    n = 2 * RET_QK_W + 2 * RET_V_W
    half = RET_DK // 2
    for r in range(x_ref.shape[0] // ROW_TILE):
        rows = slice(r * ROW_TILE, (r + 1) * ROW_TILE)
        h = _norm_mod(x_ref[rows, :], g_ref[...], sh_ref[0], sc_ref[0]).astype(BF16)
        cos, sin = cos_ref[rows, :], sin_ref[rows, :]
        for c in range(n // COL_CHUNK):
            lo = c * COL_CHUNK
            y = jnp.dot(h, w_ref[:, lo:lo + COL_CHUNK], preferred_element_type=F32)
            if lo < 2 * RET_QK_W:
                mul = 1.0 if lo < RET_QK_W else RET_DK ** -0.5
                for hd in range(COL_CHUNK // RET_DK):
                    x1 = y[:, hd * RET_DK:hd * RET_DK + half]
                    x2 = y[:, hd * RET_DK + half:(hd + 1) * RET_DK]
                    o1 = (x1 * cos - x2 * sin) * mul
                    o2 = (x2 * cos + x1 * sin) * mul
                    o_ref[rows, lo + hd * RET_DK:lo + hd * RET_DK + half] = o1.astype(BF16)
                    o_ref[rows, lo + hd * RET_DK + half:lo + (hd + 1) * RET_DK] = o2.astype(BF16)
            else:
                o_ref[rows, lo:lo + COL_CHUNK] = y.astype(BF16)


def _proj(kernel, x, gain, shift, scale, tables, w, seq, name):
    t, d = x.shape
    n = w.shape[1]
    tm = PROJ_SLABS * ROW_TILE
    per_seq = seq // tm
    row = lambda i: (i, 0)
    batch = lambda i: (i // per_seq, 0, 0)
    pos = lambda i: (i % per_seq, 0)
    return pl.pallas_call(
        kernel,
        grid=(t // tm,),
        in_specs=[
            pl.BlockSpec((tm, d), row),
            pl.BlockSpec((1, d), lambda i: (0, 0)),
            pl.BlockSpec((1, 1, d), batch),
            pl.BlockSpec((1, 1, d), batch),
            *[pl.BlockSpec((tm, LANES), pos) for _ in tables],
            _resident((d, n), lambda i: (0, 0)),
        ],
        out_specs=pl.BlockSpec((tm, n), row),
        out_shape=jax.ShapeDtypeStruct((t, n), BF16),
        compiler_params=_params(1),
        name=name,
    )(x, gain, shift, scale, *tables, w)


def _pair_lanes(p):
    return slice(p * LANES, (p + 1) * LANES)


def _fill_vt(v_ref, vt_ref):
    t = ATT_TILE
    for p in range(v_ref.shape[1] // LANES):
        for n in range(v_ref.shape[0] // t):
            vt_ref[p, n] = v_ref[n * t:(n + 1) * t, _pair_lanes(p)].T


def _tile_masks(strict):
    t = ATT_TILE
    key = lax.broadcasted_iota(jnp.int32, (t, 2 * t), 0)
    qry = lax.broadcasted_iota(jnp.int32, (t, 2 * t), 1) % t
    chan0 = lax.broadcasted_iota(jnp.int32, (LANES, 1), 0) < HEAD_DIM
    return (key < qry) if strict else (key <= qry), chan0


def _load_super_tile(q_ref, p, i, chan0):
    t = ATT_TILE
    base = pl.multiple_of(i * 2 * t, 2 * t)
    cells = []
    for c in range(2):
        q_t = q_ref[pl.ds(base + c * t, t), _pair_lanes(p)].T
        zero = jnp.zeros_like(q_t)
        cells += [jnp.where(chan0, q_t, zero), jnp.where(chan0, zero, q_t)]
    return jnp.concatenate(cells, axis=1), jnp.concatenate(cells[2:], axis=1)


def _store_super_tile(o_ref, p, i, acc_t, chan0):
    t = ATT_TILE
    base = pl.multiple_of(i * 2 * t, 2 * t)
    for cell in range(2):
        lo = cell * 2 * t
        o_t = jnp.where(chan0, acc_t[:, lo:lo + t], acc_t[:, lo + t:lo + 2 * t])
        o_ref[pl.ds(base + cell * t, t), _pair_lanes(p)] = o_t.T.astype(BF16)


def _scores(k_ref, p, j, q_cols, blocks=1):
    t = ATT_TILE
    return jnp.dot(k_ref[pl.ds(pl.multiple_of(j * t, t), blocks * t), _pair_lanes(p)], q_cols,
                   preferred_element_type=F32)


def _stage_pair_of_blocks(stage_ref, k_ref, half, j, tiles):
    t = ATT_TILE
    for p, (q_cols, _) in enumerate(tiles):
        z = _scores(k_ref, p, j, q_cols, blocks=2)
        stage_ref[half + 2 * p] = z[:t]
        stage_ref[half + 2 * p + 1] = z[t:]


def _sb_kernel(q_ref, k_ref, v_ref, o_ref, vt_ref, acc_ref, z_ref):
    t = ATT_TILE
    w = 2 * t
    seq = q_ref.shape[0]
    pairs = range(q_ref.shape[1] // LANES)
    per_half = 2 * len(pairs)
    strict, chan0 = _tile_masks(True)
    from_key = -(lax.broadcasted_iota(jnp.int32, (t, t), 1)
                 >= lax.broadcasted_iota(jnp.int32, (t, t), 0)).astype(BF16)
    _fill_vt(v_ref, vt_ref)

    def block(p, j, slot, lanes, run, mask):
        z = z_ref[slot, :, lanes]
        neg_l = jnp.maximum(z, 0.0) + jnp.log2(1.0 + jnp.exp2(-jnp.abs(z)))
        if mask is not None:
            neg_l = jnp.where(mask, neg_l, 0.0)
        since = jnp.dot(from_key, neg_l.astype(BF16), preferred_element_type=F32)
        a = jnp.exp2(z_ref[slot, :, lanes] + since)
        if mask is not None:
            a = jnp.where(mask, a, 0.0)
        part = jnp.dot(vt_ref[p, j], a.astype(BF16), preferred_element_type=F32)
        return run + since[0:1, :], part * jnp.exp2(run)

    def q_super(i, _):
        tiles = [_load_super_tile(q_ref, p, i, chan0) for p in pairs]
        mask = jnp.concatenate([strict, jnp.ones((t, w), jnp.bool_)], axis=1)
        cell1 = slice(w, 2 * w)
        diag = lax.rem(i, 2) * per_half
        for p in pairs:
            q2, qb = tiles[p]
            z_ref[diag + 2 * p + 1, :, cell1] = _scores(k_ref, p, 2 * i + 1, qb)
            z_ref[diag + 2 * p] = _scores(k_ref, p, 2 * i, q2)
        runs = []
        for p in pairs:
            run_b, part = block(p, 2 * i + 1, diag + 2 * p + 1, cell1, jnp.zeros((1, w), F32),
                                strict)
            acc_ref[p, :, w:] = part
            run = jnp.concatenate([jnp.zeros((1, w), F32), run_b], axis=1)
            run, part = block(p, 2 * i, diag + 2 * p, slice(None), run, mask)
            acc_ref[p, :, :w] = part[:, :w]
            acc_ref[p, :, w:] += part[:, w:]
            runs.append(run)

        def past(s, runs):
            j = 2 * (i - 1 - s)
            half = lax.rem(s, 2) * per_half
            _stage_pair_of_blocks(z_ref, k_ref, half, j, tiles)
            out = []
            for p in pairs:
                run, part_b = block(p, j + 1, half + 2 * p + 1, slice(None), runs[p], None)
                run, part_a = block(p, j, half + 2 * p, slice(None), run, None)
                acc_ref[p] += part_b + part_a
                out.append(run)
            return tuple(out)

        lax.fori_loop(0, i, past, tuple(runs))
        for p in pairs:
            _store_super_tile(o_ref, p, i, acc_ref[p], chan0)
        return 0

    lax.fori_loop(0, seq // w, q_super, 0)


def _pair_specs(seq, q_col, k_col, v_col):
    width = PAIRS_PER_STEP * LANES
    return [
        pl.BlockSpec((seq, width), lambda b, p: (b, q_col + p)),
        pl.BlockSpec((seq, width), lambda b, p: (b, k_col + p)),
        pl.BlockSpec((seq, width), lambda b, p: (b, v_col + p)),
    ]


def _with_weight_casts(kernel, n_cast):
    def wrapped(q_ref, k_ref, v_ref, *refs):
        srcs, o_ref, dsts = refs[:n_cast], refs[n_cast], refs[n_cast + 1:2 * n_cast + 1]
        for src, dst in zip(srcs, dsts):
            dst[...] = src[...].astype(BF16)
        kernel(q_ref, k_ref, v_ref, o_ref, *refs[2 * n_cast + 1:])
    return wrapped


def _attention_call(kernel, qkv, weights, batch, seq, n_heads, first_head, extra_scratch, name):
    width = PAIRS_PER_STEP * LANES
    groups = n_heads * HEAD_DIM // width
    third = ATT_W // width
    first = first_head * HEAD_DIM // width
    steps = batch * groups
    w_specs = [pl.BlockSpec((w.shape[0] // steps, w.shape[1]), lambda b, p: (b * groups + p, 0))
               for w in weights]
    out, *cast = pl.pallas_call(
        _with_weight_casts(kernel, len(weights)),
        grid=(batch, groups),
        in_specs=_pair_specs(seq, first, third + first, 2 * third + first) + w_specs,
        out_specs=[pl.BlockSpec((seq, width), lambda b, p: (b, p))] + w_specs,
        out_shape=[jax.ShapeDtypeStruct((batch * seq, groups * width), BF16)]
        + [jax.ShapeDtypeStruct(w.shape, BF16) for w in weights],
        scratch_shapes=[pltpu.VMEM((PAIRS_PER_STEP, seq // ATT_TILE, LANES, ATT_TILE), BF16),
                        pltpu.VMEM((PAIRS_PER_STEP, LANES, 4 * ATT_TILE), F32),
                        *extra_scratch],
        compiler_params=_params(2),
        name=name,
    )(qkv, qkv, qkv, *weights)
    return out, cast


def _sb_attention(qkv, weights, batch, seq):
    staged = pltpu.VMEM((4 * PAIRS_PER_STEP, ATT_TILE, 4 * ATT_TILE), F32)
    return _attention_call(_sb_kernel, qkv, weights, batch, seq, N_SB_HEADS, 0, [staged],
                           "sb_attention")


def _moba_kernel(q_ref, k_ref, v_ref, o_ref, vt_ref, acc_ref, sel_ref, s_ref):
    t = ATT_TILE
    w = 2 * t
    seq = q_ref.shape[0]
    nb = seq // MOBA_BLOCK
    pairs = range(q_ref.shape[1] // LANES)
    per_half = 2 * len(pairs)
    causal, chan0 = _tile_masks(False)
    blk = lax.broadcasted_iota(jnp.int32, (nb, 2 * w), 0)
    cell = lax.broadcasted_iota(jnp.int32, (nb, 2 * w), 1) // w
    blk_f = blk.astype(F32)
    kmean = [(jnp.sum(k_ref[:, _pair_lanes(p)].astype(F32).reshape(nb, MOBA_BLOCK, LANES), axis=1)
              * (1.0 / MOBA_BLOCK)).astype(BF16) for p in pairs]
    _fill_vt(v_ref, vt_ref)


    def select(p, i, q2):
        gate = jnp.dot(kmean[p], q2, preferred_element_type=F32)
        past_blk = blk < 2 * i + cell
        gate = jnp.where(past_blk, gate, NEG)
        sel = jnp.zeros((nb, 2 * w), F32)
        for _r in range(MOBA_TOPK):
            top = jnp.max(gate, axis=0, keepdims=True)
            idx = jnp.min(jnp.where(gate == top, blk_f, float(nb)), axis=0, keepdims=True)
            hit = blk_f == idx
            sel = jnp.where(hit & past_blk, 1.0, sel)
            gate = jnp.where(hit, -jnp.inf, gate)
        sel_ref[p] = sel

    def own_blocks(p, i, raw_a, raw_b):
        s = jnp.where(causal, raw_b, NEG)
        m_b = jnp.max(s, axis=0, keepdims=True)
        pr = jnp.exp2(s - m_b)
        acc_ref[p, :, :w] = jnp.zeros((LANES, w), F32)
        acc_ref[p, :, w:] = jnp.dot(vt_ref[p, 2 * i + 1], pr.astype(BF16),
                                    preferred_element_type=F32)
        m = jnp.concatenate([jnp.full((1, w), -jnp.inf, F32), m_b], axis=1)
        l = jnp.concatenate([jnp.zeros((1, w), F32), jnp.sum(pr, axis=0, keepdims=True)], axis=1)
        keep = jnp.concatenate(
            [causal, jnp.broadcast_to(sel_ref[p, pl.ds(2 * i, 1), w:] > 0.0, (t, w))], axis=1)
        s = jnp.where(keep, raw_a, NEG)
        m_new = jnp.maximum(m, jnp.max(s, axis=0, keepdims=True))
        alpha = jnp.exp2(m - m_new)
        pr = jnp.exp2(s - m_new)
        acc_ref[p] = alpha * acc_ref[p] + jnp.dot(
            vt_ref[p, 2 * i], pr.astype(BF16), preferred_element_type=F32)
        return m_new, alpha * l + jnp.sum(pr, axis=0, keepdims=True)

    def past_pair(p, j, raw_a, raw_b, m_old, l_old):
        s_a = jnp.where(sel_ref[p, pl.ds(j, 1), :] > 0.0, raw_a, NEG)
        s_b = jnp.where(sel_ref[p, pl.ds(j + 1, 1), :] > 0.0, raw_b, NEG)
        m_new = jnp.maximum(m_old, jnp.maximum(jnp.max(s_a, axis=0, keepdims=True),
                                               jnp.max(s_b, axis=0, keepdims=True)))
        alpha = jnp.exp2(m_old - m_new)
        p_a = jnp.exp2(s_a - m_new)
        p_b = jnp.exp2(s_b - m_new)
        part = (jnp.dot(vt_ref[p, j], p_a.astype(BF16), preferred_element_type=F32)
                + jnp.dot(vt_ref[p, j + 1], p_b.astype(BF16), preferred_element_type=F32))
        acc_ref[p] = alpha * acc_ref[p] + part
        l_new = (alpha * l_old + jnp.sum(p_a, axis=0, keepdims=True)
                 + jnp.sum(p_b, axis=0, keepdims=True))
        return m_new, l_new

    def q_super(i, _):
        tiles = [_load_super_tile(q_ref, p, i, chan0) for p in pairs]
        own = lax.rem(i, 2) * per_half
        for p in pairs:
            q2, qb = tiles[p]
            s_ref[own + 2 * p + 1, :, w:] = _scores(k_ref, p, 2 * i + 1, qb)
            s_ref[own + 2 * p] = _scores(k_ref, p, 2 * i, q2)
            select(p, i, q2)
        stats = tuple(own_blocks(p, i, s_ref[own + 2 * p], s_ref[own + 2 * p + 1, :, w:])
                      for p in pairs)

        def past(jj, stats):
            half = lax.rem(jj, 2) * per_half
            _stage_pair_of_blocks(s_ref, k_ref, half, 2 * jj, tiles)
            return tuple(past_pair(p, 2 * jj, s_ref[half + 2 * p], s_ref[half + 2 * p + 1],
                                   *stats[p]) for p in pairs)

        stats = lax.fori_loop(0, i, past, stats)
        for p in pairs:
            _store_super_tile(o_ref, p, i, acc_ref[p] / stats[p][1], chan0)
        return 0

    lax.fori_loop(0, seq // w, q_super, 0)


def _moba_attention(qkv, weights, batch, seq):
    sel = pltpu.VMEM((PAIRS_PER_STEP, seq // MOBA_BLOCK, 4 * ATT_TILE), F32)
    staged = pltpu.VMEM((4 * PAIRS_PER_STEP, ATT_TILE, 4 * ATT_TILE), F32)
    return _attention_call(_moba_kernel, qkv, weights, batch, seq, N_MOBA_HEADS, N_SB_HEADS,
                           [sel, staged], "moba_attention")


def _ret_kernel(q_ref, k_ref, v_ref, g_ref, gn_ref, o_ref, state_ref, decay_ref):
    c = RET_CHUNK
    heads = range(RET_HEADS_PER_STEP)
    n_row = lax.broadcasted_iota(jnp.int32, (c, 1), 0).astype(F32)
    log_g, xi, zeta, chunk_decay = [], [], [], []
    for hh in heads:
        head = (pl.program_id(1) * RET_HEADS_PER_STEP + hh + 5).astype(F32)
        lg = jnp.log(1.0 - jnp.exp2(-jnp.full((1, 1), head, F32)))
        log_g.append(lg)
        xi.append(jnp.exp(lg * (n_row + 1.0)))
        zeta.append(jnp.exp(lg * (c - 1.0 - n_row)))
        chunk_decay.append(jnp.exp(lg * c))

    @pl.when(pl.program_id(2) == 0)
    def _():
        diff = (lax.broadcasted_iota(jnp.int32, (c, c), 0)
                - lax.broadcasted_iota(jnp.int32, (c, c), 1)).astype(F32)
        for hh in heads:
            decay_ref[hh] = jnp.where(diff >= 0, jnp.exp(log_g[hh] * jnp.maximum(diff, 0.0)), 0.0)
        state_ref[...] = jnp.zeros_like(state_ref)

    def chunk(i, _):
        rows = pl.ds(pl.multiple_of(i * c, c), c)
        for hh in heads:
            qk_cols = slice(hh * RET_DK, (hh + 1) * RET_DK)
            v_cols = slice(hh * RET_DV, (hh + 1) * RET_DV)
            q, k, v = q_ref[rows, qk_cols], k_ref[rows, qk_cols], v_ref[rows, v_cols]
            inner = lax.dot_general(q, k, (((1,), (1,)), ((), ())),
                                    preferred_element_type=F32) * decay_ref[hh]
            state = state_ref[hh]
            o = (jnp.dot(inner.astype(BF16), v, preferred_element_type=F32)
                 + jnp.dot(q, state.astype(BF16), preferred_element_type=F32) * xi[hh])
            kz = (k.astype(F32) * zeta[hh]).astype(BF16)
            state_ref[hh] = state * chunk_decay[hh] + lax.dot_general(
                kz, v, (((0,), (0,)), ((), ())), preferred_element_type=F32)
            mu = jnp.mean(o, axis=-1, keepdims=True)
            var = jnp.mean(jnp.square(o - mu), axis=-1, keepdims=True)
            on = (o - mu) * lax.rsqrt(var + GN_EPS) * gn_ref[:, v_cols]
            o_ref[rows, v_cols] = (_silu(g_ref[rows, v_cols].astype(F32)) * on).astype(BF16)
        return 0

    lax.fori_loop(0, q_ref.shape[0] // c, chunk, 0)


def _retention(proj, gn_gain, batch, seq):
    hps = RET_HEADS_PER_STEP
    tile = RET_SEQ_TILE
    tiles = seq // tile
    qk_w, v_w = hps * RET_DK, hps * RET_DV
    k0 = RET_QK_W // qk_w
    v0 = 2 * RET_QK_W // v_w
    g0 = v0 + RET_V_W // v_w
    row = lambda b, h, s: b * tiles + s
    return pl.pallas_call(
        _ret_kernel,
        grid=(batch, N_RET_HEADS // hps, tiles),
        in_specs=[
            pl.BlockSpec((tile, qk_w), lambda b, h, s: (row(b, h, s), h)),
            pl.BlockSpec((tile, qk_w), lambda b, h, s: (row(b, h, s), k0 + h)),
            pl.BlockSpec((tile, v_w), lambda b, h, s: (row(b, h, s), v0 + h)),
            pl.BlockSpec((tile, v_w), lambda b, h, s: (row(b, h, s), g0 + h)),
            pl.BlockSpec((1, v_w), lambda b, h, s: (0, h)),
        ],
        out_specs=pl.BlockSpec((tile, v_w), lambda b, h, s: (row(b, h, s), h)),
        out_shape=jax.ShapeDtypeStruct((batch * seq, RET_V_W), BF16),
        scratch_shapes=[pltpu.VMEM((hps, RET_DK, RET_DV), F32),
                        pltpu.VMEM((hps, RET_CHUNK, RET_CHUNK), F32)],
        compiler_params=_params(3),
        name="retention",
    )(proj, proj, proj, proj, gn_gain)


def _mix_ffn_kernel(*refs, n_in, final_norm):
    x_ref, gate_m_ref, g_ref, sh_ref, sc_ref, gate_f_ref = refs[:6]
    a_refs, w_refs = refs[6:6 + n_in], refs[6 + n_in:6 + 2 * n_in]
    wgu_ref, wd_ref, fin_ref, o_ref = refs[6 + 2 * n_in:]
    for r in range(x_ref.shape[0] // ROW_TILE):
        rows = slice(r * ROW_TILE, (r + 1) * ROW_TILE)
        y = jnp.dot(a_refs[0][rows, :], w_refs[0][...], preferred_element_type=F32)
        for a_ref, w_ref in zip(a_refs[1:], w_refs[1:]):
            y += jnp.dot(a_ref[rows, :], w_ref[...], preferred_element_type=F32)
        x = x_ref[rows, :] + gate_m_ref[0] * y
        h = _norm_mod(x, g_ref[...], sh_ref[0], sc_ref[0]).astype(BF16)
        y = None
        for lo, hi in FF_CHUNKS:
            gt = jnp.dot(h, wgu_ref[:, lo:hi], preferred_element_type=F32)
            up = jnp.dot(h, wgu_ref[:, D_FF + lo:D_FF + hi], preferred_element_type=F32)
            part = jnp.dot((_silu(gt) * up).astype(BF16), wd_ref[lo:hi, :],
                           preferred_element_type=F32)
            y = part if y is None else y + part
        out = x + gate_f_ref[0] * y
        if final_norm:
            out = (out * lax.rsqrt(jnp.mean(out * out, axis=-1, keepdims=True) + NORM_EPS)
                   * fin_ref[...])
        o_ref[rows, :] = out


def _mix_ffn(x, gate_m, acts, w_outs, gain, shift, scale, gate_f, w_gate_up, w_down, layer,
             fin_gain, seq, final_norm, name):
    t, d = x.shape
    tm = MIX_SLABS * ROW_TILE
    per_seq = seq // tm
    row = lambda i: (i, 0)
    batch = lambda i: (i // per_seq, 0, 0)
    const = lambda i: (0, 0)
    this_layer = lambda i: (layer, 0, 0)
    return pl.pallas_call(
        functools.partial(_mix_ffn_kernel, n_in=len(acts), final_norm=final_norm),
        grid=(t // tm,),
        in_specs=[
            pl.BlockSpec((tm, d), row),
            pl.BlockSpec((1, 1, d), batch),
            pl.BlockSpec((1, d), const),
            pl.BlockSpec((1, 1, d), batch),
            pl.BlockSpec((1, 1, d), batch),
            pl.BlockSpec((1, 1, d), batch),
            *[pl.BlockSpec((tm, a.shape[1]), row) for a in acts],
            *[_resident(w.shape, const) for w in w_outs],
            _resident((None,) + w_gate_up.shape[1:], this_layer),
            _resident((None,) + w_down.shape[1:], this_layer),
            pl.BlockSpec((1, d), const),
        ],
        out_specs=pl.BlockSpec((tm, d), row),
        out_shape=jax.ShapeDtypeStruct((t, d), F32),
        compiler_params=_params(1),
        name=name,
    )(x, gate_m, gain, shift, scale, gate_f, *acts, *w_outs, w_gate_up, w_down, fin_gain)


def _rope_tables(seq, dim, tile):
    inv = ROPE_THETA ** (-jnp.arange(0, dim, 2, dtype=F32) / dim)
    hi = (jnp.arange(seq // ROPE_SIDE, dtype=jnp.int32) * ROPE_SIDE).astype(F32)[:, None] * inv
    lo = jnp.arange(ROPE_SIDE, dtype=jnp.int32).astype(F32)[:, None] * inv
    ch, sh, cl, sl = jnp.cos(hi)[:, None], jnp.sin(hi)[:, None], jnp.cos(lo)[None], jnp.sin(lo)[None]
    cos = (ch * cl - sh * sl).reshape(seq, dim // 2)
    sin = (sh * cl + ch * sl).reshape(seq, dim // 2)
    return jnp.tile(cos, (1, tile)), jnp.tile(sin, (1, tile))


def kernel(x, c, ada_w, ada_b, norm_gains, att_w_qkv, att_w_o, ret_w_in, ret_gn, ret_w_o,
           ffn_w_gate_up, ffn_w_down, final_norm):
    batch, seq, d = x.shape
    depth = ada_w.shape[0]
    assert d == D_MODEL and batch <= SUBLANES
    assert seq % (PROJ_SLABS * ROW_TILE) == 0 and seq % (MIX_SLABS * ROW_TILE) == 0
    assert seq % (2 * ATT_TILE) == 0 and seq % RET_SEQ_TILE == 0

    mod = _ada_mod(c, ada_w, ada_b)[:, :batch].reshape(depth, batch, 1, 6, d)
    xt = x.reshape(batch * seq, d)

    half = HEAD_DIM // 2
    cos_a, sin_a = _rope_tables(seq, HEAD_DIM, LANES // half)
    first_half = (jnp.arange(LANES) % HEAD_DIM) < half
    att_tables = (cos_a, jnp.where(first_half, -sin_a, 0.0), jnp.where(first_half, 0.0, sin_a))
    ret_tables = _rope_tables(seq, RET_DK, 1)
    assert depth == 2 and att_w_qkv.shape[0] == 1 and ret_w_in.shape[0] == 1
    rows = lambda w: w.reshape(-1, w.shape[-1])

    for layer in range(depth):
        shift_m, scale_m, gate_m, shift_f, scale_f, gate_f = (mod[layer, :, :, i] for i in range(6))
        gains = norm_gains[layer]
        if layer % 2 == 0:
            qkv = _proj(_att_proj_kernel, xt, gains[0:1], shift_m, scale_m, att_tables,
                        att_w_qkv[0].astype(BF16), seq, "att_proj")
            sb, (w_gate_up, w_ret_o, w_att_o) = _sb_attention(
                qkv, [rows(ffn_w_gate_up), rows(ret_w_o), rows(att_w_o)], batch, seq)
            mb, (w_down, w_ret_in) = _moba_attention(
                qkv, [rows(ffn_w_down), rows(ret_w_in)], batch, seq)
            w_gate_up = w_gate_up.reshape(ffn_w_gate_up.shape)
            w_down = w_down.reshape(ffn_w_down.shape)
            split = N_SB_HEADS * HEAD_DIM
            acts, w_outs = (sb, mb), (w_att_o[:split], w_att_o[split:])
        else:
            proj = _proj(_ret_proj_kernel, xt, gains[0:1], shift_m, scale_m, ret_tables,
                         w_ret_in, seq, "ret_proj")
            acts = (_retention(proj, ret_gn[0].reshape(1, RET_V_W), batch, seq),)
            w_outs = (w_ret_o,)
        xt = _mix_ffn(xt, gate_m, acts, w_outs, gains[1:2], shift_f, scale_f, gate_f, w_gate_up,
                      w_down, layer, final_norm.reshape(1, d), seq, layer == depth - 1,
                      "mix_ffn%d" % layer)
    return xt.reshape(batch, seq, d)
```

```python
import functools
import math

import jax
import jax.numpy as jnp
from jax import lax
from jax.experimental import pallas as pl
from jax.experimental.pallas import tpu as pltpu

F32 = jnp.float32
BF16 = jnp.bfloat16

D_MODEL = 1024
HEAD_DIM = 64
N_SB_HEADS = 8
N_MOBA_HEADS = 8
ATT_W = (N_SB_HEADS + N_MOBA_HEADS) * HEAD_DIM
MOBA_BLOCK = 256
MOBA_TOPK = 3
N_RET_HEADS = 4
RET_DK = D_MODEL // N_RET_HEADS
RET_DV = 2 * RET_DK
RET_QK_W = N_RET_HEADS * RET_DK
RET_V_W = N_RET_HEADS * RET_DV
D_FF = -(-8 * D_MODEL // 768) * 256
ROPE_THETA = 10000.0
ROPE_SIDE = 64
NORM_EPS = 1e-6
GN_EPS = 1e-5
NEG = -1e30
Q_SCALE = HEAD_DIM ** -0.5 * math.log2(math.e)

LANES = 128
SUBLANES = 8
ADA_COL_TILE = 1536
ROW_TILE = 512
PROJ_SLABS = 2
MIX_SLABS = 2
COL_CHUNK = 512
MXU_WIDTH = 256
_FF_SPLIT = (D_FF // MXU_WIDTH + 1) // 2 * MXU_WIDTH
FF_CHUNKS = ((0, _FF_SPLIT), (_FF_SPLIT, D_FF))
ATT_TILE = 256
PAIRS_PER_STEP = 2
RET_CHUNK = 256
RET_SEQ_TILE = 1024
RET_HEADS_PER_STEP = 4
VMEM_LIMIT = 56 * 1024 * 1024


def _params(n_axes):
    return pltpu.CompilerParams(
        dimension_semantics=("arbitrary",) * n_axes, vmem_limit_bytes=VMEM_LIMIT)


def _resident(shape, index_map):
    return pl.BlockSpec(shape, index_map, pipeline_mode=pl.Buffered(1))


def _silu(v):
    return v * (1.0 / (1.0 + jnp.exp(-v)))


def _ada_kernel(c_ref, w_ref, b_ref, o_ref):
    o_ref[0] = jnp.dot(_silu(c_ref[...]), w_ref[0], preferred_element_type=F32) + b_ref[0]


def _ada_mod(c, ada_w, ada_b):
    depth, d, n = ada_w.shape
    cp = jnp.pad(c, ((0, SUBLANES - c.shape[0]), (0, 0)))
    tn = ADA_COL_TILE
    return pl.pallas_call(
        _ada_kernel,
        grid=(depth, n // tn),
        in_specs=[
            pl.BlockSpec((SUBLANES, d), lambda l, j: (0, 0)),
            pl.BlockSpec((1, d, tn), lambda l, j: (l, 0, j)),
            pl.BlockSpec((1, 1, tn), lambda l, j: (l, 0, j)),
        ],
        out_specs=pl.BlockSpec((1, SUBLANES, tn), lambda l, j: (l, 0, j)),
        out_shape=jax.ShapeDtypeStruct((depth, SUBLANES, n), F32),
        compiler_params=_params(2),
        name="ada_mod",
    )(cp, ada_w, ada_b.reshape(depth, 1, n))


def _norm_mod(x, g, shift, scale):
    y = x * lax.rsqrt(jnp.mean(x * x, axis=-1, keepdims=True) + NORM_EPS)
    return (y * g) * (1.0 + scale) + shift


def _att_proj_kernel(x_ref, g_ref, sh_ref, sc_ref, cos_ref, sina_ref, sinb_ref, w_ref, o_ref):
    sb_w = N_SB_HEADS * HEAD_DIM
    for r in range(x_ref.shape[0] // ROW_TILE):
        rows = slice(r * ROW_TILE, (r + 1) * ROW_TILE)
        h = _norm_mod(x_ref[rows, :], g_ref[...], sh_ref[0], sc_ref[0]).astype(BF16)
        cos, sina, sinb = cos_ref[rows, :], sina_ref[rows, :], sinb_ref[rows, :]
        for c in range(3 * ATT_W // COL_CHUNK):
            lo = c * COL_CHUNK
            y = jnp.dot(h, w_ref[:, lo:lo + COL_CHUNK], preferred_element_type=F32)
            is_q = lo < ATT_W
            rotated = (lo % ATT_W) >= sb_w and lo < 2 * ATT_W
            for s in range(COL_CHUNK // LANES):
                ys = y[:, s * LANES:(s + 1) * LANES]
                if rotated:
                    ys = (ys * cos + pltpu.roll(ys, LANES - HEAD_DIM // 2, 1) * sina
                          + pltpu.roll(ys, HEAD_DIM // 2, 1) * sinb)
                if is_q:
                    ys = ys * Q_SCALE
                o_ref[rows, lo + s * LANES:lo + (s + 1) * LANES] = ys.astype(BF16)


def _ret_proj_kernel(x_ref, g_ref, sh_ref, sc_ref, cos_ref, sin_ref, w_ref, o_ref):
    n = 2 * RET_QK_W + 2 * RET_V_W
    half = RET_DK // 2
    for r in range(x_ref.shape[0] // ROW_TILE):
        rows = slice(r * ROW_TILE, (r + 1) * ROW_TILE)
        h = _norm_mod(x_ref[rows, :], g_ref[...], sh_ref[0], sc_ref[0]).astype(BF16)
        cos, sin = cos_ref[rows, :], sin_ref[rows, :]
        for c in range(n // COL_CHUNK):
            lo = c * COL_CHUNK
            y = jnp.dot(h, w_ref[:, lo:lo + COL_CHUNK], preferred_element_type=F32)
            if lo < 2 * RET_QK_W:
                mul = 1.0 if lo < RET_QK_W else RET_DK ** -0.5
                for hd in range(COL_CHUNK // RET_DK):
                    x1 = y[:, hd * RET_DK:hd * RET_DK + half]
                    x2 = y[:, hd * RET_DK + half:(hd + 1) * RET_DK]
                    o1 = (x1 * cos - x2 * sin) * mul
                    o2 = (x2 * cos + x1 * sin) * mul
                    o_ref[rows, lo + hd * RET_DK:lo + hd * RET_DK + half] = o1.astype(BF16)
                    o_ref[rows, lo + hd * RET_DK + half:lo + (hd + 1) * RET_DK] = o2.astype(BF16)
            else:
                o_ref[rows, lo:lo + COL_CHUNK] = y.astype(BF16)


def _proj(kernel, x, gain, shift, scale, tables, w, seq, name):
    t, d = x.shape
    n = w.shape[1]
    tm = PROJ_SLABS * ROW_TILE
    per_seq = seq // tm
    row = lambda i: (i, 0)
    batch = lambda i: (i // per_seq, 0, 0)
    pos = lambda i: (i % per_seq, 0)
    return pl.pallas_call(
        kernel,
        grid=(t // tm,),
        in_specs=[
            pl.BlockSpec((tm, d), row),
            pl.BlockSpec((1, d), lambda i: (0, 0)),
            pl.BlockSpec((1, 1, d), batch),
            pl.BlockSpec((1, 1, d), batch),
            *[pl.BlockSpec((tm, LANES), pos) for _ in tables],
            _resident((d, n), lambda i: (0, 0)),
        ],
        out_specs=pl.BlockSpec((tm, n), row),
        out_shape=jax.ShapeDtypeStruct((t, n), BF16),
        compiler_params=_params(1),
        name=name,
    )(x, gain, shift, scale, *tables, w)


def _pair_lanes(p):
    return slice(p * LANES, (p + 1) * LANES)


def _fill_vt(v_ref, vt_ref):
    t = ATT_TILE
    for p in range(v_ref.shape[1] // LANES):
        for n in range(v_ref.shape[0] // t):
            vt_ref[p, n] = v_ref[n * t:(n + 1) * t, _pair_lanes(p)].T


def _tile_masks(strict):
    t = ATT_TILE
    key = lax.broadcasted_iota(jnp.int32, (t, 2 * t), 0)
    qry = lax.broadcasted_iota(jnp.int32, (t, 2 * t), 1) % t
    chan0 = lax.broadcasted_iota(jnp.int32, (LANES, 1), 0) < HEAD_DIM
    return (key < qry) if strict else (key <= qry), chan0


def _load_super_tile(q_ref, p, i, chan0):
    t = ATT_TILE
    base = pl.multiple_of(i * 2 * t, 2 * t)
    cells = []
    for c in range(2):
        q_t = q_ref[pl.ds(base + c * t, t), _pair_lanes(p)].T
        zero = jnp.zeros_like(q_t)
        cells += [jnp.where(chan0, q_t, zero), jnp.where(chan0, zero, q_t)]
    return jnp.concatenate(cells, axis=1), jnp.concatenate(cells[2:], axis=1)


def _store_super_tile(o_ref, p, i, acc_t, chan0):
    t = ATT_TILE
    base = pl.multiple_of(i * 2 * t, 2 * t)
    for cell in range(2):
        lo = cell * 2 * t
        o_t = jnp.where(chan0, acc_t[:, lo:lo + t], acc_t[:, lo + t:lo + 2 * t])
        o_ref[pl.ds(base + cell * t, t), _pair_lanes(p)] = o_t.T.astype(BF16)


def _scores(k_ref, p, j, q_cols, blocks=1):
    t = ATT_TILE
    return jnp.dot(k_ref[pl.ds(pl.multiple_of(j * t, t), blocks * t), _pair_lanes(p)], q_cols,
                   preferred_element_type=F32)


def _stage_pair_of_blocks(stage_ref, k_ref, half, j, tiles):
    t = ATT_TILE
    for p, (q_cols, _) in enumerate(tiles):
        z = _scores(k_ref, p, j, q_cols, blocks=2)
        stage_ref[half + 2 * p] = z[:t]
        stage_ref[half + 2 * p + 1] = z[t:]


def _sb_kernel(q_ref, k_ref, v_ref, o_ref, vt_ref, acc_ref, z_ref):
    t = ATT_TILE
    w = 2 * t
    seq = q_ref.shape[0]
    pairs = range(q_ref.shape[1] // LANES)
    per_half = 2 * len(pairs)
    strict, chan0 = _tile_masks(True)
    from_key = -(lax.broadcasted_iota(jnp.int32, (t, t), 1)
                 >= lax.broadcasted_iota(jnp.int32, (t, t), 0)).astype(BF16)
    _fill_vt(v_ref, vt_ref)

    def block(p, j, slot, lanes, run, mask):
        z = z_ref[slot, :, lanes]
        neg_l = jnp.maximum(z, 0.0) + jnp.log2(1.0 + jnp.exp2(-jnp.abs(z)))
        if mask is not None:
            neg_l = jnp.where(mask, neg_l, 0.0)
        since = jnp.dot(from_key, neg_l.astype(BF16), preferred_element_type=F32)
        a = jnp.exp2(z_ref[slot, :, lanes] + since)
        if mask is not None:
            a = jnp.where(mask, a, 0.0)
        part = jnp.dot(vt_ref[p, j], a.astype(BF16), preferred_element_type=F32)
        return run + since[0:1, :], part * jnp.exp2(run)

    def q_super(i, _):
        tiles = [_load_super_tile(q_ref, p, i, chan0) for p in pairs]
        mask = jnp.concatenate([strict, jnp.ones((t, w), jnp.bool_)], axis=1)
        cell1 = slice(w, 2 * w)
        diag = lax.rem(i, 2) * per_half
        for p in pairs:
            q2, qb = tiles[p]
            z_ref[diag + 2 * p + 1, :, cell1] = _scores(k_ref, p, 2 * i + 1, qb)
            z_ref[diag + 2 * p] = _scores(k_ref, p, 2 * i, q2)
        runs = []
        for p in pairs:
            run_b, part = block(p, 2 * i + 1, diag + 2 * p + 1, cell1, jnp.zeros((1, w), F32),
                                strict)
            acc_ref[p, :, w:] = part
            run = jnp.concatenate([jnp.zeros((1, w), F32), run_b], axis=1)
            run, part = block(p, 2 * i, diag + 2 * p, slice(None), run, mask)
            acc_ref[p, :, :w] = part[:, :w]
            acc_ref[p, :, w:] += part[:, w:]
            runs.append(run)

        def past(s, runs):
            j = 2 * (i - 1 - s)
            half = lax.rem(s, 2) * per_half
            _stage_pair_of_blocks(z_ref, k_ref, half, j, tiles)
            out = []
            for p in pairs:
                run, part_b = block(p, j + 1, half + 2 * p + 1, slice(None), runs[p], None)
                run, part_a = block(p, j, half + 2 * p, slice(None), run, None)
                acc_ref[p] += part_b + part_a
                out.append(run)
            return tuple(out)

        lax.fori_loop(0, i, past, tuple(runs))
        for p in pairs:
            _store_super_tile(o_ref, p, i, acc_ref[p], chan0)
        return 0

    lax.fori_loop(0, seq // w, q_super, 0)


def _pair_specs(seq, q_col, k_col, v_col):
    width = PAIRS_PER_STEP * LANES
    return [
        pl.BlockSpec((seq, width), lambda b, p: (b, q_col + p)),
        pl.BlockSpec((seq, width), lambda b, p: (b, k_col + p)),
        pl.BlockSpec((seq, width), lambda b, p: (b, v_col + p)),
    ]


def _with_weight_casts(kernel, n_cast):
    def wrapped(q_ref, k_ref, v_ref, *refs):
        srcs, o_ref, dsts = refs[:n_cast], refs[n_cast], refs[n_cast + 1:2 * n_cast + 1]
        for src, dst in zip(srcs, dsts):
            dst[...] = src[...].astype(BF16)
        kernel(q_ref, k_ref, v_ref, o_ref, *refs[2 * n_cast + 1:])
    return wrapped


def _attention_call(kernel, qkv, weights, batch, seq, n_heads, first_head, extra_scratch, name):
    width = PAIRS_PER_STEP * LANES
    groups = n_heads * HEAD_DIM // width
    third = ATT_W // width
    first = first_head * HEAD_DIM // width
    steps = batch * groups
    w_specs = [pl.BlockSpec((w.shape[0] // steps, w.shape[1]), lambda b, p: (b * groups + p, 0))
               for w in weights]
    out, *cast = pl.pallas_call(
        _with_weight_casts(kernel, len(weights)),
        grid=(batch, groups),
        in_specs=_pair_specs(seq, first, third + first, 2 * third + first) + w_specs,
        out_specs=[pl.BlockSpec((seq, width), lambda b, p: (b, p))] + w_specs,
        out_shape=[jax.ShapeDtypeStruct((batch * seq, groups * width), BF16)]
        + [jax.ShapeDtypeStruct(w.shape, BF16) for w in weights],
        scratch_shapes=[pltpu.VMEM((PAIRS_PER_STEP, seq // ATT_TILE, LANES, ATT_TILE), BF16),
                        pltpu.VMEM((PAIRS_PER_STEP, LANES, 4 * ATT_TILE), F32),
                        *extra_scratch],
        compiler_params=_params(2),
        name=name,
    )(qkv, qkv, qkv, *weights)
    return out, cast


def _sb_attention(qkv, weights, batch, seq):
    staged = pltpu.VMEM((4 * PAIRS_PER_STEP, ATT_TILE, 4 * ATT_TILE), F32)
    return _attention_call(_sb_kernel, qkv, weights, batch, seq, N_SB_HEADS, 0, [staged],
                           "sb_attention")


def _moba_kernel(q_ref, k_ref, v_ref, o_ref, vt_ref, acc_ref, sel_ref, s_ref):
    t = ATT_TILE
    w = 2 * t
    seq = q_ref.shape[0]
    nb = seq // MOBA_BLOCK
    pairs = range(q_ref.shape[1] // LANES)
    per_half = 2 * len(pairs)
    causal, chan0 = _tile_masks(False)
    blk = lax.broadcasted_iota(jnp.int32, (nb, 2 * w), 0)
    cell = lax.broadcasted_iota(jnp.int32, (nb, 2 * w), 1) // w
    blk_f = blk.astype(F32)
    kmean = [(jnp.sum(k_ref[:, _pair_lanes(p)].astype(F32).reshape(nb, MOBA_BLOCK, LANES), axis=1)
              * (1.0 / MOBA_BLOCK)).astype(BF16) for p in pairs]
    _fill_vt(v_ref, vt_ref)


    def select(p, i, q2):
        gate = jnp.dot(kmean[p], q2, preferred_element_type=F32)
        past_blk = blk < 2 * i + cell
        gate = jnp.where(past_blk, gate, NEG)
        sel = jnp.zeros((nb, 2 * w), F32)
        for _r in range(MOBA_TOPK):
            top = jnp.max(gate, axis=0, keepdims=True)
            idx = jnp.min(jnp.where(gate == top, blk_f, float(nb)), axis=0, keepdims=True)
            hit = blk_f == idx
            sel = jnp.where(hit & past_blk, 1.0, sel)
            gate = jnp.where(hit, -jnp.inf, gate)
        sel_ref[p] = sel

    def own_blocks(p, i, raw_a, raw_b):
        s = jnp.where(causal, raw_b, NEG)
        m_b = jnp.max(s, axis=0, keepdims=True)
        pr = jnp.exp2(s - m_b)
        acc_ref[p, :, :w] = jnp.zeros((LANES, w), F32)
        acc_ref[p, :, w:] = jnp.dot(vt_ref[p, 2 * i + 1], pr.astype(BF16),
                                    preferred_element_type=F32)
        m = jnp.concatenate([jnp.full((1, w), -jnp.inf, F32), m_b], axis=1)
        l = jnp.concatenate([jnp.zeros((1, w), F32), jnp.sum(pr, axis=0, keepdims=True)], axis=1)
        keep = jnp.concatenate(
            [causal, jnp.broadcast_to(sel_ref[p, pl.ds(2 * i, 1), w:] > 0.0, (t, w))], axis=1)
        s = jnp.where(keep, raw_a, NEG)
        m_new = jnp.maximum(m, jnp.max(s, axis=0, keepdims=True))
        alpha = jnp.exp2(m - m_new)
        pr = jnp.exp2(s - m_new)
        acc_ref[p] = alpha * acc_ref[p] + jnp.dot(
            vt_ref[p, 2 * i], pr.astype(BF16), preferred_element_type=F32)
        return m_new, alpha * l + jnp.sum(pr, axis=0, keepdims=True)

    def past_pair(p, j, raw_a, raw_b, m_old, l_old):
        s_a = jnp.where(sel_ref[p, pl.ds(j, 1), :] > 0.0, raw_a, NEG)
        s_b = jnp.where(sel_ref[p, pl.ds(j + 1, 1), :] > 0.0, raw_b, NEG)
        m_new = jnp.maximum(m_old, jnp.maximum(jnp.max(s_a, axis=0, keepdims=True),
                                               jnp.max(s_b, axis=0, keepdims=True)))
        alpha = jnp.exp2(m_old - m_new)
        p_a = jnp.exp2(s_a - m_new)
        p_b = jnp.exp2(s_b - m_new)
        part = (jnp.dot(vt_ref[p, j], p_a.astype(BF16), preferred_element_type=F32)
                + jnp.dot(vt_ref[p, j + 1], p_b.astype(BF16), preferred_element_type=F32))
        acc_ref[p] = alpha * acc_ref[p] + part
        l_new = (alpha * l_old + jnp.sum(p_a, axis=0, keepdims=True)
                 + jnp.sum(p_b, axis=0, keepdims=True))
        return m_new, l_new

    def q_super(i, _):
        tiles = [_load_super_tile(q_ref, p, i, chan0) for p in pairs]
        own = lax.rem(i, 2) * per_half
        for p in pairs:
            q2, qb = tiles[p]
            s_ref[own + 2 * p + 1, :, w:] = _scores(k_ref, p, 2 * i + 1, qb)
            s_ref[own + 2 * p] = _scores(k_ref, p, 2 * i, q2)
            select(p, i, q2)
        stats = tuple(own_blocks(p, i, s_ref[own + 2 * p], s_ref[own + 2 * p + 1, :, w:])
                      for p in pairs)

        def past(jj, stats):
            half = lax.rem(jj, 2) * per_half
            _stage_pair_of_blocks(s_ref, k_ref, half, 2 * jj, tiles)
            return tuple(past_pair(p, 2 * jj, s_ref[half + 2 * p], s_ref[half + 2 * p + 1],
                                   *stats[p]) for p in pairs)

        stats = lax.fori_loop(0, i, past, stats)
        for p in pairs:
            _store_super_tile(o_ref, p, i, acc_ref[p] / stats[p][1], chan0)
        return 0

    lax.fori_loop(0, seq // w, q_super, 0)


def _moba_attention(qkv, weights, batch, seq):
    sel = pltpu.VMEM((PAIRS_PER_STEP, seq // MOBA_BLOCK, 4 * ATT_TILE), F32)
    staged = pltpu.VMEM((4 * PAIRS_PER_STEP, ATT_TILE, 4 * ATT_TILE), F32)
    return _attention_call(_moba_kernel, qkv, weights, batch, seq, N_MOBA_HEADS, N_SB_HEADS,
                           [sel, staged], "moba_attention")


def _ret_kernel(q_ref, k_ref, v_ref, g_ref, gn_ref, o_ref, state_ref, decay_ref):
    c = RET_CHUNK
    heads = range(RET_HEADS_PER_STEP)
    n_row = lax.broadcasted_iota(jnp.int32, (c, 1), 0).astype(F32)
    log_g, xi, zeta, chunk_decay = [], [], [], []
    for hh in heads:
        head = (pl.program_id(1) * RET_HEADS_PER_STEP + hh + 5).astype(F32)
        lg = jnp.log(1.0 - jnp.exp2(-jnp.full((1, 1), head, F32)))
        log_g.append(lg)
        xi.append(jnp.exp(lg * (n_row + 1.0)))
        zeta.append(jnp.exp(lg * (c - 1.0 - n_row)))
        chunk_decay.append(jnp.exp(lg * c))

    @pl.when(pl.program_id(2) == 0)
    def _():
        diff = (lax.broadcasted_iota(jnp.int32, (c, c), 0)
                - lax.broadcasted_iota(jnp.int32, (c, c), 1)).astype(F32)
        for hh in heads:
            decay_ref[hh] = jnp.where(diff >= 0, jnp.exp(log_g[hh] * jnp.maximum(diff, 0.0)), 0.0)
        state_ref[...] = jnp.zeros_like(state_ref)

    def chunk(i, _):
        rows = pl.ds(pl.multiple_of(i * c, c), c)
        for hh in heads:
            qk_cols = slice(hh * RET_DK, (hh + 1) * RET_DK)
            v_cols = slice(hh * RET_DV, (hh + 1) * RET_DV)
            q, k, v = q_ref[rows, qk_cols], k_ref[rows, qk_cols], v_ref[rows, v_cols]
            inner = lax.dot_general(q, k, (((1,), (1,)), ((), ())),
                                    preferred_element_type=F32) * decay_ref[hh]
            state = state_ref[hh]
            q_xi = (q.astype(F32) * xi[hh]).astype(BF16)
            o = (jnp.dot(inner.astype(BF16), v, preferred_element_type=F32)
                 + jnp.dot(q_xi, state.astype(BF16), preferred_element_type=F32))
            kz = (k.astype(F32) * zeta[hh]).astype(BF16)
            state_ref[hh] = state * chunk_decay[hh] + lax.dot_general(
                kz, v, (((0,), (0,)), ((), ())), preferred_element_type=F32)
            mu = jnp.mean(o, axis=-1, keepdims=True)
            var = jnp.mean(jnp.square(o - mu), axis=-1, keepdims=True)
            on = (o - mu) * lax.rsqrt(var + GN_EPS) * gn_ref[:, v_cols]
            o_ref[rows, v_cols] = (_silu(g_ref[rows, v_cols].astype(F32)) * on).astype(BF16)
        return 0

    lax.fori_loop(0, q_ref.shape[0] // c, chunk, 0)


def _retention(proj, gn_gain, batch, seq):
    hps = RET_HEADS_PER_STEP
    tile = RET_SEQ_TILE
    tiles = seq // tile
    qk_w, v_w = hps * RET_DK, hps * RET_DV
    k0 = RET_QK_W // qk_w
    v0 = 2 * RET_QK_W // v_w
    g0 = v0 + RET_V_W // v_w
    row = lambda b, h, s: b * tiles + s
    return pl.pallas_call(
        _ret_kernel,
        grid=(batch, N_RET_HEADS // hps, tiles),
        in_specs=[
            pl.BlockSpec((tile, qk_w), lambda b, h, s: (row(b, h, s), h)),
            pl.BlockSpec((tile, qk_w), lambda b, h, s: (row(b, h, s), k0 + h)),
            pl.BlockSpec((tile, v_w), lambda b, h, s: (row(b, h, s), v0 + h)),
            pl.BlockSpec((tile, v_w), lambda b, h, s: (row(b, h, s), g0 + h)),
            pl.BlockSpec((1, v_w), lambda b, h, s: (0, h)),
        ],
        out_specs=pl.BlockSpec((tile, v_w), lambda b, h, s: (row(b, h, s), h)),
        out_shape=jax.ShapeDtypeStruct((batch * seq, RET_V_W), BF16),
        scratch_shapes=[pltpu.VMEM((hps, RET_DK, RET_DV), F32),
                        pltpu.VMEM((hps, RET_CHUNK, RET_CHUNK), F32)],
        compiler_params=_params(3),
        name="retention",
    )(proj, proj, proj, proj, gn_gain)


def _mix_ffn_kernel(*refs, n_in, final_norm):
    x_ref, gate_m_ref, g_ref, sh_ref, sc_ref, gate_f_ref = refs[:6]
    a_refs, w_refs = refs[6:6 + n_in], refs[6 + n_in:6 + 2 * n_in]
    wgu_ref, wd_ref, fin_ref, o_ref = refs[6 + 2 * n_in:]
    for r in range(x_ref.shape[0] // ROW_TILE):
        rows = slice(r * ROW_TILE, (r + 1) * ROW_TILE)
        y = jnp.dot(a_refs[0][rows, :], w_refs[0][...], preferred_element_type=F32)
        for a_ref, w_ref in zip(a_refs[1:], w_refs[1:]):
            y += jnp.dot(a_ref[rows, :], w_ref[...], preferred_element_type=F32)
        x = x_ref[rows, :] + gate_m_ref[0] * y
        h = _norm_mod(x, g_ref[...], sh_ref[0], sc_ref[0]).astype(BF16)
        y = None
        for lo, hi in FF_CHUNKS:
            gt = jnp.dot(h, wgu_ref[:, lo:hi], preferred_element_type=F32)
            up = jnp.dot(h, wgu_ref[:, D_FF + lo:D_FF + hi], preferred_element_type=F32)
            part = jnp.dot((_silu(gt) * up).astype(BF16), wd_ref[lo:hi, :],
                           preferred_element_type=F32)
            y = part if y is None else y + part
        out = x + gate_f_ref[0] * y
        if final_norm:
            out = (out * lax.rsqrt(jnp.mean(out * out, axis=-1, keepdims=True) + NORM_EPS)
                   * fin_ref[...])
        o_ref[rows, :] = out


def _mix_ffn(x, gate_m, acts, w_outs, gain, shift, scale, gate_f, w_gate_up, w_down, layer,
             fin_gain, seq, final_norm, name):
    t, d = x.shape
    tm = MIX_SLABS * ROW_TILE
    per_seq = seq // tm
    row = lambda i: (i, 0)
    batch = lambda i: (i // per_seq, 0, 0)
    const = lambda i: (0, 0)
    this_layer = lambda i: (layer, 0, 0)
    return pl.pallas_call(
        functools.partial(_mix_ffn_kernel, n_in=len(acts), final_norm=final_norm),
        grid=(t // tm,),
        in_specs=[
            pl.BlockSpec((tm, d), row),
            pl.BlockSpec((1, 1, d), batch),
            pl.BlockSpec((1, d), const),
            pl.BlockSpec((1, 1, d), batch),
            pl.BlockSpec((1, 1, d), batch),
            pl.BlockSpec((1, 1, d), batch),
            *[pl.BlockSpec((tm, a.shape[1]), row) for a in acts],
            *[_resident(w.shape, const) for w in w_outs],
            _resident((None,) + w_gate_up.shape[1:], this_layer),
            _resident((None,) + w_down.shape[1:], this_layer),
            pl.BlockSpec((1, d), const),
        ],
        out_specs=pl.BlockSpec((tm, d), row),
        out_shape=jax.ShapeDtypeStruct((t, d), F32),
        compiler_params=_params(1),
        name=name,
    )(x, gate_m, gain, shift, scale, gate_f, *acts, *w_outs, w_gate_up, w_down, fin_gain)


def _rope_tables(seq, dim, tile):
    inv = ROPE_THETA ** (-jnp.arange(0, dim, 2, dtype=F32) / dim)
    hi = (jnp.arange(seq // ROPE_SIDE, dtype=jnp.int32) * ROPE_SIDE).astype(F32)[:, None] * inv
    lo = jnp.arange(ROPE_SIDE, dtype=jnp.int32).astype(F32)[:, None] * inv
    ch, sh, cl, sl = jnp.cos(hi)[:, None], jnp.sin(hi)[:, None], jnp.cos(lo)[None], jnp.sin(lo)[None]
    cos = (ch * cl - sh * sl).reshape(seq, dim // 2)
    sin = (sh * cl + ch * sl).reshape(seq, dim // 2)
    return jnp.tile(cos, (1, tile)), jnp.tile(sin, (1, tile))


def kernel(x, c, ada_w, ada_b, norm_gains, att_w_qkv, att_w_o, ret_w_in, ret_gn, ret_w_o,
           ffn_w_gate_up, ffn_w_down, final_norm):
    batch, seq, d = x.shape
    depth = ada_w.shape[0]
    assert d == D_MODEL and batch <= SUBLANES
    assert seq % (PROJ_SLABS * ROW_TILE) == 0 and seq % (MIX_SLABS * ROW_TILE) == 0
    assert seq % (2 * ATT_TILE) == 0 and seq % RET_SEQ_TILE == 0

    mod = _ada_mod(c, ada_w, ada_b)[:, :batch].reshape(depth, batch, 1, 6, d)
    xt = x.reshape(batch * seq, d)

    half = HEAD_DIM // 2
    cos_a, sin_a = _rope_tables(seq, HEAD_DIM, LANES // half)
    first_half = (jnp.arange(LANES) % HEAD_DIM) < half
    att_tables = (cos_a, jnp.where(first_half, -sin_a, 0.0), jnp.where(first_half, 0.0, sin_a))
    ret_tables = _rope_tables(seq, RET_DK, 1)
    assert depth == 2 and att_w_qkv.shape[0] == 1 and ret_w_in.shape[0] == 1
    rows = lambda w: w.reshape(-1, w.shape[-1])

    for layer in range(depth):
        shift_m, scale_m, gate_m, shift_f, scale_f, gate_f = (mod[layer, :, :, i] for i in range(6))
        gains = norm_gains[layer]
        if layer % 2 == 0:
            qkv = _proj(_att_proj_kernel, xt, gains[0:1], shift_m, scale_m, att_tables,
                        att_w_qkv[0].astype(BF16), seq, "att_proj")
            sb, (w_gate_up, w_ret_o, w_att_o) = _sb_attention(
                qkv, [rows(ffn_w_gate_up), rows(ret_w_o), rows(att_w_o)], batch, seq)
            mb, (w_down, w_ret_in) = _moba_attention(
                qkv, [rows(ffn_w_down), rows(ret_w_in)], batch, seq)
            w_gate_up = w_gate_up.reshape(ffn_w_gate_up.shape)
            w_down = w_down.reshape(ffn_w_down.shape)
            split = N_SB_HEADS * HEAD_DIM
            acts, w_outs = (sb, mb), (w_att_o[:split], w_att_o[split:])
        else:
            proj = _proj(_ret_proj_kernel, xt, gains[0:1], shift_m, scale_m, ret_tables,
                         w_ret_in, seq, "ret_proj")
            acts = (_retention(proj, ret_gn[0].reshape(1, RET_V_W), batch, seq),)
            w_outs = (w_ret_o,)
        xt = _mix_ffn(xt, gate_m, acts, w_outs, gains[1:2], shift_f, scale_f, gate_f, w_gate_up,
                      w_down, layer, final_norm.reshape(1, d), seq, layer == depth - 1,
                      "mix_ffn%d" % layer)
    return xt.reshape(batch, seq, d)
```

```python
import functools
import math

import jax
import jax.numpy as jnp
from jax import lax
from jax.experimental import pallas as pl
from jax.experimental.pallas import tpu as pltpu

F32 = jnp.float32
BF16 = jnp.bfloat16

D_MODEL = 1024
HEAD_DIM = 64
N_SB_HEADS = 8
N_MOBA_HEADS = 8
ATT_W = (N_SB_HEADS + N_MOBA_HEADS) * HEAD_DIM
MOBA_BLOCK = 256
MOBA_TOPK = 3
N_RET_HEADS = 4
RET_DK = D_MODEL // N_RET_HEADS
RET_DV = 2 * RET_DK
RET_QK_W = N_RET_HEADS * RET_DK
RET_V_W = N_RET_HEADS * RET_DV
D_FF = -(-8 * D_MODEL // 768) * 256
ROPE_THETA = 10000.0
ROPE_SIDE = 64
NORM_EPS = 1e-6
GN_EPS = 1e-5
NEG = -1e30
Q_SCALE = HEAD_DIM ** -0.5 * math.log2(math.e)

LANES = 128
SUBLANES = 8
ADA_COL_TILE = 1536
ROW_TILE = 512
PROJ_SLABS = 2
MIX_SLABS = 2
COL_CHUNK = 512
MXU_WIDTH = 256
_FF_SPLIT = (D_FF // MXU_WIDTH + 1) // 2 * MXU_WIDTH
FF_CHUNKS = ((0, _FF_SPLIT), (_FF_SPLIT, D_FF))
ATT_TILE = 256
PAIRS_PER_STEP = 2
RET_CHUNK = 256
RET_SEQ_TILE = 1024
RET_HEADS_PER_STEP = 4
VMEM_LIMIT = 56 * 1024 * 1024


def _params(n_axes):
    return pltpu.CompilerParams(
        dimension_semantics=("arbitrary",) * n_axes, vmem_limit_bytes=VMEM_LIMIT)


def _resident(shape, index_map):
    return pl.BlockSpec(shape, index_map, pipeline_mode=pl.Buffered(1))


def _silu(v):
    h = 0.5 * v
    return h * jnp.tanh(h) + h


def _ada_kernel(c_ref, w_ref, b_ref, o_ref):
    o_ref[0] = jnp.dot(_silu(c_ref[...]), w_ref[0], preferred_element_type=F32) + b_ref[0]


def _ada_mod(c, ada_w, ada_b):
    depth, d, n = ada_w.shape
    cp = jnp.pad(c, ((0, SUBLANES - c.shape[0]), (0, 0)))
    tn = ADA_COL_TILE
    return pl.pallas_call(
        _ada_kernel,
        grid=(depth, n // tn),
        in_specs=[
            pl.BlockSpec((SUBLANES, d), lambda l, j: (0, 0)),
            pl.BlockSpec((1, d, tn), lambda l, j: (l, 0, j)),
            pl.BlockSpec((1, 1, tn), lambda l, j: (l, 0, j)),
        ],
        out_specs=pl.BlockSpec((1, SUBLANES, tn), lambda l, j: (l, 0, j)),
        out_shape=jax.ShapeDtypeStruct((depth, SUBLANES, n), F32),
        compiler_params=_params(2),
        name="ada_mod",
    )(cp, ada_w, ada_b.reshape(depth, 1, n))


def _norm_mod(x, g, shift, scale):
    y = x * lax.rsqrt(jnp.mean(x * x, axis=-1, keepdims=True) + NORM_EPS)
    return (y * g) * (1.0 + scale) + shift


def _att_proj_kernel(x_ref, g_ref, sh_ref, sc_ref, cos_ref, sina_ref, sinb_ref, w_ref, o_ref):
    sb_w = N_SB_HEADS * HEAD_DIM
    for r in range(x_ref.shape[0] // ROW_TILE):
        rows = slice(r * ROW_TILE, (r + 1) * ROW_TILE)
        h = _norm_mod(x_ref[rows, :], g_ref[...], sh_ref[0], sc_ref[0]).astype(BF16)
        cos, sina, sinb = cos_ref[rows, :], sina_ref[rows, :], sinb_ref[rows, :]
        for c in range(3 * ATT_W // COL_CHUNK):
            lo = c * COL_CHUNK
            y = jnp.dot(h, w_ref[:, lo:lo + COL_CHUNK], preferred_element_type=F32)
            is_q = lo < ATT_W
            rotated = (lo % ATT_W) >= sb_w and lo < 2 * ATT_W
            for s in range(COL_CHUNK // LANES):
                ys = y[:, s * LANES:(s + 1) * LANES]
                if rotated:
                    ys = (ys * cos + pltpu.roll(ys, LANES - HEAD_DIM // 2, 1) * sina
                          + pltpu.roll(ys, HEAD_DIM // 2, 1) * sinb)
                if is_q:
                    ys = ys * Q_SCALE
                o_ref[rows, lo + s * LANES:lo + (s + 1) * LANES] = ys.astype(BF16)


def _ret_proj_kernel(x_ref, g_ref, sh_ref, sc_ref, cos_ref, sin_ref, w_ref, o_ref):
    n = 2 * RET_QK_W + 2 * RET_V_W
    half = RET_DK // 2
    for r in range(x_ref.shape[0] // ROW_TILE):
        rows = slice(r * ROW_TILE, (r + 1) * ROW_TILE)
        h = _norm_mod(x_ref[rows, :], g_ref[...], sh_ref[0], sc_ref[0]).astype(BF16)
        cos, sin = cos_ref[rows, :], sin_ref[rows, :]
        for c in range(n // COL_CHUNK):
            lo = c * COL_CHUNK
            y = jnp.dot(h, w_ref[:, lo:lo + COL_CHUNK], preferred_element_type=F32)
            if lo < 2 * RET_QK_W:
                mul = 1.0 if lo < RET_QK_W else RET_DK ** -0.5
                for hd in range(COL_CHUNK // RET_DK):
                    x1 = y[:, hd * RET_DK:hd * RET_DK + half]
                    x2 = y[:, hd * RET_DK + half:(hd + 1) * RET_DK]
                    o1 = (x1 * cos - x2 * sin) * mul
                    o2 = (x2 * cos + x1 * sin) * mul
                    o_ref[rows, lo + hd * RET_DK:lo + hd * RET_DK + half] = o1.astype(BF16)
                    o_ref[rows, lo + hd * RET_DK + half:lo + (hd + 1) * RET_DK] = o2.astype(BF16)
            else:
                o_ref[rows, lo:lo + COL_CHUNK] = y.astype(BF16)


def _proj(kernel, x, gain, shift, scale, tables, w, seq, name):
    t, d = x.shape
    n = w.shape[1]
    tm = PROJ_SLABS * ROW_TILE
    per_seq = seq // tm
    row = lambda i: (i, 0)
    batch = lambda i: (i // per_seq, 0, 0)
    pos = lambda i: (i % per_seq, 0)
    return pl.pallas_call(
        kernel,
        grid=(t // tm,),
        in_specs=[
            pl.BlockSpec((tm, d), row),
            pl.BlockSpec((1, d), lambda i: (0, 0)),
            pl.BlockSpec((1, 1, d), batch),
            pl.BlockSpec((1, 1, d), batch),
            *[pl.BlockSpec((tm, LANES), pos) for _ in tables],
            _resident((d, n), lambda i: (0, 0)),
        ],
        out_specs=pl.BlockSpec((tm, n), row),
        out_shape=jax.ShapeDtypeStruct((t, n), BF16),
        compiler_params=_params(1),
        name=name,
    )(x, gain, shift, scale, *tables, w)


def _pair_lanes(p):
    return slice(p * LANES, (p + 1) * LANES)


def _fill_vt(v_ref, vt_ref):
    t = ATT_TILE
    for p in range(v_ref.shape[1] // LANES):
        for n in range(v_ref.shape[0] // t):
            vt_ref[p, n] = v_ref[n * t:(n + 1) * t, _pair_lanes(p)].T


def _tile_masks(strict):
    t = ATT_TILE
    key = lax.broadcasted_iota(jnp.int32, (t, 2 * t), 0)
    qry = lax.broadcasted_iota(jnp.int32, (t, 2 * t), 1) % t
    chan0 = lax.broadcasted_iota(jnp.int32, (LANES, 1), 0) < HEAD_DIM
    return (key < qry) if strict else (key <= qry), chan0


def _load_super_tile(q_ref, p, i, chan0):
    t = ATT_TILE
    base = pl.multiple_of(i * 2 * t, 2 * t)
    cells = []
    for c in range(2):
        q_t = q_ref[pl.ds(base + c * t, t), _pair_lanes(p)].T
        zero = jnp.zeros_like(q_t)
        cells += [jnp.where(chan0, q_t, zero), jnp.where(chan0, zero, q_t)]
    return jnp.concatenate(cells, axis=1), jnp.concatenate(cells[2:], axis=1)


def _store_super_tile(o_ref, p, i, acc_t, chan0):
    t = ATT_TILE
    base = pl.multiple_of(i * 2 * t, 2 * t)
    for cell in range(2):
        lo = cell * 2 * t
        o_t = jnp.where(chan0, acc_t[:, lo:lo + t], acc_t[:, lo + t:lo + 2 * t])
        o_ref[pl.ds(base + cell * t, t), _pair_lanes(p)] = o_t.T.astype(BF16)


def _scores(k_ref, p, j, q_cols, blocks=1):
    t = ATT_TILE
    return jnp.dot(k_ref[pl.ds(pl.multiple_of(j * t, t), blocks * t), _pair_lanes(p)], q_cols,
                   preferred_element_type=F32)


def _stage_pair_of_blocks(stage_ref, k_ref, half, j, tiles):
    t = ATT_TILE
    for p, (q_cols, _) in enumerate(tiles):
        z = _scores(k_ref, p, j, q_cols, blocks=2)
        stage_ref[half + 2 * p] = z[:t]
        stage_ref[half + 2 * p + 1] = z[t:]


def _sb_kernel(q_ref, k_ref, v_ref, o_ref, vt_ref, acc_ref, z_ref):
    t = ATT_TILE
    w = 2 * t
    seq = q_ref.shape[0]
    pairs = range(q_ref.shape[1] // LANES)
    per_half = 2 * len(pairs)
    strict, chan0 = _tile_masks(True)
    from_key = -(lax.broadcasted_iota(jnp.int32, (t, t), 1)
                 >= lax.broadcasted_iota(jnp.int32, (t, t), 0)).astype(BF16)
    _fill_vt(v_ref, vt_ref)

    def block(p, j, slot, lanes, run, mask):
        z = z_ref[slot, :, lanes]
        neg_l = jnp.maximum(z, 0.0) + jnp.log2(1.0 + jnp.exp2(-jnp.abs(z)))
        if mask is not None:
            neg_l = jnp.where(mask, neg_l, 0.0)
        since = jnp.dot(from_key, neg_l.astype(BF16), preferred_element_type=F32)
        a = jnp.exp2(z_ref[slot, :, lanes] + since)
        if mask is not None:
            a = jnp.where(mask, a, 0.0)
        part = jnp.dot(vt_ref[p, j], a.astype(BF16), preferred_element_type=F32)
        return run + since[0:1, :], part * jnp.exp2(run)

    def q_super(i, _):
        tiles = [_load_super_tile(q_ref, p, i, chan0) for p in pairs]
        mask = jnp.concatenate([strict, jnp.ones((t, w), jnp.bool_)], axis=1)
        cell1 = slice(w, 2 * w)
        diag = lax.rem(i, 2) * per_half
        for p in pairs:
            q2, qb = tiles[p]
            z_ref[diag + 2 * p + 1, :, cell1] = _scores(k_ref, p, 2 * i + 1, qb)
            z_ref[diag + 2 * p] = _scores(k_ref, p, 2 * i, q2)
        runs = []
        for p in pairs:
            run_b, part = block(p, 2 * i + 1, diag + 2 * p + 1, cell1, jnp.zeros((1, w), F32),
                                strict)
            acc_ref[p, :, w:] = part
            run = jnp.concatenate([jnp.zeros((1, w), F32), run_b], axis=1)
            run, part = block(p, 2 * i, diag + 2 * p, slice(None), run, mask)
            acc_ref[p, :, :w] = part[:, :w]
            acc_ref[p, :, w:] += part[:, w:]
            runs.append(run)

        def past(s, runs):
            j = 2 * (i - 1 - s)
            half = lax.rem(s, 2) * per_half
            _stage_pair_of_blocks(z_ref, k_ref, half, j, tiles)
            out = []
            for p in pairs:
                run, part_b = block(p, j + 1, half + 2 * p + 1, slice(None), runs[p], None)
                run, part_a = block(p, j, half + 2 * p, slice(None), run, None)
                acc_ref[p] += part_b + part_a
                out.append(run)
            return tuple(out)

        lax.fori_loop(0, i, past, tuple(runs))
        for p in pairs:
            _store_super_tile(o_ref, p, i, acc_ref[p], chan0)
        return 0

    lax.fori_loop(0, seq // w, q_super, 0)


def _pair_specs(seq, q_col, k_col, v_col):
    width = PAIRS_PER_STEP * LANES
    return [
        pl.BlockSpec((seq, width), lambda b, p: (b, q_col + p)),
        pl.BlockSpec((seq, width), lambda b, p: (b, k_col + p)),
        pl.BlockSpec((seq, width), lambda b, p: (b, v_col + p)),
    ]


def _with_weight_casts(kernel, n_cast):
    def wrapped(q_ref, k_ref, v_ref, *refs):
        srcs, o_ref, dsts = refs[:n_cast], refs[n_cast], refs[n_cast + 1:2 * n_cast + 1]
        for src, dst in zip(srcs, dsts):
            dst[...] = src[...].astype(BF16)
        kernel(q_ref, k_ref, v_ref, o_ref, *refs[2 * n_cast + 1:])
    return wrapped


def _attention_call(kernel, qkv, weights, batch, seq, n_heads, first_head, extra_scratch, name):
    width = PAIRS_PER_STEP * LANES
    groups = n_heads * HEAD_DIM // width
    third = ATT_W // width
    first = first_head * HEAD_DIM // width
    steps = batch * groups
    w_specs = [pl.BlockSpec((w.shape[0] // steps, w.shape[1]), lambda b, p: (b * groups + p, 0))
               for w in weights]
    out, *cast = pl.pallas_call(
        _with_weight_casts(kernel, len(weights)),
        grid=(batch, groups),
        in_specs=_pair_specs(seq, first, third + first, 2 * third + first) + w_specs,
        out_specs=[pl.BlockSpec((seq, width), lambda b, p: (b, p))] + w_specs,
        out_shape=[jax.ShapeDtypeStruct((batch * seq, groups * width), BF16)]
        + [jax.ShapeDtypeStruct(w.shape, BF16) for w in weights],
        scratch_shapes=[pltpu.VMEM((PAIRS_PER_STEP, seq // ATT_TILE, LANES, ATT_TILE), BF16),
                        pltpu.VMEM((PAIRS_PER_STEP, LANES, 4 * ATT_TILE), F32),
                        *extra_scratch],
        compiler_params=_params(2),
        name=name,
    )(qkv, qkv, qkv, *weights)
    return out, cast


def _sb_attention(qkv, weights, batch, seq):
    staged = pltpu.VMEM((4 * PAIRS_PER_STEP, ATT_TILE, 4 * ATT_TILE), F32)
    return _attention_call(_sb_kernel, qkv, weights, batch, seq, N_SB_HEADS, 0, [staged],
                           "sb_attention")


def _moba_kernel(q_ref, k_ref, v_ref, o_ref, vt_ref, acc_ref, sel_ref, s_ref):
    t = ATT_TILE
    w = 2 * t
    seq = q_ref.shape[0]
    nb = seq // MOBA_BLOCK
    pairs = range(q_ref.shape[1] // LANES)
    per_half = 2 * len(pairs)
    causal, chan0 = _tile_masks(False)
    blk = lax.broadcasted_iota(jnp.int32, (nb, 2 * w), 0)
    cell = lax.broadcasted_iota(jnp.int32, (nb, 2 * w), 1) // w
    blk_f = blk.astype(F32)
    kmean = [(jnp.sum(k_ref[:, _pair_lanes(p)].astype(F32).reshape(nb, MOBA_BLOCK, LANES), axis=1)
              * (1.0 / MOBA_BLOCK)).astype(BF16) for p in pairs]
    _fill_vt(v_ref, vt_ref)


    def select(p, i, q2):
        gate = jnp.dot(kmean[p], q2, preferred_element_type=F32)
        past_blk = blk < 2 * i + cell
        gate = jnp.where(past_blk, gate, NEG)
        sel = jnp.zeros((nb, 2 * w), F32)
        for _r in range(MOBA_TOPK):
            top = jnp.max(gate, axis=0, keepdims=True)
            idx = jnp.min(jnp.where(gate == top, blk_f, float(nb)), axis=0, keepdims=True)
            hit = blk_f == idx
            sel = jnp.where(hit & past_blk, 1.0, sel)
            gate = jnp.where(hit, -jnp.inf, gate)
        sel_ref[p] = sel

    def own_blocks(p, i, raw_a, raw_b):
        s = jnp.where(causal, raw_b, NEG)
        m_b = jnp.max(s, axis=0, keepdims=True)
        pr = jnp.exp2(s - m_b)
        acc_ref[p, :, :w] = jnp.zeros((LANES, w), F32)
        acc_ref[p, :, w:] = jnp.dot(vt_ref[p, 2 * i + 1], pr.astype(BF16),
                                    preferred_element_type=F32)
        m = jnp.concatenate([jnp.full((1, w), -jnp.inf, F32), m_b], axis=1)
        l = jnp.concatenate([jnp.zeros((1, w), F32), jnp.sum(pr, axis=0, keepdims=True)], axis=1)
        keep = jnp.concatenate(
            [causal, jnp.broadcast_to(sel_ref[p, pl.ds(2 * i, 1), w:] > 0.0, (t, w))], axis=1)
        s = jnp.where(keep, raw_a, NEG)
        m_new = jnp.maximum(m, jnp.max(s, axis=0, keepdims=True))
        alpha = jnp.exp2(m - m_new)
        pr = jnp.exp2(s - m_new)
        acc_ref[p] = alpha * acc_ref[p] + jnp.dot(
            vt_ref[p, 2 * i], pr.astype(BF16), preferred_element_type=F32)
        return m_new, alpha * l + jnp.sum(pr, axis=0, keepdims=True)

    def past_pair(p, j, raw_a, raw_b, m_old, l_old):
        s_a = jnp.where(sel_ref[p, pl.ds(j, 1), :] > 0.0, raw_a, NEG)
        s_b = jnp.where(sel_ref[p, pl.ds(j + 1, 1), :] > 0.0, raw_b, NEG)
        m_new = jnp.maximum(m_old, jnp.maximum(jnp.max(s_a, axis=0, keepdims=True),
                                               jnp.max(s_b, axis=0, keepdims=True)))
        alpha = jnp.exp2(m_old - m_new)
        p_a = jnp.exp2(s_a - m_new)
        p_b = jnp.exp2(s_b - m_new)
        part = (jnp.dot(vt_ref[p, j], p_a.astype(BF16), preferred_element_type=F32)
                + jnp.dot(vt_ref[p, j + 1], p_b.astype(BF16), preferred_element_type=F32))
        acc_ref[p] = alpha * acc_ref[p] + part
        l_new = (alpha * l_old + jnp.sum(p_a, axis=0, keepdims=True)
                 + jnp.sum(p_b, axis=0, keepdims=True))
        return m_new, l_new

    def q_super(i, _):
        tiles = [_load_super_tile(q_ref, p, i, chan0) for p in pairs]
        own = lax.rem(i, 2) * per_half
        for p in pairs:
            q2, qb = tiles[p]
            s_ref[own + 2 * p + 1, :, w:] = _scores(k_ref, p, 2 * i + 1, qb)
            s_ref[own + 2 * p] = _scores(k_ref, p, 2 * i, q2)
            select(p, i, q2)
        stats = tuple(own_blocks(p, i, s_ref[own + 2 * p], s_ref[own + 2 * p + 1, :, w:])
                      for p in pairs)

        def past(jj, stats):
            half = lax.rem(jj, 2) * per_half
            _stage_pair_of_blocks(s_ref, k_ref, half, 2 * jj, tiles)
            return tuple(past_pair(p, 2 * jj, s_ref[half + 2 * p], s_ref[half + 2 * p + 1],
                                   *stats[p]) for p in pairs)

        stats = lax.fori_loop(0, i, past, stats)
        for p in pairs:
            _store_super_tile(o_ref, p, i, acc_ref[p] / stats[p][1], chan0)
        return 0

    lax.fori_loop(0, seq // w, q_super, 0)


def _moba_attention(qkv, weights, batch, seq):
    sel = pltpu.VMEM((PAIRS_PER_STEP, seq // MOBA_BLOCK, 4 * ATT_TILE), F32)
    staged = pltpu.VMEM((4 * PAIRS_PER_STEP, ATT_TILE, 4 * ATT_TILE), F32)
    return _attention_call(_moba_kernel, qkv, weights, batch, seq, N_MOBA_HEADS, N_SB_HEADS,
                           [sel, staged], "moba_attention")


def _ret_kernel(q_ref, k_ref, v_ref, g_ref, gn_ref, o_ref, state_ref, decay_ref):
    c = RET_CHUNK
    heads = range(RET_HEADS_PER_STEP)
    n_row = lax.broadcasted_iota(jnp.int32, (c, 1), 0).astype(F32)
    log_g, xi, zeta, chunk_decay = [], [], [], []
    for hh in heads:
        head = (pl.program_id(1) * RET_HEADS_PER_STEP + hh + 5).astype(F32)
        lg = jnp.log(1.0 - jnp.exp2(-jnp.full((1, 1), head, F32)))
        log_g.append(lg)
        xi.append(jnp.exp(lg * (n_row + 1.0)))
        zeta.append(jnp.exp(lg * (c - 1.0 - n_row)))
        chunk_decay.append(jnp.exp(lg * c))

    @pl.when(pl.program_id(2) == 0)
    def _():
        diff = (lax.broadcasted_iota(jnp.int32, (c, c), 0)
                - lax.broadcasted_iota(jnp.int32, (c, c), 1)).astype(F32)
        for hh in heads:
            decay_ref[hh] = jnp.where(diff >= 0, jnp.exp(log_g[hh] * jnp.maximum(diff, 0.0)), 0.0)
        state_ref[...] = jnp.zeros_like(state_ref)

    def chunk(i, _):
        rows = pl.ds(pl.multiple_of(i * c, c), c)
        for hh in heads:
            qk_cols = slice(hh * RET_DK, (hh + 1) * RET_DK)
            v_cols = slice(hh * RET_DV, (hh + 1) * RET_DV)
            q, k, v = q_ref[rows, qk_cols], k_ref[rows, qk_cols], v_ref[rows, v_cols]
            inner = lax.dot_general(q, k, (((1,), (1,)), ((), ())),
                                    preferred_element_type=F32) * decay_ref[hh]
            state = state_ref[hh]
            q_xi = (q.astype(F32) * xi[hh]).astype(BF16)
            o = (jnp.dot(inner.astype(BF16), v, preferred_element_type=F32)
                 + jnp.dot(q_xi, state.astype(BF16), preferred_element_type=F32))
            kz = (k.astype(F32) * zeta[hh]).astype(BF16)
            state_ref[hh] = state * chunk_decay[hh] + lax.dot_general(
                kz, v, (((0,), (0,)), ((), ())), preferred_element_type=F32)
            mu = jnp.mean(o, axis=-1, keepdims=True)
            var = jnp.mean(jnp.square(o - mu), axis=-1, keepdims=True)
            on = (o - mu) * lax.rsqrt(var + GN_EPS) * gn_ref[:, v_cols]
            o_ref[rows, v_cols] = (_silu(g_ref[rows, v_cols].astype(F32)) * on).astype(BF16)
        return 0

    lax.fori_loop(0, q_ref.shape[0] // c, chunk, 0)


def _retention(proj, gn_gain, batch, seq):
    hps = RET_HEADS_PER_STEP
    tile = RET_SEQ_TILE
    tiles = seq // tile
    qk_w, v_w = hps * RET_DK, hps * RET_DV
    k0 = RET_QK_W // qk_w
    v0 = 2 * RET_QK_W // v_w
    g0 = v0 + RET_V_W // v_w
    row = lambda b, h, s: b * tiles + s
    return pl.pallas_call(
        _ret_kernel,
        grid=(batch, N_RET_HEADS // hps, tiles),
        in_specs=[
            pl.BlockSpec((tile, qk_w), lambda b, h, s: (row(b, h, s), h)),
            pl.BlockSpec((tile, qk_w), lambda b, h, s: (row(b, h, s), k0 + h)),
            pl.BlockSpec((tile, v_w), lambda b, h, s: (row(b, h, s), v0 + h)),
            pl.BlockSpec((tile, v_w), lambda b, h, s: (row(b, h, s), g0 + h)),
            pl.BlockSpec((1, v_w), lambda b, h, s: (0, h)),
        ],
        out_specs=pl.BlockSpec((tile, v_w), lambda b, h, s: (row(b, h, s), h)),
        out_shape=jax.ShapeDtypeStruct((batch * seq, RET_V_W), BF16),
        scratch_shapes=[pltpu.VMEM((hps, RET_DK, RET_DV), F32),
                        pltpu.VMEM((hps, RET_CHUNK, RET_CHUNK), F32)],
        compiler_params=_params(3),
        name="retention",
    )(proj, proj, proj, proj, gn_gain)


def _mix_ffn_kernel(*refs, n_in, final_norm):
    x_ref, gate_m_ref, g_ref, sh_ref, sc_ref, gate_f_ref = refs[:6]
    a_refs, w_refs = refs[6:6 + n_in], refs[6 + n_in:6 + 2 * n_in]
    wgu_ref, wd_ref, fin_ref, o_ref = refs[6 + 2 * n_in:]
    for r in range(x_ref.shape[0] // ROW_TILE):
        rows = slice(r * ROW_TILE, (r + 1) * ROW_TILE)
        y = jnp.dot(a_refs[0][rows, :], w_refs[0][...], preferred_element_type=F32)
        for a_ref, w_ref in zip(a_refs[1:], w_refs[1:]):
            y += jnp.dot(a_ref[rows, :], w_ref[...], preferred_element_type=F32)
        x = x_ref[rows, :] + gate_m_ref[0] * y
        h = _norm_mod(x, g_ref[...], sh_ref[0], sc_ref[0]).astype(BF16)
        y = None
        for lo, hi in FF_CHUNKS:
            gt = jnp.dot(h, wgu_ref[:, lo:hi], preferred_element_type=F32)
            up = jnp.dot(h, wgu_ref[:, D_FF + lo:D_FF + hi], preferred_element_type=F32)
            part = jnp.dot((_silu(gt) * up).astype(BF16), wd_ref[lo:hi, :],
                           preferred_element_type=F32)
            y = part if y is None else y + part
        out = x + gate_f_ref[0] * y
        if final_norm:
            out = (out * lax.rsqrt(jnp.mean(out * out, axis=-1, keepdims=True) + NORM_EPS)
                   * fin_ref[...])
        o_ref[rows, :] = out


def _mix_ffn(x, gate_m, acts, w_outs, gain, shift, scale, gate_f, w_gate_up, w_down, layer,
             fin_gain, seq, final_norm, name):
    t, d = x.shape
    tm = MIX_SLABS * ROW_TILE
    per_seq = seq // tm
    row = lambda i: (i, 0)
    batch = lambda i: (i // per_seq, 0, 0)
    const = lambda i: (0, 0)
    this_layer = lambda i: (layer, 0, 0)
    return pl.pallas_call(
        functools.partial(_mix_ffn_kernel, n_in=len(acts), final_norm=final_norm),
        grid=(t // tm,),
        in_specs=[
            pl.BlockSpec((tm, d), row),
            pl.BlockSpec((1, 1, d), batch),
            pl.BlockSpec((1, d), const),
            pl.BlockSpec((1, 1, d), batch),
            pl.BlockSpec((1, 1, d), batch),
            pl.BlockSpec((1, 1, d), batch),
            *[pl.BlockSpec((tm, a.shape[1]), row) for a in acts],
            *[_resident(w.shape, const) for w in w_outs],
            _resident((None,) + w_gate_up.shape[1:], this_layer),
            _resident((None,) + w_down.shape[1:], this_layer),
            pl.BlockSpec((1, d), const),
        ],
        out_specs=pl.BlockSpec((tm, d), row),
        out_shape=jax.ShapeDtypeStruct((t, d), F32),
        compiler_params=_params(1),
        name=name,
    )(x, gate_m, gain, shift, scale, gate_f, *acts, *w_outs, w_gate_up, w_down, fin_gain)


def _rope_tables(seq, dim, tile):
    inv = ROPE_THETA ** (-jnp.arange(0, dim, 2, dtype=F32) / dim)
    hi = (jnp.arange(seq // ROPE_SIDE, dtype=jnp.int32) * ROPE_SIDE).astype(F32)[:, None] * inv
    lo = jnp.arange(ROPE_SIDE, dtype=jnp.int32).astype(F32)[:, None] * inv
    ch, sh, cl, sl = jnp.cos(hi)[:, None], jnp.sin(hi)[:, None], jnp.cos(lo)[None], jnp.sin(lo)[None]
    cos = (ch * cl - sh * sl).reshape(seq, dim // 2)
    sin = (sh * cl + ch * sl).reshape(seq, dim // 2)
    return jnp.tile(cos, (1, tile)), jnp.tile(sin, (1, tile))


def kernel(x, c, ada_w, ada_b, norm_gains, att_w_qkv, att_w_o, ret_w_in, ret_gn, ret_w_o,
           ffn_w_gate_up, ffn_w_down, final_norm):
    batch, seq, d = x.shape
    depth = ada_w.shape[0]
    assert d == D_MODEL and batch <= SUBLANES
    assert seq % (PROJ_SLABS * ROW_TILE) == 0 and seq % (MIX_SLABS * ROW_TILE) == 0
    assert seq % (2 * ATT_TILE) == 0 and seq % RET_SEQ_TILE == 0

    mod = _ada_mod(c, ada_w, ada_b)[:, :batch].reshape(depth, batch, 1, 6, d)
    xt = x.reshape(batch * seq, d)

    half = HEAD_DIM // 2
    cos_a, sin_a = _rope_tables(seq, HEAD_DIM, LANES // half)
    first_half = (jnp.arange(LANES) % HEAD_DIM) < half
    att_tables = (cos_a, jnp.where(first_half, -sin_a, 0.0), jnp.where(first_half, 0.0, sin_a))
    ret_tables = _rope_tables(seq, RET_DK, 1)
    assert depth == 2 and att_w_qkv.shape[0] == 1 and ret_w_in.shape[0] == 1
    rows = lambda w: w.reshape(-1, w.shape[-1])

    for layer in range(depth):
        shift_m, scale_m, gate_m, shift_f, scale_f, gate_f = (mod[layer, :, :, i] for i in range(6))
        gains = norm_gains[layer]
        if layer % 2 == 0:
            qkv = _proj(_att_proj_kernel, xt, gains[0:1], shift_m, scale_m, att_tables,
                        att_w_qkv[0].astype(BF16), seq, "att_proj")
            sb, (w_gate_up, w_ret_o, w_att_o) = _sb_attention(
                qkv, [rows(ffn_w_gate_up), rows(ret_w_o), rows(att_w_o)], batch, seq)
            mb, (w_down, w_ret_in) = _moba_attention(
                qkv, [rows(ffn_w_down), rows(ret_w_in)], batch, seq)
            w_gate_up = w_gate_up.reshape(ffn_w_gate_up.shape)
            w_down = w_down.reshape(ffn_w_down.shape)
            split = N_SB_HEADS * HEAD_DIM
            acts, w_outs = (sb, mb), (w_att_o[:split], w_att_o[split:])
        else:
            proj = _proj(_ret_proj_kernel, xt, gains[0:1], shift_m, scale_m, ret_tables,
                         w_ret_in, seq, "ret_proj")
            acts = (_retention(proj, ret_gn[0].reshape(1, RET_V_W), batch, seq),)
            w_outs = (w_ret_o,)
        xt = _mix_ffn(xt, gate_m, acts, w_outs, gains[1:2], shift_f, scale_f, gate_f, w_gate_up,
                      w_down, layer, final_norm.reshape(1, d), seq, layer == depth - 1,
                      "mix_ffn%d" % layer)
    return xt.reshape(batch, seq, d)
```

```python
import functools
import math

import jax
import jax.numpy as jnp
from jax import lax
from jax.experimental import pallas as pl
from jax.experimental.pallas import tpu as pltpu

F32 = jnp.float32
BF16 = jnp.bfloat16

D_MODEL = 1024
HEAD_DIM = 64
N_SB_HEADS = 8
N_MOBA_HEADS = 8
ATT_W = (N_SB_HEADS + N_MOBA_HEADS) * HEAD_DIM
MOBA_BLOCK = 256
MOBA_TOPK = 3
N_RET_HEADS = 4
RET_DK = D_MODEL // N_RET_HEADS
RET_DV = 2 * RET_DK
RET_QK_W = N_RET_HEADS * RET_DK
RET_V_W = N_RET_HEADS * RET_DV
D_FF = -(-8 * D_MODEL // 768) * 256
ROPE_THETA = 10000.0
ROPE_SIDE = 64
NORM_EPS = 1e-6
GN_EPS = 1e-5
NEG = -1e30
Q_SCALE = HEAD_DIM ** -0.5 * math.log2(math.e)

LANES = 128
SUBLANES = 8
ADA_COL_TILE = 1536
ROW_TILE = 512
PROJ_SLABS = 2
MIX_SLABS = 2
COL_CHUNK = 512
MXU_WIDTH = 256
_FF_SPLIT = (D_FF // MXU_WIDTH + 1) // 2 * MXU_WIDTH
FF_CHUNKS = ((0, _FF_SPLIT), (_FF_SPLIT, D_FF))
ATT_TILE = 256
PAIRS_PER_STEP = 2
RET_CHUNK = 256
RET_SEQ_TILE = 1024
RET_HEADS_PER_STEP = 4
VMEM_LIMIT = 56 * 1024 * 1024


def _params(n_axes):
    return pltpu.CompilerParams(
        dimension_semantics=("arbitrary",) * n_axes, vmem_limit_bytes=VMEM_LIMIT)


def _resident(shape, index_map):
    return pl.BlockSpec(shape, index_map, pipeline_mode=pl.Buffered(1))


def _silu(v):
    h = 0.5 * v
    return h * jnp.tanh(h) + h


def _ada_kernel(c_ref, w_ref, b_ref, o_ref):
    o_ref[0] = jnp.dot(_silu(c_ref[...]), w_ref[0], preferred_element_type=F32) + b_ref[0]


def _ada_mod(c, ada_w, ada_b):
    depth, d, n = ada_w.shape
    cp = jnp.pad(c, ((0, SUBLANES - c.shape[0]), (0, 0)))
    tn = ADA_COL_TILE
    return pl.pallas_call(
        _ada_kernel,
        grid=(depth, n // tn),
        in_specs=[
            pl.BlockSpec((SUBLANES, d), lambda l, j: (0, 0)),
            pl.BlockSpec((1, d, tn), lambda l, j: (l, 0, j)),
            pl.BlockSpec((1, 1, tn), lambda l, j: (l, 0, j)),
        ],
        out_specs=pl.BlockSpec((1, SUBLANES, tn), lambda l, j: (l, 0, j)),
        out_shape=jax.ShapeDtypeStruct((depth, SUBLANES, n), F32),
        compiler_params=_params(2),
        name="ada_mod",
    )(cp, ada_w, ada_b.reshape(depth, 1, n))


def _norm_mod(x, g, shift, scale):
    y = x * lax.rsqrt(jnp.mean(x * x, axis=-1, keepdims=True) + NORM_EPS)
    return (y * g) * (1.0 + scale) + shift


def _att_proj_kernel(x_ref, g_ref, sh_ref, sc_ref, cos_ref, sina_ref, sinb_ref, w_ref, o_ref,
                     vt_ref):
    sb_w = N_SB_HEADS * HEAD_DIM
    blocks_per_slab = ROW_TILE // ATT_TILE
    for r in range(x_ref.shape[0] // ROW_TILE):
        rows = slice(r * ROW_TILE, (r + 1) * ROW_TILE)
        h = _norm_mod(x_ref[rows, :], g_ref[...], sh_ref[0], sc_ref[0]).astype(BF16)
        cos, sina, sinb = cos_ref[rows, :], sina_ref[rows, :], sinb_ref[rows, :]
        for c in range(3 * ATT_W // COL_CHUNK):
            lo = c * COL_CHUNK
            y = jnp.dot(h, w_ref[:, lo:lo + COL_CHUNK], preferred_element_type=F32)
            is_q = lo < ATT_W
            rotated = (lo % ATT_W) >= sb_w and lo < 2 * ATT_W
            for s in range(COL_CHUNK // LANES):
                ys = y[:, s * LANES:(s + 1) * LANES]
                if rotated:
                    ys = (ys * cos + pltpu.roll(ys, LANES - HEAD_DIM // 2, 1) * sina
                          + pltpu.roll(ys, HEAD_DIM // 2, 1) * sinb)
                if is_q:
                    ys = ys * Q_SCALE
                o_ref[rows, lo + s * LANES:lo + (s + 1) * LANES] = ys.astype(BF16)
            if lo >= 2 * ATT_W:
                chans = slice(lo - 2 * ATT_W, lo - 2 * ATT_W + COL_CHUNK)
                for b in range(blocks_per_slab):
                    vt_ref[r * blocks_per_slab + b, chans, :] = (
                        y[b * ATT_TILE:(b + 1) * ATT_TILE, :].T.astype(BF16))


def _ret_proj_kernel(x_ref, g_ref, sh_ref, sc_ref, cos_ref, sin_ref, w_ref, o_ref):
    n = 2 * RET_QK_W + 2 * RET_V_W
    half = RET_DK // 2
    for r in range(x_ref.shape[0] // ROW_TILE):
        rows = slice(r * ROW_TILE, (r + 1) * ROW_TILE)
        h = _norm_mod(x_ref[rows, :], g_ref[...], sh_ref[0], sc_ref[0]).astype(BF16)
        cos, sin = cos_ref[rows, :], sin_ref[rows, :]
        for c in range(n // COL_CHUNK):
            lo = c * COL_CHUNK
            y = jnp.dot(h, w_ref[:, lo:lo + COL_CHUNK], preferred_element_type=F32)
            if lo < 2 * RET_QK_W:
                mul = 1.0 if lo < RET_QK_W else RET_DK ** -0.5
                for hd in range(COL_CHUNK // RET_DK):
                    x1 = y[:, hd * RET_DK:hd * RET_DK + half]
                    x2 = y[:, hd * RET_DK + half:(hd + 1) * RET_DK]
                    o1 = (x1 * cos - x2 * sin) * mul
                    o2 = (x2 * cos + x1 * sin) * mul
                    o_ref[rows, lo + hd * RET_DK:lo + hd * RET_DK + half] = o1.astype(BF16)
                    o_ref[rows, lo + hd * RET_DK + half:lo + (hd + 1) * RET_DK] = o2.astype(BF16)
            else:
                o_ref[rows, lo:lo + COL_CHUNK] = y.astype(BF16)


def _proj(kernel, x, gain, shift, scale, tables, w, seq, name, vt_channels=0):
    t, d = x.shape
    n = w.shape[1]
    tm = PROJ_SLABS * ROW_TILE
    per_seq = seq // tm
    row = lambda i: (i, 0)
    batch = lambda i: (i // per_seq, 0, 0)
    pos = lambda i: (i % per_seq, 0)
    vt_spec, vt_shape = [], []
    if vt_channels:
        vt_spec = [pl.BlockSpec((tm // ATT_TILE, vt_channels, ATT_TILE), lambda i: (i, 0, 0))]
        vt_shape = [jax.ShapeDtypeStruct((t // ATT_TILE, vt_channels, ATT_TILE), BF16)]
    return pl.pallas_call(
        kernel,
        grid=(t // tm,),
        in_specs=[
            pl.BlockSpec((tm, d), row),
            pl.BlockSpec((1, d), lambda i: (0, 0)),
            pl.BlockSpec((1, 1, d), batch),
            pl.BlockSpec((1, 1, d), batch),
            *[pl.BlockSpec((tm, LANES), pos) for _ in tables],
            _resident((d, n), lambda i: (0, 0)),
        ],
        out_specs=[pl.BlockSpec((tm, n), row)] + vt_spec,
        out_shape=[jax.ShapeDtypeStruct((t, n), BF16)] + vt_shape,
        compiler_params=_params(1),
        name=name,
    )(x, gain, shift, scale, *tables, w)


def _pair_lanes(p):
    return slice(p * LANES, (p + 1) * LANES)


def _vt(vt_ref, p, n):
    return vt_ref[n, _pair_lanes(p), :]


def _tile_masks(strict):
    t = ATT_TILE
    key = lax.broadcasted_iota(jnp.int32, (t, 2 * t), 0)
    qry = lax.broadcasted_iota(jnp.int32, (t, 2 * t), 1) % t
    chan0 = lax.broadcasted_iota(jnp.int32, (LANES, 1), 0) < HEAD_DIM
    return (key < qry) if strict else (key <= qry), chan0


def _load_super_tile(q_ref, p, i, chan0):
    t = ATT_TILE
    base = pl.multiple_of(i * 2 * t, 2 * t)
    cells = []
    for c in range(2):
        q_t = q_ref[pl.ds(base + c * t, t), _pair_lanes(p)].T
        zero = jnp.zeros_like(q_t)
        cells += [jnp.where(chan0, q_t, zero), jnp.where(chan0, zero, q_t)]
    return jnp.concatenate(cells, axis=1), jnp.concatenate(cells[2:], axis=1)


def _store_super_tile(o_ref, p, i, acc_t, chan0):
    t = ATT_TILE
    base = pl.multiple_of(i * 2 * t, 2 * t)
    for cell in range(2):
        lo = cell * 2 * t
        o_t = jnp.where(chan0, acc_t[:, lo:lo + t], acc_t[:, lo + t:lo + 2 * t])
        o_ref[pl.ds(base + cell * t, t), _pair_lanes(p)] = o_t.T.astype(BF16)


def _scores(k_ref, p, j, q_cols, blocks=1):
    t = ATT_TILE
    return jnp.dot(k_ref[pl.ds(pl.multiple_of(j * t, t), blocks * t), _pair_lanes(p)], q_cols,
                   preferred_element_type=F32)


def _stage_pair_of_blocks(stage_ref, k_ref, half, j, tiles):
    t = ATT_TILE
    for p, (q_cols, _) in enumerate(tiles):
        z = _scores(k_ref, p, j, q_cols, blocks=2)
        stage_ref[half + 2 * p] = z[:t]
        stage_ref[half + 2 * p + 1] = z[t:]


def _sb_kernel(q_ref, k_ref, vt_ref, o_ref, acc_ref, z_ref):
    t = ATT_TILE
    w = 2 * t
    seq = q_ref.shape[0]
    pairs = range(q_ref.shape[1] // LANES)
    per_half = 2 * len(pairs)
    strict, chan0 = _tile_masks(True)
    from_key = -(lax.broadcasted_iota(jnp.int32, (t, t), 1)
                 >= lax.broadcasted_iota(jnp.int32, (t, t), 0)).astype(BF16)

    def block(p, j, slot, lanes, run, mask):
        z = z_ref[slot, :, lanes]
        neg_l = jnp.maximum(z, 0.0) + jnp.log2(1.0 + jnp.exp2(-jnp.abs(z)))
        if mask is not None:
            neg_l = jnp.where(mask, neg_l, 0.0)
        since = jnp.dot(from_key, neg_l.astype(BF16), preferred_element_type=F32)
        a = jnp.exp2(z_ref[slot, :, lanes] + since)
        if mask is not None:
            a = jnp.where(mask, a, 0.0)
        part = jnp.dot(_vt(vt_ref, p, j), a.astype(BF16), preferred_element_type=F32)
        return run + since[0:1, :], part * jnp.exp2(run)

    def q_super(i, _):
        tiles = [_load_super_tile(q_ref, p, i, chan0) for p in pairs]
        mask = jnp.concatenate([strict, jnp.ones((t, w), jnp.bool_)], axis=1)
        cell1 = slice(w, 2 * w)
        diag = lax.rem(i, 2) * per_half
        for p in pairs:
            q2, qb = tiles[p]
            z_ref[diag + 2 * p + 1, :, cell1] = _scores(k_ref, p, 2 * i + 1, qb)
            z_ref[diag + 2 * p] = _scores(k_ref, p, 2 * i, q2)
        runs = []
        for p in pairs:
            run_b, part = block(p, 2 * i + 1, diag + 2 * p + 1, cell1, jnp.zeros((1, w), F32),
                                strict)
            acc_ref[p, :, w:] = part
            run = jnp.concatenate([jnp.zeros((1, w), F32), run_b], axis=1)
            run, part = block(p, 2 * i, diag + 2 * p, slice(None), run, mask)
            acc_ref[p, :, :w] = part[:, :w]
            acc_ref[p, :, w:] += part[:, w:]
            runs.append(run)

        def past(s, runs):
            j = 2 * (i - 1 - s)
            half = lax.rem(s, 2) * per_half
            _stage_pair_of_blocks(z_ref, k_ref, half, j, tiles)
            out = []
            for p in pairs:
                run, part_b = block(p, j + 1, half + 2 * p + 1, slice(None), runs[p], None)
                run, part_a = block(p, j, half + 2 * p, slice(None), run, None)
                acc_ref[p] += part_b + part_a
                out.append(run)
            return tuple(out)

        lax.fori_loop(0, i, past, tuple(runs))
        for p in pairs:
            _store_super_tile(o_ref, p, i, acc_ref[p], chan0)
        return 0

    lax.fori_loop(0, seq // w, q_super, 0)


def _pair_specs(seq, q_col, k_col, v_col):
    width = PAIRS_PER_STEP * LANES
    return [
        pl.BlockSpec((seq, width), lambda b, p: (b, q_col + p)),
        pl.BlockSpec((seq, width), lambda b, p: (b, k_col + p)),
        pl.BlockSpec((seq // ATT_TILE, width, ATT_TILE), lambda b, p: (b, v_col + p, 0)),
    ]


def _with_weight_casts(kernel, n_cast):
    def wrapped(q_ref, k_ref, v_ref, *refs):
        srcs, o_ref, dsts = refs[:n_cast], refs[n_cast], refs[n_cast + 1:2 * n_cast + 1]
        for src, dst in zip(srcs, dsts):
            dst[...] = src[...].astype(BF16)
        kernel(q_ref, k_ref, v_ref, o_ref, *refs[2 * n_cast + 1:])
    return wrapped


def _attention_call(kernel, qkv, vt, weights, batch, seq, n_heads, first_head, extra_scratch, name):
    width = PAIRS_PER_STEP * LANES
    groups = n_heads * HEAD_DIM // width
    third = ATT_W // width
    first = first_head * HEAD_DIM // width
    steps = batch * groups
    w_specs = [pl.BlockSpec((w.shape[0] // steps, w.shape[1]), lambda b, p: (b * groups + p, 0))
               for w in weights]
    out, *cast = pl.pallas_call(
        _with_weight_casts(kernel, len(weights)),
        grid=(batch, groups),
        in_specs=_pair_specs(seq, first, third + first, first) + w_specs,
        out_specs=[pl.BlockSpec((seq, width), lambda b, p: (b, p))] + w_specs,
        out_shape=[jax.ShapeDtypeStruct((batch * seq, groups * width), BF16)]
        + [jax.ShapeDtypeStruct(w.shape, BF16) for w in weights],
        scratch_shapes=[pltpu.VMEM((PAIRS_PER_STEP, LANES, 4 * ATT_TILE), F32),
                        *extra_scratch],
        compiler_params=_params(2),
        name=name,
    )(qkv, qkv, vt, *weights)
    return out, cast


def _sb_attention(qkv, vt, weights, batch, seq):
    staged = pltpu.VMEM((4 * PAIRS_PER_STEP, ATT_TILE, 4 * ATT_TILE), F32)
    return _attention_call(_sb_kernel, qkv, vt, weights, batch, seq, N_SB_HEADS, 0, [staged],
                           "sb_attention")


def _moba_kernel(q_ref, k_ref, vt_ref, o_ref, acc_ref, sel_ref, s_ref):
    t = ATT_TILE
    w = 2 * t
    seq = q_ref.shape[0]
    nb = seq // MOBA_BLOCK
    pairs = range(q_ref.shape[1] // LANES)
    per_half = 2 * len(pairs)
    causal, chan0 = _tile_masks(False)
    blk = lax.broadcasted_iota(jnp.int32, (nb, 2 * w), 0)
    cell = lax.broadcasted_iota(jnp.int32, (nb, 2 * w), 1) // w
    blk_f = blk.astype(F32)
    kmean = [(jnp.sum(k_ref[:, _pair_lanes(p)].astype(F32).reshape(nb, MOBA_BLOCK, LANES), axis=1)
              * (1.0 / MOBA_BLOCK)).astype(BF16) for p in pairs]


    def select(p, i, q2):
        gate = jnp.dot(kmean[p], q2, preferred_element_type=F32)
        past_blk = blk < 2 * i + cell
        gate = jnp.where(past_blk, gate, NEG)
        sel = jnp.zeros((nb, 2 * w), F32)
        for _r in range(MOBA_TOPK):
            top = jnp.max(gate, axis=0, keepdims=True)
            idx = jnp.min(jnp.where(gate == top, blk_f, float(nb)), axis=0, keepdims=True)
            hit = blk_f == idx
            sel = jnp.where(hit & past_blk, 1.0, sel)
            gate = jnp.where(hit, -jnp.inf, gate)
        sel_ref[p] = sel

    def own_blocks(p, i, raw_a, raw_b):
        s = jnp.where(causal, raw_b, NEG)
        m_b = jnp.max(s, axis=0, keepdims=True)
        pr = jnp.exp2(s - m_b)
        acc_ref[p, :, :w] = jnp.zeros((LANES, w), F32)
        acc_ref[p, :, w:] = jnp.dot(_vt(vt_ref, p, 2 * i + 1), pr.astype(BF16),
                                    preferred_element_type=F32)
        m = jnp.concatenate([jnp.full((1, w), -jnp.inf, F32), m_b], axis=1)
        l = jnp.concatenate([jnp.zeros((1, w), F32), jnp.sum(pr, axis=0, keepdims=True)], axis=1)
        keep = jnp.concatenate(
            [causal, jnp.broadcast_to(sel_ref[p, pl.ds(2 * i, 1), w:] > 0.0, (t, w))], axis=1)
        s = jnp.where(keep, raw_a, NEG)
        m_new = jnp.maximum(m, jnp.max(s, axis=0, keepdims=True))
        alpha = jnp.exp2(m - m_new)
        pr = jnp.exp2(s - m_new)
        acc_ref[p] = alpha * acc_ref[p] + jnp.dot(
            _vt(vt_ref, p, 2 * i), pr.astype(BF16), preferred_element_type=F32)
        return m_new, alpha * l + jnp.sum(pr, axis=0, keepdims=True)

    def past_pair(p, j, raw_a, raw_b, m_old, l_old):
        s_a = jnp.where(sel_ref[p, pl.ds(j, 1), :] > 0.0, raw_a, NEG)
        s_b = jnp.where(sel_ref[p, pl.ds(j + 1, 1), :] > 0.0, raw_b, NEG)
        m_new = jnp.maximum(m_old, jnp.maximum(jnp.max(s_a, axis=0, keepdims=True),
                                               jnp.max(s_b, axis=0, keepdims=True)))
        alpha = jnp.exp2(m_old - m_new)
        p_a = jnp.exp2(s_a - m_new)
        p_b = jnp.exp2(s_b - m_new)
        part = (jnp.dot(_vt(vt_ref, p, j), p_a.astype(BF16), preferred_element_type=F32)
                + jnp.dot(_vt(vt_ref, p, j + 1), p_b.astype(BF16), preferred_element_type=F32))
        acc_ref[p] = alpha * acc_ref[p] + part
        l_new = (alpha * l_old + jnp.sum(p_a, axis=0, keepdims=True)
                 + jnp.sum(p_b, axis=0, keepdims=True))
        return m_new, l_new

    def q_super(i, _):
        tiles = [_load_super_tile(q_ref, p, i, chan0) for p in pairs]
        own = lax.rem(i, 2) * per_half
        for p in pairs:
            q2, qb = tiles[p]
            s_ref[own + 2 * p + 1, :, w:] = _scores(k_ref, p, 2 * i + 1, qb)
            s_ref[own + 2 * p] = _scores(k_ref, p, 2 * i, q2)
            select(p, i, q2)
        stats = tuple(own_blocks(p, i, s_ref[own + 2 * p], s_ref[own + 2 * p + 1, :, w:])
                      for p in pairs)

        def past(jj, stats):
            half = lax.rem(jj, 2) * per_half
            _stage_pair_of_blocks(s_ref, k_ref, half, 2 * jj, tiles)
            return tuple(past_pair(p, 2 * jj, s_ref[half + 2 * p], s_ref[half + 2 * p + 1],
                                   *stats[p]) for p in pairs)

        stats = lax.fori_loop(0, i, past, stats)
        for p in pairs:
            _store_super_tile(o_ref, p, i, acc_ref[p] / stats[p][1], chan0)
        return 0

    lax.fori_loop(0, seq // w, q_super, 0)


def _moba_attention(qkv, vt, weights, batch, seq):
    sel = pltpu.VMEM((PAIRS_PER_STEP, seq // MOBA_BLOCK, 4 * ATT_TILE), F32)
    staged = pltpu.VMEM((4 * PAIRS_PER_STEP, ATT_TILE, 4 * ATT_TILE), F32)
    return _attention_call(_moba_kernel, qkv, vt, weights, batch, seq, N_MOBA_HEADS, N_SB_HEADS,
                           [sel, staged], "moba_attention")


def _ret_kernel(q_ref, k_ref, v_ref, g_ref, gn_ref, o_ref, state_ref, decay_ref):
    c = RET_CHUNK
    heads = range(RET_HEADS_PER_STEP)
    n_row = lax.broadcasted_iota(jnp.int32, (c, 1), 0).astype(F32)
    log_g, xi, zeta, chunk_decay = [], [], [], []
    for hh in heads:
        head = (pl.program_id(1) * RET_HEADS_PER_STEP + hh + 5).astype(F32)
        lg = jnp.log(1.0 - jnp.exp2(-jnp.full((1, 1), head, F32)))
        log_g.append(lg)
        xi.append(jnp.exp(lg * (n_row + 1.0)))
        zeta.append(jnp.exp(lg * (c - 1.0 - n_row)))
        chunk_decay.append(jnp.exp(lg * c))

    @pl.when(pl.program_id(2) == 0)
    def _():
        diff = (lax.broadcasted_iota(jnp.int32, (c, c), 0)
                - lax.broadcasted_iota(jnp.int32, (c, c), 1)).astype(F32)
        for hh in heads:
            decay_ref[hh] = jnp.where(diff >= 0, jnp.exp(log_g[hh] * jnp.maximum(diff, 0.0)), 0.0)
        state_ref[...] = jnp.zeros_like(state_ref)

    def chunk(i, _):
        rows = pl.ds(pl.multiple_of(i * c, c), c)
        for hh in heads:
            qk_cols = slice(hh * RET_DK, (hh + 1) * RET_DK)
            v_cols = slice(hh * RET_DV, (hh + 1) * RET_DV)
            q, k, v = q_ref[rows, qk_cols], k_ref[rows, qk_cols], v_ref[rows, v_cols]
            inner = lax.dot_general(q, k, (((1,), (1,)), ((), ())),
                                    preferred_element_type=F32) * decay_ref[hh]
            state = state_ref[hh]
            q_xi = (q.astype(F32) * xi[hh]).astype(BF16)
            o = (jnp.dot(inner.astype(BF16), v, preferred_element_type=F32)
                 + jnp.dot(q_xi, state.astype(BF16), preferred_element_type=F32))
            kz = (k.astype(F32) * zeta[hh]).astype(BF16)
            state_ref[hh] = state * chunk_decay[hh] + lax.dot_general(
                kz, v, (((0,), (0,)), ((), ())), preferred_element_type=F32)
            mu = jnp.mean(o, axis=-1, keepdims=True)
            var = jnp.mean(jnp.square(o - mu), axis=-1, keepdims=True)
            on = (o - mu) * lax.rsqrt(var + GN_EPS) * gn_ref[:, v_cols]
            o_ref[rows, v_cols] = (_silu(g_ref[rows, v_cols].astype(F32)) * on).astype(BF16)
        return 0

    lax.fori_loop(0, q_ref.shape[0] // c, chunk, 0)


def _retention(proj, gn_gain, batch, seq):
    hps = RET_HEADS_PER_STEP
    tile = RET_SEQ_TILE
    tiles = seq // tile
    qk_w, v_w = hps * RET_DK, hps * RET_DV
    k0 = RET_QK_W // qk_w
    v0 = 2 * RET_QK_W // v_w
    g0 = v0 + RET_V_W // v_w
    row = lambda b, h, s: b * tiles + s
    return pl.pallas_call(
        _ret_kernel,
        grid=(batch, N_RET_HEADS // hps, tiles),
        in_specs=[
            pl.BlockSpec((tile, qk_w), lambda b, h, s: (row(b, h, s), h)),
            pl.BlockSpec((tile, qk_w), lambda b, h, s: (row(b, h, s), k0 + h)),
            pl.BlockSpec((tile, v_w), lambda b, h, s: (row(b, h, s), v0 + h)),
            pl.BlockSpec((tile, v_w), lambda b, h, s: (row(b, h, s), g0 + h)),
            pl.BlockSpec((1, v_w), lambda b, h, s: (0, h)),
        ],
        out_specs=pl.BlockSpec((tile, v_w), lambda b, h, s: (row(b, h, s), h)),
        out_shape=jax.ShapeDtypeStruct((batch * seq, RET_V_W), BF16),
        scratch_shapes=[pltpu.VMEM((hps, RET_DK, RET_DV), F32),
                        pltpu.VMEM((hps, RET_CHUNK, RET_CHUNK), F32)],
        compiler_params=_params(3),
        name="retention",
    )(proj, proj, proj, proj, gn_gain)


def _mix_ffn_kernel(*refs, n_in, final_norm):
    x_ref, gate_m_ref, g_ref, sh_ref, sc_ref, gate_f_ref = refs[:6]
    a_refs, w_refs = refs[6:6 + n_in], refs[6 + n_in:6 + 2 * n_in]
    wgu_ref, wd_ref, fin_ref, o_ref = refs[6 + 2 * n_in:]
    for r in range(x_ref.shape[0] // ROW_TILE):
        rows = slice(r * ROW_TILE, (r + 1) * ROW_TILE)
        y = jnp.dot(a_refs[0][rows, :], w_refs[0][...], preferred_element_type=F32)
        for a_ref, w_ref in zip(a_refs[1:], w_refs[1:]):
            y += jnp.dot(a_ref[rows, :], w_ref[...], preferred_element_type=F32)
        x = x_ref[rows, :] + gate_m_ref[0] * y
        h = _norm_mod(x, g_ref[...], sh_ref[0], sc_ref[0]).astype(BF16)
        y = None
        for lo, hi in FF_CHUNKS:
            gt = jnp.dot(h, wgu_ref[:, lo:hi], preferred_element_type=F32)
            up = jnp.dot(h, wgu_ref[:, D_FF + lo:D_FF + hi], preferred_element_type=F32)
            part = jnp.dot((_silu(gt) * up).astype(BF16), wd_ref[lo:hi, :],
                           preferred_element_type=F32)
            y = part if y is None else y + part
        out = x + gate_f_ref[0] * y
        if final_norm:
            out = (out * lax.rsqrt(jnp.mean(out * out, axis=-1, keepdims=True) + NORM_EPS)
                   * fin_ref[...])
        o_ref[rows, :] = out


def _mix_ffn(x, gate_m, acts, w_outs, gain, shift, scale, gate_f, w_gate_up, w_down, layer,
             fin_gain, seq, final_norm, name):
    t, d = x.shape
    tm = MIX_SLABS * ROW_TILE
    per_seq = seq // tm
    row = lambda i: (i, 0)
    batch = lambda i: (i // per_seq, 0, 0)
    const = lambda i: (0, 0)
    this_layer = lambda i: (layer, 0, 0)
    return pl.pallas_call(
        functools.partial(_mix_ffn_kernel, n_in=len(acts), final_norm=final_norm),
        grid=(t // tm,),
        in_specs=[
            pl.BlockSpec((tm, d), row),
            pl.BlockSpec((1, 1, d), batch),
            pl.BlockSpec((1, d), const),
            pl.BlockSpec((1, 1, d), batch),
            pl.BlockSpec((1, 1, d), batch),
            pl.BlockSpec((1, 1, d), batch),
            *[pl.BlockSpec((tm, a.shape[1]), row) for a in acts],
            *[_resident(w.shape, const) for w in w_outs],
            _resident((None,) + w_gate_up.shape[1:], this_layer),
            _resident((None,) + w_down.shape[1:], this_layer),
            pl.BlockSpec((1, d), const),
        ],
        out_specs=pl.BlockSpec((tm, d), row),
        out_shape=jax.ShapeDtypeStruct((t, d), F32),
        compiler_params=_params(1),
        name=name,
    )(x, gate_m, gain, shift, scale, gate_f, *acts, *w_outs, w_gate_up, w_down, fin_gain)


def _rope_tables(seq, dim, tile):
    inv = ROPE_THETA ** (-jnp.arange(0, dim, 2, dtype=F32) / dim)
    hi = (jnp.arange(seq // ROPE_SIDE, dtype=jnp.int32) * ROPE_SIDE).astype(F32)[:, None] * inv
    lo = jnp.arange(ROPE_SIDE, dtype=jnp.int32).astype(F32)[:, None] * inv
    ch, sh, cl, sl = jnp.cos(hi)[:, None], jnp.sin(hi)[:, None], jnp.cos(lo)[None], jnp.sin(lo)[None]
    cos = (ch * cl - sh * sl).reshape(seq, dim // 2)
    sin = (sh * cl + ch * sl).reshape(seq, dim // 2)
    return jnp.tile(cos, (1, tile)), jnp.tile(sin, (1, tile))


def kernel(x, c, ada_w, ada_b, norm_gains, att_w_qkv, att_w_o, ret_w_in, ret_gn, ret_w_o,
           ffn_w_gate_up, ffn_w_down, final_norm):
    batch, seq, d = x.shape
    depth = ada_w.shape[0]
    assert d == D_MODEL and batch <= SUBLANES
    assert seq % (PROJ_SLABS * ROW_TILE) == 0 and seq % (MIX_SLABS * ROW_TILE) == 0
    assert seq % (2 * ATT_TILE) == 0 and seq % RET_SEQ_TILE == 0

    mod = _ada_mod(c, ada_w, ada_b)[:, :batch].reshape(depth, batch, 1, 6, d)
    xt = x.reshape(batch * seq, d)

    half = HEAD_DIM // 2
    cos_a, sin_a = _rope_tables(seq, HEAD_DIM, LANES // half)
    first_half = (jnp.arange(LANES) % HEAD_DIM) < half
    att_tables = (cos_a, jnp.where(first_half, -sin_a, 0.0), jnp.where(first_half, 0.0, sin_a))
    ret_tables = _rope_tables(seq, RET_DK, 1)
    assert depth == 2 and att_w_qkv.shape[0] == 1 and ret_w_in.shape[0] == 1
    rows = lambda w: w.reshape(-1, w.shape[-1])

    for layer in range(depth):
        shift_m, scale_m, gate_m, shift_f, scale_f, gate_f = (mod[layer, :, :, i] for i in range(6))
        gains = norm_gains[layer]
        if layer % 2 == 0:
            qkv, vt = _proj(_att_proj_kernel, xt, gains[0:1], shift_m, scale_m, att_tables,
                            att_w_qkv[0].astype(BF16), seq, "att_proj", vt_channels=ATT_W)
            sb, (w_gate_up, w_ret_o, w_att_o) = _sb_attention(
                qkv, vt, [rows(ffn_w_gate_up), rows(ret_w_o), rows(att_w_o)], batch, seq)
            mb, (w_down, w_ret_in) = _moba_attention(
                qkv, vt, [rows(ffn_w_down), rows(ret_w_in)], batch, seq)
            w_gate_up = w_gate_up.reshape(ffn_w_gate_up.shape)
            w_down = w_down.reshape(ffn_w_down.shape)
            split = N_SB_HEADS * HEAD_DIM
            acts, w_outs = (sb, mb), (w_att_o[:split], w_att_o[split:])
        else:
            proj, = _proj(_ret_proj_kernel, xt, gains[0:1], shift_m, scale_m, ret_tables,
                          w_ret_in, seq, "ret_proj")
            acts = (_retention(proj, ret_gn[0].reshape(1, RET_V_W), batch, seq),)
            w_outs = (w_ret_o,)
        xt = _mix_ffn(xt, gate_m, acts, w_outs, gains[1:2], shift_f, scale_f, gate_f, w_gate_up,
                      w_down, layer, final_norm.reshape(1, d), seq, layer == depth - 1,
                      "mix_ffn%d" % layer)
    return xt.reshape(batch, seq, d)
```
